```python
import math
import jax, jax.numpy as jnp
from jax import lax
import numpy as np

D_MODEL = 1024
BATCH = 16
SEQ = 4096
DEPTH = 2
DEC_BATCH = 32
DEC_SEQ = 64
PAST_LEN = 1024

CHUNK = 64
N_EVEN = (DEPTH + 1) // 2
N_ODD = DEPTH // 2
MIX_WIDTH = D_MODEL
RET_HEADS = 8
RET_DK = 32
RET_DV = 64
RET_THETA = 10000.0
DSA_HEADS = 8
DSA_KV_HEADS = 2
DSA_HD = 64
DSA_ROT = DSA_HD // 4
IDX_HEADS = 4
IDX_DIM = 64
IDX_ROT = IDX_DIM // 4
DSA_TOPK_MAX = 256
Q_BLOCK = 128
ROPE_THETA = 500000.0
BAND_HEADS = 8
BAND_HD = 64
BAND_PREV = 8
BAND_PAST = BAND_PREV * CHUNK
REL_CLIP = 256
SSD_HEADS = 8
SSD_HD = 64
SSD_GROUPS = 2
SSD_STATE = 128
SSD_CONV = 4
SSD_INNER = SSD_HEADS * SSD_HD
SSD_CONV_DIM = SSD_INNER + 2 * SSD_GROUPS * SSD_STATE
N_EXPERTS = 32
TOP_K = 4
D_FF = 1024
SWIGLU_LIMIT = 7.0
SWIGLU_ALPHA = 1.702
MOE_BLOCK = 128
DN_ALPHA = (2 * DEPTH) ** 0.25
DN_BETA = (8 * DEPTH) ** -0.25
LN_EPS = 1e-5

EVEN_SIZES = (RET_HEADS * RET_DK, RET_HEADS * RET_DK, RET_HEADS * RET_DV, RET_HEADS * RET_DV,
              DSA_HEADS * DSA_HD, DSA_KV_HEADS * DSA_HD, DSA_KV_HEADS * DSA_HD,
              IDX_HEADS * IDX_DIM, IDX_DIM, IDX_HEADS)
EVEN_CUTS = tuple(int(c) for c in np.cumsum(EVEN_SIZES)[:-1])
EVEN_IN = sum(EVEN_SIZES)
ODD_SIZES = (BAND_HEADS * BAND_HD, BAND_HEADS * BAND_HD, BAND_HEADS * BAND_HD,
             SSD_INNER, SSD_CONV_DIM, SSD_HEADS)
ODD_CUTS = tuple(int(c) for c in np.cumsum(ODD_SIZES)[:-1])
ODD_IN = sum(ODD_SIZES)

kernel_name = 'hybrid_streaming_retention_dsa_band_ssd_moe_step'


def layer_norm(x, g, b):
    xf = x.astype(jnp.float32)
    mu = jnp.mean(xf, axis=-1, keepdims=True)
    var = jnp.mean(jnp.square(xf - mu), axis=-1, keepdims=True)
    return ((xf - mu) * lax.rsqrt(var + LN_EPS) * g + b).astype(x.dtype)


def rms_norm(x, g):
    xf = x.astype(jnp.float32)
    return (xf * lax.rsqrt(jnp.mean(jnp.square(xf), axis=-1, keepdims=True) + LN_EPS) * g).astype(x.dtype)


def rope(x, pos, rot_dim, theta):
    half = rot_dim // 2
    inv = theta ** (-jnp.arange(half, dtype=jnp.float32) / half)
    ang = pos.astype(jnp.float32)[:, None] * inv[None, :]
    cos = jnp.cos(ang)[:, None, :].astype(x.dtype)
    sin = jnp.sin(ang)[:, None, :].astype(x.dtype)
    x1, x2, rest = x[..., :half], x[..., half:rot_dim], x[..., rot_dim:]
    return jnp.concatenate([x1 * cos - x2 * sin, x2 * cos + x1 * sin, rest], axis=-1)


def retention(q, k, v, s0):
    Bx, T, H, dk = q.shape
    Lc = min(CHUNK, T)
    nc = T // Lc
    log_g = jnp.log(1.0 - 2.0 ** (-5.0 - jnp.arange(H, dtype=jnp.float32)))
    n = jnp.arange(Lc, dtype=jnp.float32)
    diff = n[:, None] - n[None, :]
    dmask = jnp.where(diff >= 0, jnp.exp(jnp.maximum(diff, 0.0)[None] * log_g[:, None, None]), 0.0)
    qc = q.reshape(Bx, nc, Lc, H, dk)
    kc = k.reshape(Bx, nc, Lc, H, dk)
    vc = v.reshape(Bx, nc, Lc, H, -1)
    att = jnp.einsum('bcnhd,bcmhd->bchnm', qc, kc) * dmask
    y = jnp.einsum('bchnm,bcmhe->bcnhe', att, vc)
    w_end = jnp.exp((Lc - 1 - n)[:, None] * log_g[None, :])
    kv = jnp.einsum('bcmhd,bcmhe->bchde', kc * w_end[:, :, None], vc)
    g_chunk = jnp.exp(Lc * log_g)[:, None, None]

    def step(s, kv_c):
        return (s * g_chunk + kv_c).astype(s.dtype), s

    s_last, s_before = lax.scan(step, s0, jnp.moveaxis(kv, 1, 0))
    xi = jnp.exp((n + 1.0)[:, None] * log_g[None, :])
    y = y + jnp.einsum('bcnhd,cbhde->bcnhe', qc * xi[:, :, None], s_before)
    return y.reshape(Bx, T, H, -1), s_last


def dsa_attend(q, iq, iw, qlim, K, V, IK, kpos, topk):
    s = jnp.einsum('bqhd,bsd->bqhs', iq, IK).astype(jnp.float32)
    score = jnp.einsum('bqh,bqhs->bqs', iw.astype(jnp.float32), jax.nn.relu(s)) * (IDX_HEADS * IDX_DIM) ** -0.5
    adm = kpos[None, :] < qlim[:, None]
    score = jnp.where(adm[None], score, -jnp.inf)
    _, idx = lax.top_k(score, topk)
    ok = kpos[idx] < qlim[None, :, None]
    gather = jax.vmap(lambda a, i: a[i])
    Ks = gather(K, idx)
    Vs = gather(V, idx)
    Bx, Q, H, d = q.shape
    qg = q.reshape(Bx, Q, DSA_KV_HEADS, H // DSA_KV_HEADS, d)
    logits = jnp.einsum('bqngd,bqknd->bqngk', qg, Ks).astype(jnp.float32) * d ** -0.5
    logits = jnp.where(ok[:, :, None, None, :], logits, -1e30)
    p = jax.nn.softmax(logits, axis=-1).astype(Vs.dtype)
    o = jnp.einsum('bqngk,bqknd->bqngd', p, Vs)
    return o.reshape(Bx, Q, H * d)


def dsa_prompt(q, k, v, iq, ik, iw):
    Bx, T = q.shape[:2]
    nb = T // Q_BLOCK
    topk = min(DSA_TOPK_MAX, T // 4)
    kpos = jnp.arange(T, dtype=jnp.int32)
    qlim = ((kpos // CHUNK) + 1) * CHUNK

    def blk(a):
        return jnp.moveaxis(a.reshape((Bx, nb, Q_BLOCK) + a.shape[2:]), 1, 0)

    out = lax.map(lambda a: dsa_attend(a[0], a[1], a[2], a[3], k, v, ik, kpos, topk),
                  (blk(q), blk(iq), blk(iw), qlim.reshape(nb, Q_BLOCK)))
    return jnp.moveaxis(out, 0, 1).reshape(Bx, T, -1)


def band_attend(q, k, v, qpos, kpos, kvalid, rel_bias):
    logits = jnp.einsum('bqhd,bkhd->bhqk', q, k).astype(jnp.float32) * BAND_HD ** -0.5
    rel = jnp.clip(qpos[:, None] - kpos[None, :], -REL_CLIP, REL_CLIP) + REL_CLIP
    logits = logits + rel_bias[:, rel].astype(jnp.float32)[None]
    if kvalid is not None:
        logits = jnp.where(kvalid[None, None, None, :], logits, -1e30)
    p = jax.nn.softmax(logits, axis=-1).astype(v.dtype)
    return jnp.einsum('bhqk,bkhd->bqhd', p, v)


def band_prompt(q, k, v, rel_bias):
    Bx, T, H, d = q.shape
    nc = T // CHUNK
    width = BAND_PAST + CHUNK
    kp = jnp.pad(k, ((0, 0), (BAND_PAST, 0), (0, 0), (0, 0)))
    vp = jnp.pad(v, ((0, 0), (BAND_PAST, 0), (0, 0), (0, 0)))

    def one(c):
        start = c * CHUNK
        qc = lax.dynamic_slice_in_dim(q, start, CHUNK, axis=1)
        kc = lax.dynamic_slice_in_dim(kp, start, width, axis=1)
        vc = lax.dynamic_slice_in_dim(vp, start, width, axis=1)
        qpos = start + jnp.arange(CHUNK, dtype=jnp.int32)
        kpos = start - BAND_PAST + jnp.arange(width, dtype=jnp.int32)
        return band_attend(qc, kc, vc, qpos, kpos, kpos >= 0, rel_bias)

    out = lax.map(one, jnp.arange(nc, dtype=jnp.int32))
    return jnp.moveaxis(out, 0, 1).reshape(Bx, T, H * d)


def ssd_scan(x, dt, A, Bm, Cm, d_skip, h0):
    Bx, T = x.shape[:2]
    Lc = min(CHUNK, T)
    nc = T // Lc

    def chunks(a):
        return a.reshape((Bx, nc, Lc) + a.shape[2:])

    xc, dtc, Bc, Cc = chunks(x), chunks(dt), chunks(Bm), chunks(Cm)
    acum = jnp.cumsum((dtc * A).astype(jnp.float32), axis=2)
    causal = jnp.tril(jnp.ones((Lc, Lc), bool))[None, None, :, :, None, None]
    seg = acum[:, :, :, None] - acum[:, :, None, :]
    lmat = jnp.exp(jnp.where(causal, seg, -jnp.inf))
    cb = jnp.einsum('bcngs,bcmgs->bcnmg', Cc, Bc)
    xdt = xc * dtc[..., None]
    y = jnp.einsum('bcnmgj,bcmgjp->bcngjp', cb[..., None] * lmat, xdt)
    decay_end = jnp.exp(acum[:, :, -1:] - acum)
    st = jnp.einsum('bcmgjp,bcmgs->bcgjps', xdt * decay_end[..., None], Bc)
    chunk_dec = jnp.exp(acum[:, :, -1])

    def step(h, inp):
        s_c, d_c = inp
        return (h * d_c[..., None, None] + s_c).astype(h.dtype), h

    h_last, h_before = lax.scan(step, h0, (jnp.moveaxis(st, 1, 0), jnp.moveaxis(chunk_dec, 1, 0)))
    h_before = jnp.moveaxis(h_before, 0, 1)
    y = y + jnp.einsum('bcngs,bcgjps->bcngjp', Cc, h_before) * jnp.exp(acum)[..., None]
    y = y + d_skip[:, :, None] * xc
    return y.reshape(x.shape), h_last


def even_mixer(x, pos, ret_s0, dsa_past, w_in, w_out, gn_g, gn_b):
    Bx, T, _ = x.shape
    qa, ka, va, ga, qb, kb, vb, iq, ik, iw = jnp.split(x @ w_in, EVEN_CUTS, axis=-1)
    qa = rope(qa.reshape(Bx, T, RET_HEADS, RET_DK), pos, RET_DK, RET_THETA)
    ka = rope(ka.reshape(Bx, T, RET_HEADS, RET_DK), pos, RET_DK, RET_THETA) * RET_DK ** -0.5
    ya, ret_s = retention(qa, ka, va.reshape(Bx, T, RET_HEADS, RET_DV), ret_s0)
    ya = jax.nn.silu(ga) * layer_norm(ya, gn_g, gn_b).reshape(Bx, T, -1)
    qb = rope(qb.reshape(Bx, T, DSA_HEADS, DSA_HD), pos, DSA_ROT, ROPE_THETA)
    kb = rope(kb.reshape(Bx, T, DSA_KV_HEADS, DSA_HD), pos, DSA_ROT, ROPE_THETA)
    vb = vb.reshape(Bx, T, DSA_KV_HEADS, DSA_HD)
    iq = rope(iq.reshape(Bx, T, IDX_HEADS, IDX_DIM), pos, IDX_ROT, ROPE_THETA)
    ik = rope(ik[:, :, None, :], pos, IDX_ROT, ROPE_THETA)[:, :, 0]
    if dsa_past is None:
        yb = dsa_prompt(qb, kb, vb, iq, ik, iw)
    else:
        pk, pv, pik = dsa_past
        K = jnp.concatenate([pk, kb], axis=1)
        V = jnp.concatenate([pv, vb], axis=1)
        IK = jnp.concatenate([pik, ik], axis=1)
        L = K.shape[1]
        yb = dsa_attend(qb, iq, iw, jnp.full((T,), L, jnp.int32), K, V, IK,
                        jnp.arange(L, dtype=jnp.int32), min(DSA_TOPK_MAX, L // 4))
    y = jnp.concatenate([ya, yb.astype(ya.dtype)], axis=-1) @ w_out
    return y, ret_s, kb, vb, ik


def odd_mixer(x, pos, band_past, h0, conv_buf, w_in, w_out, rel_bias, conv_w, conv_b,
              dt_bias, a_log, d_skip, norm_g):
    Bx, T, _ = x.shape
    qc, kc, vc, z, xbc, dt = jnp.split(x @ w_in, ODD_CUTS, axis=-1)
    shp = (Bx, T, BAND_HEADS, BAND_HD)
    qc, kc, vc = qc.reshape(shp), kc.reshape(shp), vc.reshape(shp)
    if band_past is None:
        yc = band_prompt(qc, kc, vc, rel_bias)
    else:
        pk, pv = band_past
        cl = pk.shape[1]
        kpos = jnp.concatenate([pos[0] - cl + jnp.arange(cl, dtype=jnp.int32), pos])
        K = jnp.concatenate([pk, kc], axis=1)
        V = jnp.concatenate([pv, vc], axis=1)
        yc = band_attend(qc, K, V, pos, kpos, None, rel_bias).reshape(Bx, T, -1)
    xpad = jnp.concatenate([conv_buf, xbc], axis=1)
    conv = conv_b + sum(xpad[:, j:j + T] * conv_w[j] for j in range(SSD_CONV))
    new_buf = xpad[:, -(SSD_CONV - 1):]
    u = jax.nn.silu(conv)
    xs, bm, cm = jnp.split(u, (SSD_INNER, SSD_INNER + SSD_GROUPS * SSD_STATE), axis=-1)
    J = SSD_HEADS // SSD_GROUPS
    xs = xs.reshape(Bx, T, SSD_GROUPS, J, SSD_HD)
    bm = bm.reshape(Bx, T, SSD_GROUPS, SSD_STATE)
    cm = cm.reshape(Bx, T, SSD_GROUPS, SSD_STATE)
    dtv = jax.nn.softplus((dt + dt_bias).astype(jnp.float32)).reshape(Bx, T, SSD_GROUPS, J)
    A = -jnp.exp(a_log.astype(jnp.float32)).reshape(SSD_GROUPS, J)
    yd, h_last = ssd_scan(xs, dtv, A, bm, cm, d_skip.reshape(SSD_GROUPS, J),
                          h0.reshape(Bx, SSD_GROUPS, J, SSD_HD, SSD_STATE))
    yd = yd.reshape(Bx, T, SSD_GROUPS, J * SSD_HD) * jax.nn.silu(z).reshape(Bx, T, SSD_GROUPS, -1)
    yd = rms_norm(yd, norm_g.reshape(SSD_GROUPS, -1)).reshape(Bx, T, -1)
    y = jnp.concatenate([yc, yd.astype(yc.dtype)], axis=-1) @ w_out
    return y, kc, vc, h_last.reshape(Bx, SSD_HEADS, SSD_HD, SSD_STATE), new_buf


def moe(x, w_r, b_r, w_gu, b_gu, w_dn, b_dn):
    shape = x.shape
    xt = x.reshape(-1, shape[-1])
    n_tok = xt.shape[0]
    n_pair = n_tok * TOP_K
    logits = (xt @ w_r + b_r).astype(jnp.float32)
    top_v, top_i = lax.top_k(logits, TOP_K)
    gate = jax.nn.softmax(top_v, axis=-1).reshape(-1)
    e_flat = top_i.reshape(-1)
    order = jnp.argsort(e_flat, stable=True)
    e_s = e_flat[order]
    tok_s = order // TOP_K
    g_s = gate[order]
    counts = jnp.bincount(e_flat, length=N_EXPERTS)
    padded = (counts + MOE_BLOCK - 1) // MOE_BLOCK * MOE_BLOCK
    pad_end = jnp.cumsum(padded)
    rank = jnp.arange(n_pair) - (jnp.cumsum(counts) - counts)[e_s]
    dest = (pad_end - padded)[e_s] + rank
    n_rows = -(-n_pair // MOE_BLOCK) * MOE_BLOCK + N_EXPERTS * MOE_BLOCK
    n_blk = n_rows // MOE_BLOCK
    buf = jnp.zeros((n_rows, shape[-1]), xt.dtype).at[dest].set(xt[tok_s])
    blk_e = jnp.minimum(jnp.searchsorted(pad_end, jnp.arange(n_blk) * MOE_BLOCK, side='right'), N_EXPERTS - 1)

    def expert_block(args):
        xb, e = args
        h = xb @ w_gu[e] + b_gu[e]
        g = jnp.minimum(h[:, :D_FF], SWIGLU_LIMIT)
        up = jnp.clip(h[:, D_FF:], -SWIGLU_LIMIT, SWIGLU_LIMIT)
        return ((up + 1.0) * g * jax.nn.sigmoid(SWIGLU_ALPHA * g)) @ w_dn[e] + b_dn[e]

    ybuf = lax.map(expert_block, (buf.reshape(n_blk, MOE_BLOCK, -1), blk_e)).reshape(n_rows, -1)
    y = ybuf[dest] * g_s[:, None].astype(ybuf.dtype)
    out = jnp.zeros_like(xt).at[tok_s].add(y.astype(xt.dtype))
    return out.reshape(shape)


def setup_inputs(seed: int = 0) -> dict:
    key = jax.random.key(seed)
    keys = iter(jax.random.split(key, 48))

    def nrm(shape, scale):
        return jax.random.normal(next(keys), shape, jnp.float32) * scale

    band_len = min(BAND_PAST, PAST_LEN)
    dt0 = jnp.exp(jax.random.uniform(next(keys), (N_ODD, SSD_HEADS), jnp.float32, math.log(1e-3), math.log(1e-1)))
    a0 = jax.random.uniform(next(keys), (N_ODD, SSD_HEADS), jnp.float32, 1.0, 16.0)
    return {
        'x_prompt': nrm((BATCH, SEQ, D_MODEL), 1.0),
        'x_sample': nrm((DEC_BATCH, DEC_SEQ, D_MODEL), 1.0),
        'state_ret': nrm((N_EVEN, DEC_BATCH, RET_HEADS, RET_DK, RET_DV), 0.5),
        'cache_dsa_k': nrm((N_EVEN, DEC_BATCH, PAST_LEN, DSA_KV_HEADS, DSA_HD), 1.0),
        'cache_dsa_v': nrm((N_EVEN, DEC_BATCH, PAST_LEN, DSA_KV_HEADS, DSA_HD), 1.0),
        'cache_dsa_kidx': nrm((N_EVEN, DEC_BATCH, PAST_LEN, IDX_DIM), 1.0),
        'cache_band_k': nrm((N_ODD, DEC_BATCH, band_len, BAND_HEADS, BAND_HD), 1.0),
        'cache_band_v': nrm((N_ODD, DEC_BATCH, band_len, BAND_HEADS, BAND_HD), 1.0),
        'state_ssm': nrm((N_ODD, DEC_BATCH, SSD_HEADS, SSD_HD, SSD_STATE), 0.1),
        'state_conv': nrm((N_ODD, DEC_BATCH, SSD_CONV - 1, SSD_CONV_DIM), 1.0),
        'e_w_in': nrm((N_EVEN, D_MODEL, EVEN_IN), D_MODEL ** -0.5),
        'e_w_out': nrm((N_EVEN, MIX_WIDTH, D_MODEL), MIX_WIDTH ** -0.5 * DN_BETA),
        'e_gn_g': 1.0 + nrm((N_EVEN, RET_HEADS, RET_DV), 0.02),
        'e_gn_b': nrm((N_EVEN, RET_HEADS, RET_DV), 0.02),
        'o_w_in': nrm((N_ODD, D_MODEL, ODD_IN), D_MODEL ** -0.5),
        'o_w_out': nrm((N_ODD, MIX_WIDTH, D_MODEL), MIX_WIDTH ** -0.5 * DN_BETA),
        'o_rel_bias': nrm((N_ODD, BAND_HEADS, 2 * REL_CLIP + 1), 0.2),
        'o_conv_w': nrm((N_ODD, SSD_CONV, SSD_CONV_DIM), SSD_CONV ** -0.5),
        'o_conv_b': nrm((N_ODD, SSD_CONV_DIM), 0.02),
        'o_dt_bias': dt0 + jnp.log(-jnp.expm1(-dt0)),
        'o_a_log': jnp.log(a0),
        'o_d_skip': 1.0 + nrm((N_ODD, SSD_HEADS), 0.02),
        'o_norm_g': 1.0 + nrm((N_ODD, SSD_INNER), 0.02),
        'ln1_g': 1.0 + nrm((DEPTH, D_MODEL), 0.02),
        'ln1_b': nrm((DEPTH, D_MODEL), 0.02),
        'ln2_g': 1.0 + nrm((DEPTH, D_MODEL), 0.02),
        'ln2_b': nrm((DEPTH, D_MODEL), 0.02),
        'router_w': nrm((DEPTH, D_MODEL, N_EXPERTS), D_MODEL ** -0.5),
        'router_b': nrm((DEPTH, N_EXPERTS), 0.01),
        'exp_w_gu': nrm((DEPTH, N_EXPERTS, D_MODEL, 2 * D_FF), D_MODEL ** -0.5),
        'exp_b_gu': nrm((DEPTH, N_EXPERTS, 2 * D_FF), 0.01),
        'exp_w_dn': nrm((DEPTH, N_EXPERTS, D_FF, D_MODEL), D_FF ** -0.5 * DN_BETA),
        'exp_b_dn': nrm((DEPTH, N_EXPERTS, D_MODEL), 0.01),
    }


def reference(x_prompt, x_sample, state_ret, cache_dsa_k, cache_dsa_v, cache_dsa_kidx,
              cache_band_k, cache_band_v, state_ssm, state_conv,
              e_w_in, e_w_out, e_gn_g, e_gn_b,
              o_w_in, o_w_out, o_rel_bias, o_conv_w, o_conv_b, o_dt_bias, o_a_log, o_d_skip, o_norm_g,
              ln1_g, ln1_b, ln2_g, ln2_b, router_w, router_b, exp_w_gu, exp_b_gu, exp_w_dn, exp_b_dn):
    Bp, Tp, _ = x_prompt.shape
    Bs, Ts, _ = x_sample.shape
    past = cache_dsa_k.shape[2]
    pos_p = jnp.arange(Tp, dtype=jnp.int32)
    pos_s = past + jnp.arange(Ts, dtype=jnp.int32)
    hp, hs = x_prompt, x_sample
    rp, dkp, dvp, dip, bkp, bvp, sp, cp = [], [], [], [], [], [], [], []
    rs, dks, dvs, dis, bks, bvs, ss, cs = [], [], [], [], [], [], [], []
    for layer in range(DEPTH):
        i = layer // 2
        if layer % 2 == 0:
            zero_ret = jnp.zeros((Bp, RET_HEADS, RET_DK, RET_DV), hp.dtype)
            mp, s1, k1, v1, ik1 = even_mixer(hp, pos_p, zero_ret, None, e_w_in[i], e_w_out[i], e_gn_g[i], e_gn_b[i])
            ms, s2, k2, v2, ik2 = even_mixer(hs, pos_s, state_ret[i],
                                             (cache_dsa_k[i], cache_dsa_v[i], cache_dsa_kidx[i]),
                                             e_w_in[i], e_w_out[i], e_gn_g[i], e_gn_b[i])
            rp.append(s1); dkp.append(k1); dvp.append(v1); dip.append(ik1)
            rs.append(s2); dks.append(k2); dvs.append(v2); dis.append(ik2)
        else:
            zero_h = jnp.zeros((Bp, SSD_HEADS, SSD_HD, SSD_STATE), hp.dtype)
            zero_c = jnp.zeros((Bp, SSD_CONV - 1, SSD_CONV_DIM), hp.dtype)
            odd_w = (o_w_in[i], o_w_out[i], o_rel_bias[i], o_conv_w[i], o_conv_b[i],
                     o_dt_bias[i], o_a_log[i], o_d_skip[i], o_norm_g[i])
            mp, k1, v1, h1, c1 = odd_mixer(hp, pos_p, None, zero_h, zero_c, *odd_w)
            ms, k2, v2, h2, c2 = odd_mixer(hs, pos_s, (cache_band_k[i], cache_band_v[i]),
                                           state_ssm[i], state_conv[i], *odd_w)
            keep = min(BAND_PAST, Tp)
            bkp.append(k1[:, -keep:]); bvp.append(v1[:, -keep:]); sp.append(h1); cp.append(c1)
            bks.append(k2); bvs.append(v2); ss.append(h2); cs.append(c2)
        hp = layer_norm(DN_ALPHA * hp + mp.astype(hp.dtype), ln1_g[layer], ln1_b[layer])
        hs = layer_norm(DN_ALPHA * hs + ms.astype(hs.dtype), ln1_g[layer], ln1_b[layer])
        moe_w = (router_w[layer], router_b[layer], exp_w_gu[layer], exp_b_gu[layer], exp_w_dn[layer], exp_b_dn[layer])
        hp = layer_norm(DN_ALPHA * hp + moe(hp, *moe_w), ln2_g[layer], ln2_b[layer])
        hs = layer_norm(DN_ALPHA * hs + moe(hs, *moe_w), ln2_g[layer], ln2_b[layer])
    return (hp, hs,
            jnp.stack(rp), jnp.stack(dkp), jnp.stack(dvp), jnp.stack(dip),
            jnp.stack(bkp), jnp.stack(bvp), jnp.stack(sp), jnp.stack(cp),
            jnp.stack(rs), jnp.stack(dks), jnp.stack(dvs), jnp.stack(dis),
            jnp.stack(bks), jnp.stack(bvs), jnp.stack(ss), jnp.stack(cs))
```

```python
import functools
import math

import jax
import jax.numpy as jnp
import numpy as np
from jax import lax
from jax.experimental import pallas as pl
from jax.experimental.pallas import tpu as pltpu

F32 = jnp.float32
BF16 = jnp.bfloat16
I32 = jnp.int32

D_MODEL = 1024
CHUNK = 64
RET_HEADS, RET_DK, RET_DV, RET_THETA = 8, 32, 64, 10000.0
DSA_HEADS, DSA_KV_HEADS, DSA_HD = 8, 2, 64
DSA_ROT = DSA_HD // 4
IDX_HEADS, IDX_DIM = 4, 64
DSA_TOPK_MAX = 256
ROPE_THETA = 500000.0
BAND_HEADS, BAND_HD, BAND_PREV = 8, 64, 8
BAND_PAST = BAND_PREV * CHUNK
REL_CLIP = 256
SSD_HEADS, SSD_HD, SSD_GROUPS, SSD_STATE, SSD_CONV = 8, 64, 2, 128, 4
SSD_INNER = SSD_HEADS * SSD_HD
N_EXPERTS, TOP_K, D_FF = 32, 4, 1024
SWIGLU_LIMIT, SWIGLU_ALPHA = 7.0, 1.702
DEPTH = 2
DN_ALPHA = (2 * DEPTH) ** 0.25
LN_EPS = 1e-5

LANE = 128
VMEM_LIMIT = 56 * 1024 * 1024
INT_MIN = -(2 ** 31)
NEG_BIG = -1e30

EVEN_IN = 2628
EVEN_W = 2688
ODD_IN = 3080
ODD_W = 3200

MOE_TM = 256
RET_LC = 256
SSD_LC = 256
DSA_TQ = 128
DSA_TK = 256
BAND_TQ = 256


def _cparams(n_axes):
    return pltpu.CompilerParams(dimension_semantics=("arbitrary",) * n_axes,
                                vmem_limit_bytes=VMEM_LIMIT)


def _dot(a, b):
    return jnp.dot(a, b, preferred_element_type=F32)


def _dot_nt(a, b):
    return lax.dot_general(a, b, (((1,), (1,)), ((), ())), preferred_element_type=F32)


def _dot_tn(a, b):
    return lax.dot_general(a, b, (((0,), (0,)), ((), ())), preferred_element_type=F32)


def _dot_f32(a, b):
    return jnp.dot(a, b, preferred_element_type=F32, precision=lax.Precision.HIGHEST)


def _layer_norm(x, g, b):
    mu = jnp.mean(x, axis=-1, keepdims=True)
    xc = x - mu
    var = jnp.mean(xc * xc, axis=-1, keepdims=True)
    return xc * lax.rsqrt(var + LN_EPS) * g + b


def _silu(x):
    return x * jax.nn.sigmoid(x)


def _proj_kernel(x_ref, w_ref, o_ref):
    o_ref[...] = _dot(x_ref[...].astype(BF16), w_ref[...])


def _proj(x, w, tm):
    n, k = x.shape
    wd = w.shape[1]
    return pl.pallas_call(
        _proj_kernel,
        grid=(n // tm,),
        in_specs=[pl.BlockSpec((tm, k), lambda i: (i, 0)),
                  pl.BlockSpec((k, wd), lambda i: (0, 0))],
        out_specs=pl.BlockSpec((tm, wd), lambda i: (i, 0)),
        out_shape=jax.ShapeDtypeStruct((n, wd), F32),
        compiler_params=_cparams(1),
        name="in_proj",
    )(x, w)


def _rope_tables(pos, n_heads, d, rot, theta, scale=1.0, pad_to=None):
    half = rot // 2
    inv = theta ** (-jnp.arange(half, dtype=F32) / half)
    ang = pos.astype(F32)[:, None] * inv[None, :]
    cos, sin = jnp.cos(ang), jnp.sin(ang)
    p = pos.shape[0]
    one = jnp.ones((p, d - rot), F32)
    zr = jnp.zeros((p, d - rot), F32)
    zh = jnp.zeros((p, half), F32)
    c = jnp.tile(jnp.concatenate([cos, cos, one], 1), (1, n_heads))
    a = jnp.tile(jnp.concatenate([-sin, zh, zr], 1), (1, n_heads))
    b = jnp.tile(jnp.concatenate([zh, sin, zr], 1), (1, n_heads))
    if pad_to is not None and pad_to > n_heads * d:
        extra = pad_to - n_heads * d
        c = jnp.concatenate([c, jnp.ones((p, extra), F32)], 1)
        a = jnp.concatenate([a, jnp.zeros((p, extra), F32)], 1)
        b = jnp.concatenate([b, jnp.zeros((p, extra), F32)], 1)
    return jnp.stack([c, a, b]) * scale


def _rope(x, tab_ref, half):
    w = x.shape[-1]
    return (x * tab_ref[0] + pltpu.roll(x, w - half, 1) * tab_ref[1]
            + pltpu.roll(x, half, 1) * tab_ref[2])


def _even_prep_kernel(qa_ref, ka_ref, qb_ref, kb_ref, iq_ref, ikw_ref,
                      tq_ref, tk_ref, td_ref, ti_ref,
                      qa_o, ka_o, qb_o, kb_o, iq_o, ikw_o):
    qa_o[...] = _rope(qa_ref[...], tq_ref, RET_DK // 2)
    ka_o[...] = _rope(ka_ref[...], tk_ref, RET_DK // 2)
    qb_o[...] = _rope(qb_ref[...], td_ref, DSA_ROT // 2).astype(BF16)
    kb = kb_ref[...]
    h = DSA_ROT // 2
    kb_o[...] = (kb * td_ref[0, :, :LANE] + pltpu.roll(kb, LANE - h, 1) * td_ref[1, :, :LANE]
                 + pltpu.roll(kb, h, 1) * td_ref[2, :, :LANE])
    iq = iq_ref[...]
    w = iq.shape[-1]
    iq_o[...] = (iq * td_ref[0, :, :w] + pltpu.roll(iq, w - h, 1) * td_ref[1, :, :w]
                 + pltpu.roll(iq, h, 1) * td_ref[2, :, :w]).astype(BF16)
    ikw_o[...] = _rope(ikw_ref[...], ti_ref, DSA_ROT // 2)


def _even_prep(pe, tabs, tm, n_prompt_blocks, tab_blocks):
    n = pe.shape[0]
    tq, tk, td, ti = tabs

    def tix(i):
        return (0, jnp.where(i < n_prompt_blocks, i % tab_blocks, tab_blocks), 0)

    def col(wd, j):
        return pl.BlockSpec((tm, wd), lambda i: (i, j))

    def tab(wd):
        return pl.BlockSpec((3, tm, wd), tix)

    def out(wd):
        return pl.BlockSpec((tm, wd), lambda i: (i, 0))

    return pl.pallas_call(
        _even_prep_kernel,
        grid=(n // tm,),
        in_specs=[col(256, 0), col(256, 1), col(512, 3), col(128, 16), col(256, 9), col(128, 20),
                  tab(256), tab(256), tab(512), tab(128)],
        out_specs=[out(256), out(256), out(512), out(128), out(256), out(128)],
        out_shape=[jax.ShapeDtypeStruct((n, 256), F32), jax.ShapeDtypeStruct((n, 256), F32),
                   jax.ShapeDtypeStruct((n, 512), BF16), jax.ShapeDtypeStruct((n, 128), F32),
                   jax.ShapeDtypeStruct((n, 256), BF16), jax.ShapeDtypeStruct((n, 128), F32)],
        compiler_params=_cparams(1),
        name="even_rope",
    )(pe, pe, pe, pe, pe, pe, tq, tk, td, ti)


def _ret_kernel(gch_ref, q_ref, k_ref, v_ref, g_ref, dm_ref, qd_ref, kd_ref, gng_ref, gnb_ref,
                s0_ref, *rest):
    o_ref, s_ref, s_sc = rest[-3], rest[-2], rest[-1]
    c = pl.program_id(1)

    @pl.when(c == 0)
    def _():
        s_sc[...] = s0_ref[0]

    q = q_ref[...]
    k = k_ref[...]
    qx = (q * qd_ref[...]).astype(BF16)
    kw = (k * kd_ref[...]).astype(BF16)
    qb = q.astype(BF16)
    kb = k.astype(BF16)
    vb = v_ref[...].astype(BF16)
    gate = g_ref[...]
    for h in range(RET_HEADS):
        ks = slice(h * RET_DK, (h + 1) * RET_DK)
        vs = slice(h * RET_DV, (h + 1) * RET_DV)
        att = _dot_nt(qb[:, ks], kb[:, ks]) * dm_ref[h]
        s_old = s_sc[h]
        y = _dot(att.astype(BF16), vb[:, vs]) + _dot(qx[:, ks], s_old.astype(BF16))
        s_sc[h] = s_old * gch_ref[h] + _dot_tn(kw[:, ks], vb[:, vs])
        yn = _layer_norm(y, gng_ref[:, vs], gnb_ref[:, vs])
        o_ref[:, vs] = _silu(gate[:, vs]) * yn

    @pl.when(c == pl.num_programs(1) - 1)
    def _():
        s_ref[0] = s_sc[...]


def _retention(qa, ka, pe, s0, gn_g, gn_b, n_seq, t, row0, prev_out):
    n = qa.shape[0]
    lc = min(RET_LC, t)
    nc = t // lc
    blk0 = row0 // lc
    log_g = jnp.log(1.0 - 2.0 ** (-5.0 - jnp.arange(RET_HEADS, dtype=F32)))
    pos = jnp.arange(lc, dtype=F32)
    diff = pos[:, None] - pos[None, :]
    dmask = jnp.where(diff >= 0, jnp.exp(jnp.maximum(diff, 0.0)[None] * log_g[:, None, None]), 0.0)
    w_end = jnp.exp((lc - 1 - pos)[:, None] * log_g[None, :])
    xi = jnp.exp((pos + 1.0)[:, None] * log_g[None, :])
    kdec = jnp.repeat(w_end, RET_DK, axis=1)
    qdec = jnp.repeat(xi, RET_DK, axis=1)
    gch = jnp.exp(lc * log_g)

    def rows(wd, j):
        return pl.BlockSpec((lc, wd), lambda b, c: (blk0 + b * nc + c, j))

    def const(shape):
        nd = len(shape)
        return pl.BlockSpec(shape, lambda b, c: (0,) * nd)

    in_specs = [pl.BlockSpec(memory_space=pltpu.SMEM),
                rows(256, 0), rows(256, 0), rows(512, 1), rows(512, 2),
                const((RET_HEADS, lc, lc)), const((lc, 256)), const((lc, 256)),
                const((1, 512)), const((1, 512)),
                pl.BlockSpec((1, RET_HEADS, RET_DK, RET_DV), lambda b, c: (b, 0, 0, 0))]
    args = [gch, qa, ka, pe, pe, dmask, qdec, kdec,
            gn_g.reshape(1, 512), gn_b.reshape(1, 512), s0]
    aliases = {}
    if prev_out is not None:
        in_specs.append(pl.BlockSpec(memory_space=pl.ANY))
        args.append(prev_out)
        aliases = {len(args) - 1: 0}
    return pl.pallas_call(
        _ret_kernel,
        grid=(n_seq, nc),
        in_specs=in_specs,
        out_specs=[pl.BlockSpec((lc, 512), lambda b, c: (blk0 + b * nc + c, 0)),
                   pl.BlockSpec((1, RET_HEADS, RET_DK, RET_DV), lambda b, c: (b, 0, 0, 0))],
        out_shape=[jax.ShapeDtypeStruct((n, 512), F32),
                   jax.ShapeDtypeStruct((n_seq, RET_HEADS, RET_DK, RET_DV), F32)],
        scratch_shapes=[pltpu.VMEM((RET_HEADS, RET_DK, RET_DV), F32)],
        input_output_aliases=aliases,
        compiler_params=_cparams(2),
        name="retention",
    )(*args)


def _dsa_kernel(nkb_ref, q_ref, iq_ref, iw_ref, qlim_ref, k_ref, v_ref, ik_ref, *rest,
                topk):
    o_ref, key_sc, m_sc, l_sc, acc_sc = rest[-5:]
    nkb = nkb_ref[pl.program_id(1)]
    tq = q_ref.shape[0]
    tk = key_sc.shape[2]
    qlim = qlim_ref[...]
    iw = iw_ref[:, IDX_DIM:IDX_DIM + IDX_HEADS]
    iq = iq_ref[...]
    idx_scale = (IDX_HEADS * IDX_DIM) ** -0.5
    lane = lax.broadcasted_iota(I32, (tq, tk), 1)

    def score_body(kb, carry):
        off = pl.multiple_of(kb * tk, tk)
        ikb = ik_ref[pl.ds(off, tk), :][:, :IDX_DIM].astype(BF16)
        s = jnp.zeros((tq, tk), F32)
        for h in range(IDX_HEADS):
            sh = _dot_nt(iq[:, h * IDX_DIM:(h + 1) * IDX_DIM], ikb)
            s = s + iw[:, h:h + 1] * jnp.maximum(sh, 0.0)
        s = s * idx_scale
        s = jnp.where(s == 0.0, 0.0, s)
        bits = pltpu.bitcast(s, I32)
        key = jnp.where(bits >= 0, bits, bits ^ jnp.int32(0x7FFFFFFF))
        adm = (off + lane) < qlim
        key_sc[kb] = jnp.where(adm, key, jnp.int32(INT_MIN))
        return carry

    lax.fori_loop(0, nkb, score_body, 0)

    def count_ge(cand):
        def body(kb, acc):
            return acc + jnp.where(key_sc[kb] >= cand, 1.0, 0.0)
        acc = lax.fori_loop(0, nkb, body, jnp.zeros((tq, tk), F32))
        return jnp.sum(acc, axis=1, keepdims=True)

    def bit_body(it, ans):
        cand = ans + (jnp.int32(1) << (31 - it))
        return jnp.where(count_ge(cand) >= topk, cand, ans)

    t = lax.fori_loop(0, 32, bit_body, jnp.full((tq, 1), INT_MIN, I32))

    def gt_body(kb, acc):
        return acc + jnp.where(key_sc[kb] > t, 1.0, 0.0)

    n_gt = jnp.sum(lax.fori_loop(0, nkb, gt_body, jnp.zeros((tq, tk), F32)), axis=1, keepdims=True)
    need = topk - n_gt

    m_sc[...] = jnp.full(m_sc.shape, NEG_BIG, F32)
    l_sc[...] = jnp.zeros(l_sc.shape, F32)
    acc_sc[...] = jnp.zeros(acc_sc.shape, F32)
    r_i = lax.broadcasted_iota(I32, (tk, tk), 0)
    c_i = lax.broadcasted_iota(I32, (tk, tk), 1)
    upper = (r_i < c_i).astype(BF16)
    att_scale = DSA_HD ** -0.5
    group = DSA_HEADS // DSA_KV_HEADS

    def att_body(kb, n_eq):
        off = pl.multiple_of(kb * tk, tk)
        key = key_sc[kb]
        adm = (off + lane) < qlim
        eq = jnp.logical_and(key == t, adm)
        eqf = jnp.where(eq, 1.0, 0.0)
        pref = _dot(eqf.astype(BF16), upper)
        sel = jnp.logical_or(key > t, jnp.logical_and(eq, (n_eq + pref) < need))
        kblk = k_ref[pl.ds(off, tk), :].astype(BF16)
        vblk = v_ref[pl.ds(off, tk), :].astype(BF16)
        for h in range(DSA_HEADS):
            n = h // group
            hs = slice(h * DSA_HD, (h + 1) * DSA_HD)
            ns = slice(n * DSA_HD, (n + 1) * DSA_HD)
            lg = _dot_nt(q_ref[:, hs], kblk[:, ns]) * att_scale
            m_old = m_sc[h]
            m_new = jnp.maximum(m_old, jnp.max(jnp.where(sel, lg, NEG_BIG), axis=1, keepdims=True))
            p = jnp.where(sel, jnp.exp(lg - m_new), 0.0)
            alpha = jnp.exp(m_old - m_new)
            l_sc[h] = alpha * l_sc[h] + jnp.sum(p, axis=1, keepdims=True)
            acc_sc[:, hs] = alpha * acc_sc[:, hs] + _dot(p.astype(BF16), vblk[:, ns])
            m_sc[h] = m_new
        return n_eq + jnp.sum(eqf, axis=1, keepdims=True)

    lax.fori_loop(0, nkb, att_body, jnp.zeros((tq, 1), F32))
    for h in range(DSA_HEADS):
        hs = slice(h * DSA_HD, (h + 1) * DSA_HD)
        o_ref[:, hs] = acc_sc[:, hs] / l_sc[h]


def _dsa(qb, iq, ikw, k_arr, v_arr, ik_arr, kv_col, qlim, nkb, n_seq, tq_total, s_len, tq, row0,
         topk, prev_out):
    n = qb.shape[0]
    nq = tq_total // tq
    blk0 = row0 // tq

    def qrows(wd):
        return pl.BlockSpec((tq, wd), lambda b, i, s: (blk0 + b * nq + i, 0))

    in_specs = [qrows(512), qrows(256), qrows(128),
                pl.BlockSpec((tq, 1), lambda b, i, s: (i, 0)),
                pl.BlockSpec((s_len, 128), lambda b, i, s: (b, 0)),
                pl.BlockSpec((s_len, 128), lambda b, i, s: (b, kv_col)),
                pl.BlockSpec((s_len, 128), lambda b, i, s: (b, 0))]
    args = [nkb, qb, iq, ikw, qlim, k_arr, v_arr, ik_arr]
    aliases = {}
    if prev_out is not None:
        in_specs.append(pl.BlockSpec(memory_space=pl.ANY))
        args.append(prev_out)
        aliases = {len(args) - 1: 0}
    grid_spec = pltpu.PrefetchScalarGridSpec(
        num_scalar_prefetch=1,
        grid=(n_seq, nq),
        in_specs=in_specs,
        out_specs=pl.BlockSpec((tq, 512), lambda b, i, s: (blk0 + b * nq + i, 0)),
        scratch_shapes=[pltpu.VMEM((s_len // DSA_TK, tq, DSA_TK), I32),
                        pltpu.VMEM((DSA_HEADS, tq, 1), F32),
                        pltpu.VMEM((DSA_HEADS, tq, 1), F32),
                        pltpu.VMEM((tq, 512), F32)])
    return pl.pallas_call(
        functools.partial(_dsa_kernel, topk=topk),
        grid_spec=grid_spec,
        out_shape=jax.ShapeDtypeStruct((n, 512), F32),
        input_output_aliases=aliases,
        compiler_params=_cparams(2),
        name="dsa_attention",
    )(*args)


def _outproj_ln_kernel(ya_ref, yb_ref, w_ref, h_ref, g_ref, b_ref, o_ref):
    half = ya_ref.shape[1]
    y = (_dot(ya_ref[...].astype(BF16), w_ref[:half, :])
         + _dot(yb_ref[...].astype(BF16), w_ref[half:, :]))
    o_ref[...] = _layer_norm(DN_ALPHA * h_ref[...] + y, g_ref[...], b_ref[...])


def _outproj_ln(ya, yb, w, h, g, b, tm):
    n = h.shape[0]
    return pl.pallas_call(
        _outproj_ln_kernel,
        grid=(n // tm,),
        in_specs=[pl.BlockSpec((tm, 512), lambda i: (i, 0)),
                  pl.BlockSpec((tm, 512), lambda i: (i, 0)),
                  pl.BlockSpec((D_MODEL, D_MODEL), lambda i: (0, 0)),
                  pl.BlockSpec((tm, D_MODEL), lambda i: (i, 0)),
                  pl.BlockSpec((1, D_MODEL), lambda i: (0, 0)),
                  pl.BlockSpec((1, D_MODEL), lambda i: (0, 0))],
        out_specs=pl.BlockSpec((tm, D_MODEL), lambda i: (i, 0)),
        out_shape=jax.ShapeDtypeStruct((n, D_MODEL), F32),
        compiler_params=_cparams(1),
        name="out_proj_ln",
    )(ya, yb, w, h, g.reshape(1, -1), b.reshape(1, -1))


def _router_kernel(x_ref, w_ref, b_ref, idx_o, gate_o, rank_o, cnt_o, cnt_sc):
    i = pl.program_id(0)

    @pl.when(i == 0)
    def _():
        cnt_sc[...] = jnp.zeros(cnt_sc.shape, F32)

    tm = x_ref.shape[0]
    lane = lax.broadcasted_iota(I32, (tm, LANE), 1)
    logits = _dot_f32(x_ref[...], w_ref[...]) + b_ref[...]
    logits = jnp.where(lane < N_EXPERTS, logits, -jnp.inf)
    vals, idxs = [], []
    onehot = jnp.zeros((tm, LANE), F32)
    for _ in range(TOP_K):
        m = jnp.max(logits, axis=1, keepdims=True)
        ix = jnp.min(jnp.where(logits == m, lane, LANE), axis=1, keepdims=True)
        hit = lane == ix
        onehot = jnp.where(hit, 1.0, onehot)
        logits = jnp.where(hit, -jnp.inf, logits)
        vals.append(m)
        idxs.append(ix)
    es = [jnp.exp(v - vals[0]) for v in vals]
    den = es[0] + es[1] + es[2] + es[3]
    r_i = lax.broadcasted_iota(I32, (tm, tm), 0)
    c_i = lax.broadcasted_iota(I32, (tm, tm), 1)
    lower = (c_i < r_i).astype(BF16)
    rank_dense = _dot(lower, onehot.astype(BF16)) + cnt_sc[...]
    idx_out = jnp.zeros((tm, LANE), I32)
    gate_out = jnp.zeros((tm, LANE), F32)
    rank_out = jnp.zeros((tm, LANE), F32)
    for k in range(TOP_K):
        rk = jnp.sum(jnp.where(lane == idxs[k], rank_dense, 0.0), axis=1, keepdims=True)
        idx_out = jnp.where(lane == k, idxs[k], idx_out)
        gate_out = jnp.where(lane == k, es[k] / den, gate_out)
        rank_out = jnp.where(lane == k, rk, rank_out)
    idx_o[...] = idx_out
    gate_o[...] = gate_out
    rank_o[...] = rank_out.astype(I32)
    cnt = cnt_sc[...] + jnp.sum(onehot, axis=0, keepdims=True)
    cnt_sc[...] = cnt
    cnt_o[...] = cnt


def _router(x, w_r, b_r, tm):
    n = x.shape[0]
    w = jnp.zeros((D_MODEL, LANE), F32).at[:, :N_EXPERTS].set(w_r)
    b = jnp.zeros((1, LANE), F32).at[0, :N_EXPERTS].set(b_r)
    row = pl.BlockSpec((tm, LANE), lambda i: (i, 0))
    return pl.pallas_call(
        _router_kernel,
        grid=(n // tm,),
        in_specs=[pl.BlockSpec((tm, D_MODEL), lambda i: (i, 0)),
                  pl.BlockSpec((D_MODEL, LANE), lambda i: (0, 0)),
                  pl.BlockSpec((1, LANE), lambda i: (0, 0))],
        out_specs=[row, row, row, pl.BlockSpec((1, LANE), lambda i: (0, 0))],
        out_shape=[jax.ShapeDtypeStruct((n, LANE), I32), jax.ShapeDtypeStruct((n, LANE), F32),
                   jax.ShapeDtypeStruct((n, LANE), I32), jax.ShapeDtypeStruct((1, LANE), F32)],
        scratch_shapes=[pltpu.VMEM((1, LANE), F32)],
        compiler_params=_cparams(1),
        name="moe_router",
    )(x, w, b)


def _moe_kernel(be_ref, nu_ref, x_ref, wgu_ref, bgu_ref, wdn_ref, bdn_ref, o_ref):
    i = pl.program_id(0)

    @pl.when(i < nu_ref[0])
    def _():
        h = _dot(x_ref[...].astype(BF16), wgu_ref[0]) + bgu_ref[0]
        g = jnp.minimum(h[:, :D_FF], SWIGLU_LIMIT)
        up = jnp.clip(h[:, D_FF:], -SWIGLU_LIMIT, SWIGLU_LIMIT)
        a = (up + 1.0) * g * jax.nn.sigmoid(SWIGLU_ALPHA * g)
        o_ref[...] = _dot(a.astype(BF16), wdn_ref[0]) + bdn_ref[0]

    @pl.when(i >= nu_ref[0])
    def _():
        o_ref[...] = jnp.zeros(o_ref.shape, F32)


def _moe_experts(xs, blk_e, n_used, w_gu, b_gu, w_dn, b_dn):
    n_rows = xs.shape[0]
    tm = MOE_TM
    grid_spec = pltpu.PrefetchScalarGridSpec(
        num_scalar_prefetch=2,
        grid=(n_rows // tm,),
        in_specs=[pl.BlockSpec((tm, D_MODEL), lambda i, be, nu: (i, 0)),
                  pl.BlockSpec((1, D_MODEL, 2 * D_FF), lambda i, be, nu: (be[i], 0, 0)),
                  pl.BlockSpec((1, 1, 2 * D_FF), lambda i, be, nu: (be[i], 0, 0)),
                  pl.BlockSpec((1, D_FF, D_MODEL), lambda i, be, nu: (be[i], 0, 0)),
                  pl.BlockSpec((1, 1, D_MODEL), lambda i, be, nu: (be[i], 0, 0))],
        out_specs=pl.BlockSpec((tm, D_MODEL), lambda i, be, nu: (i, 0)))
    return pl.pallas_call(
        _moe_kernel,
        grid_spec=grid_spec,
        out_shape=jax.ShapeDtypeStruct((n_rows, D_MODEL), F32),
        compiler_params=_cparams(1),
        name="moe_experts",
    )(blk_e, n_used, xs, w_gu, b_gu.reshape(N_EXPERTS, 1, -1), w_dn, b_dn.reshape(N_EXPERTS, 1, -1))


def _combine_ln_kernel(h_ref, y0_ref, y1_ref, y2_ref, y3_ref, gate_ref, g_ref, b_ref, o_ref):
    gate = gate_ref[...]
    y = (gate[:, 0:1] * y0_ref[...] + gate[:, 1:2] * y1_ref[...]
         + gate[:, 2:3] * y2_ref[...] + gate[:, 3:4] * y3_ref[...])
    o_ref[...] = _layer_norm(DN_ALPHA * h_ref[...] + y, g_ref[...], b_ref[...])


def _combine_ln(h, ys, gate, g, b, tm):
    n = h.shape[0]
    row = pl.BlockSpec((tm, D_MODEL), lambda i: (i, 0))
    vec = pl.BlockSpec((1, D_MODEL), lambda i: (0, 0))
    return pl.pallas_call(
        _combine_ln_kernel,
        grid=(n // tm,),
        in_specs=[row, row, row, row, row, pl.BlockSpec((tm, LANE), lambda i: (i, 0)), vec, vec],
        out_specs=row,
        out_shape=jax.ShapeDtypeStruct((n, D_MODEL), F32),
        compiler_params=_cparams(1),
        name="moe_combine_ln",
    )(h, ys[0], ys[1], ys[2], ys[3], gate, g.reshape(1, -1), b.reshape(1, -1))


def _moe_layer(h, w_r, b_r, w_gu, b_gu, w_dn, b_dn, ln_g, ln_b, tm):
    n = h.shape[0]
    idx, gate, rank, cnt = _router(h, w_r, b_r, tm)
    top_i = idx[:, :TOP_K]
    counts = cnt[0, :N_EXPERTS].astype(I32)
    padded = (counts + MOE_TM - 1) // MOE_TM * MOE_TM
    pad_end = jnp.cumsum(padded)
    start = pad_end - padded
    dest = start[top_i] + rank[:, :TOP_K]
    n_blk = -(-(n * TOP_K) // MOE_TM) + N_EXPERTS
    n_rows = n_blk * MOE_TM
    n_used = (pad_end[-1] // MOE_TM).astype(I32)
    blk = jnp.arange(n_blk, dtype=I32)
    blk_e = jnp.searchsorted(pad_end, jnp.minimum(blk, n_used - 1) * MOE_TM, side='right')
    blk_e = jnp.minimum(blk_e, N_EXPERTS - 1).astype(I32)
    tok = jnp.broadcast_to(jnp.arange(n, dtype=I32)[:, None], (n, TOP_K))
    src = jnp.zeros((n_rows,), I32).at[dest.reshape(-1)].set(tok.reshape(-1))
    xs = jnp.take(h, src, axis=0)
    ybuf = _moe_experts(xs, blk_e, n_used.reshape(1), w_gu.astype(BF16), b_gu,
                        w_dn.astype(BF16), b_dn)
    ys = [jnp.take(ybuf, dest[:, k], axis=0) for k in range(TOP_K)]
    return _combine_ln(h, ys, gate, ln_g, ln_b, tm)


def _band_kernel(q_ref, *rest, nkb):
    k_refs = rest[:nkb]
    v_refs = rest[nkb:2 * nkb]
    bias_ref = rest[2 * nkb]
    o_ref = rest[-1]
    c = pl.program_id(1)
    tq = q_ref.shape[0]
    tkb = k_refs[0].shape[0]
    scale = BAND_HD ** -0.5
    qb = q_ref[...].astype(BF16)
    kbs = [r[...].astype(BF16) for r in k_refs]
    vbs = [r[...].astype(BF16) for r in v_refs]
    valid = [(c + j - (nkb - 1)) >= 0 for j in range(nkb)]
    for h in range(BAND_HEADS):
        hs = slice(h * BAND_HD, (h + 1) * BAND_HD)
        lgs = []
        for j in range(nkb):
            lg = _dot_nt(qb[:, hs], kbs[j][:, hs]) * scale + bias_ref[h, :, j * tkb:(j + 1) * tkb]
            lgs.append(jnp.where(valid[j], lg, NEG_BIG))
        m = lgs[0].max(axis=1, keepdims=True)
        for j in range(1, nkb):
            m = jnp.maximum(m, lgs[j].max(axis=1, keepdims=True))
        ps = [jnp.exp(lg - m) for lg in lgs]
        den = ps[0].sum(axis=1, keepdims=True)
        for j in range(1, nkb):
            den = den + ps[j].sum(axis=1, keepdims=True)
        inv = 1.0 / den
        acc = _dot((ps[0] * inv).astype(BF16), vbs[0][:, hs])
        for j in range(1, nkb):
            acc = acc + _dot((ps[j] * inv).astype(BF16), vbs[j][:, hs])
        o_ref[:, hs] = acc


def _band(q_arr, k_arr, v_arr, cols, bias, n_seq, t, tq, tkb, nkb, q_row0, kv_blocks_per_seq,
          n_out, prev_out):
    nq = t // tq
    qblk0 = q_row0 // tq
    qcol, kcol, vcol = cols

    def kv_spec(j, col):
        def ix(b, c):
            return (b * kv_blocks_per_seq + jnp.maximum(c + j - (nkb - 1), 0), col)
        return pl.BlockSpec((tkb, 512), ix)

    in_specs = ([pl.BlockSpec((tq, 512), lambda b, c: (qblk0 + b * nq + c, qcol))]
                + [kv_spec(j, kcol) for j in range(nkb)]
                + [kv_spec(j, vcol) for j in range(nkb)]
                + [pl.BlockSpec(bias.shape, lambda b, c: (0, 0, 0))])
    args = [q_arr] + [k_arr] * nkb + [v_arr] * nkb + [bias]
    aliases = {}
    if prev_out is not None:
        in_specs.append(pl.BlockSpec(memory_space=pl.ANY))
        args.append(prev_out)
        aliases = {len(args) - 1: 0}
    return pl.pallas_call(
        functools.partial(_band_kernel, nkb=nkb),
        grid=(n_seq, nq),
        in_specs=in_specs,
        out_specs=pl.BlockSpec((tq, 512), lambda b, c: (qblk0 + b * nq + c, 0)),
        out_shape=jax.ShapeDtypeStruct((n_out, 512), F32),
        input_output_aliases=aliases,
        compiler_params=_cparams(2),
        name="band_attention",
    )(*args)


def _band_bias(rel_bias, tq, n_keys, key0):
    qp = jnp.arange(tq, dtype=I32)[:, None]
    kp = key0 + jnp.arange(n_keys, dtype=I32)[None, :]
    rel = jnp.clip(qp - kp, -REL_CLIP, REL_CLIP) + REL_CLIP
    cs = (qp // CHUNK) * CHUNK
    band = jnp.logical_and(kp >= cs - BAND_PAST, kp < cs + CHUNK)
    return jnp.where(band[None], rel_bias[:, rel], NEG_BIG).astype(F32)


def _ssd_kernel(dsk_ref, z_ref, xbc_ref, dt_ref, cw_ref, cb_ref, dtb_ref, alog_ref, ng_ref,
                h0_ref, c0_ref, *rest):
    o_ref, h_ref, cl_ref, h_sc, xe_sc, y_sc = rest[-6:]
    c = pl.program_id(1)
    lc = z_ref.shape[0]

    @pl.when(c == 0)
    def _():
        h_sc[...] = h0_ref[0]
        xe_sc[0:8, :] = jnp.zeros((8, xe_sc.shape[1]), F32)
        xe_sc[8 - (SSD_CONV - 1):8, :] = c0_ref[0]

    xe_sc[8:8 + lc, :] = xbc_ref[...]
    conv = cb_ref[...] + cw_ref[SSD_CONV - 1:SSD_CONV, :] * xe_sc[8:8 + lc, :]
    for s in range(1, SSD_CONV):
        conv = conv + cw_ref[SSD_CONV - 1 - s:SSD_CONV - s, :] * xe_sc[8 - s:8 - s + lc, :]
    u = _silu(conv)
    gs = SSD_GROUPS * SSD_STATE
    xs = u[:, :SSD_INNER]
    bm = u[:, SSD_INNER:SSD_INNER + gs].astype(BF16)
    cm = u[:, SSD_INNER + gs:].astype(BF16)
    dx = dt_ref[...] + dtb_ref[...]
    dtv = jnp.maximum(dx, 0.0) + jnp.log1p(jnp.exp(-jnp.abs(dx)))
    a = dtv * (-jnp.exp(alog_ref[...]))
    r_i = lax.broadcasted_iota(I32, (lc, lc), 0)
    c_i = lax.broadcasted_iota(I32, (lc, lc), 1)
    causal = c_i <= r_i
    acum = _dot_f32(causal.astype(F32), a)
    acum_t = acum.T
    hpg = SSD_HEADS // SSD_GROUPS
    for g in range(SSD_GROUPS):
        ss = slice(g * SSD_STATE, (g + 1) * SSD_STATE)
        cb = _dot_nt(cm[:, ss], bm[:, ss])
        for jj in range(hpg):
            j = g * hpg + jj
            ps = slice(j * SSD_HD, (j + 1) * SSD_HD)
            col = acum[:, j:j + 1]
            row = acum_t[j:j + 1, :]
            lmat = jnp.exp(jnp.where(causal, col - row, -jnp.inf))
            x_j = xs[:, ps]
            xdt = x_j * dtv[:, j:j + 1]
            h_old = h_sc[j]
            y = _dot((cb * lmat).astype(BF16), xdt.astype(BF16))
            y = y + _dot_nt(cm[:, ss], h_old.astype(BF16)) * jnp.exp(col)
            y = y + dsk_ref[j] * x_j
            last = acum[lc - 1:lc, j:j + 1]
            st = _dot_tn((xdt * jnp.exp(last - col)).astype(BF16), bm[:, ss])
            h_sc[j] = h_old * jnp.exp(last) + st
            y_sc[:, ps] = y
    yd = y_sc[...] * _silu(z_ref[...])
    gw = SSD_INNER // SSD_GROUPS
    for g in range(SSD_GROUPS):
        ws = slice(g * gw, (g + 1) * gw)
        yg = yd[:, ws]
        ms = jnp.mean(yg * yg, axis=-1, keepdims=True)
        o_ref[:, ws] = yg * lax.rsqrt(ms + LN_EPS) * ng_ref[:, ws]
    xe_sc[0:8, :] = xe_sc[lc:lc + 8, :]

    @pl.when(c == pl.num_programs(1) - 1)
    def _():
        h_ref[0] = h_sc[...]
        cl_ref[0] = xe_sc[8 - (SSD_CONV - 1):8, :]


def _ssd(po, h0, c0, conv_w, conv_b, dt_bias, a_log, d_skip, norm_g, n_seq, t, row0, prev_out):
    n = po.shape[0]
    lc = min(SSD_LC, t)
    nc = t // lc
    blk0 = row0 // lc
    cdim = conv_w.shape[1]

    def rows(wd, j):
        return pl.BlockSpec((lc, wd), lambda b, c: (blk0 + b * nc + c, j))

    def const(shape):
        nd = len(shape)
        return pl.BlockSpec(shape, lambda b, c: (0,) * nd)

    pad8 = lambda v: jnp.zeros((1, LANE), F32).at[0, :SSD_HEADS].set(v)
    in_specs = [pl.BlockSpec(memory_space=pltpu.SMEM),
                rows(512, 3), rows(cdim, 2), rows(LANE, 24),
                const((SSD_CONV, cdim)), const((1, cdim)), const((1, LANE)), const((1, LANE)),
                const((1, SSD_INNER)),
                pl.BlockSpec((1, SSD_HEADS, SSD_HD, SSD_STATE), lambda b, c: (b, 0, 0, 0)),
                pl.BlockSpec((1, SSD_CONV - 1, cdim), lambda b, c: (b, 0, 0))]
    args = [d_skip, po, po, po, conv_w, conv_b.reshape(1, -1), pad8(dt_bias), pad8(a_log),
            norm_g.reshape(1, -1), h0, c0]
    aliases = {}
    if prev_out is not None:
        in_specs.append(pl.BlockSpec(memory_space=pl.ANY))
        args.append(prev_out)
        aliases = {len(args) - 1: 0}
    return pl.pallas_call(
        _ssd_kernel,
        grid=(n_seq, nc),
        in_specs=in_specs,
        out_specs=[pl.BlockSpec((lc, 512), lambda b, c: (blk0 + b * nc + c, 0)),
                   pl.BlockSpec((1, SSD_HEADS, SSD_HD, SSD_STATE), lambda b, c: (b, 0, 0, 0)),
                   pl.BlockSpec((1, SSD_CONV - 1, cdim), lambda b, c: (b, 0, 0))],
        out_shape=[jax.ShapeDtypeStruct((n, 512), F32),
                   jax.ShapeDtypeStruct((n_seq, SSD_HEADS, SSD_HD, SSD_STATE), F32),
                   jax.ShapeDtypeStruct((n_seq, SSD_CONV - 1, cdim), F32)],
        scratch_shapes=[pltpu.VMEM((SSD_HEADS, SSD_HD, SSD_STATE), F32),
                        pltpu.VMEM((lc + 8, cdim), F32),
                        pltpu.VMEM((lc, 512), F32)],
        input_output_aliases=aliases,
        compiler_params=_cparams(2),
        name="ssd_scan",
    )(*args)


def _pad_cols(w, width):
    return jnp.concatenate([w, jnp.zeros((w.shape[0], width - w.shape[1]), w.dtype)], axis=1)


def kernel(x_prompt, x_sample, state_ret, cache_dsa_k, cache_dsa_v, cache_dsa_kidx, cache_band_k, cache_band_v, state_ssm, state_conv, e_w_in, e_w_out, e_gn_g, e_gn_b, o_w_in, o_w_out, o_rel_bias, o_conv_w, o_conv_b, o_dt_bias, o_a_log, o_d_skip, o_norm_g, ln1_g, ln1_b, ln2_g, ln2_b, router_w, router_b, exp_w_gu, exp_b_gu, exp_w_dn, exp_b_dn):
    bp, tp, _ = x_prompt.shape
    bs, ts, _ = x_sample.shape
    past = cache_dsa_k.shape[2]
    n_p, n_s = bp * tp, bs * ts
    n = n_p + n_s
    tm = math.gcd(512, math.gcd(n_p, n_s))
    assert tp % tm == 0 and tm % ts == 0 and ts == CHUNK

    h = jnp.concatenate([x_prompt.reshape(n_p, D_MODEL), x_sample.reshape(n_s, D_MODEL)], axis=0)

    pe = _proj(h, _pad_cols(e_w_in[0], EVEN_W).astype(BF16), tm)
    pos_p = jnp.arange(tp, dtype=I32)
    pos_s = past + jnp.arange(ts, dtype=I32)
    pos_tab = jnp.concatenate([pos_p, jnp.tile(pos_s, tm // ts)])
    tabs = (_rope_tables(pos_tab, RET_HEADS, RET_DK, RET_DK, RET_THETA),
            _rope_tables(pos_tab, RET_HEADS, RET_DK, RET_DK, RET_THETA, scale=RET_DK ** -0.5),
            _rope_tables(pos_tab, DSA_HEADS, DSA_HD, DSA_ROT, ROPE_THETA),
            _rope_tables(pos_tab, 1, IDX_DIM, DSA_ROT, ROPE_THETA, pad_to=LANE))
    qa, ka, qb, kb, iq, ikw = _even_prep(pe, tabs, tm, n_p // tm, tp // tm)

    ya, ret_p = _retention(qa, ka, pe, jnp.zeros((bp, RET_HEADS, RET_DK, RET_DV), F32),
                           e_gn_g[0], e_gn_b[0], bp, tp, 0, None)
    ya, ret_s = _retention(qa, ka, pe, state_ret[0], e_gn_g[0], e_gn_b[0], bs, ts, n_p, ya)

    topk_p = min(DSA_TOPK_MAX, tp // 4)
    qlim_p = (((pos_p // CHUNK) + 1) * CHUNK).reshape(tp, 1)
    nq_p = tp // DSA_TQ
    nkb_p = ((jnp.arange(nq_p, dtype=I32) + 1) * DSA_TQ + DSA_TK - 1) // DSA_TK
    yb = _dsa(qb, iq, ikw, kb, pe, ikw, 17, qlim_p, nkb_p, bp, tp, tp, DSA_TQ, 0, topk_p, None)

    s_len = past + ts
    s_pad = -(-s_len // DSA_TK) * DSA_TK
    topk_s = min(DSA_TOPK_MAX, s_len // 4)

    def cat_keys(cache, new):
        zpad = jnp.zeros((bs, s_pad - s_len, LANE), F32)
        return jnp.concatenate([cache, new.reshape(bs, ts, LANE), zpad], axis=1).reshape(bs * s_pad, LANE)

    kidx_pad = jnp.concatenate([cache_dsa_kidx[0], jnp.zeros((bs, past, LANE - IDX_DIM), F32)], axis=-1)
    ks = cat_keys(cache_dsa_k[0].reshape(bs, past, LANE), kb[n_p:])
    vs = cat_keys(cache_dsa_v[0].reshape(bs, past, LANE), pe[n_p:, 2176:2304])
    iks = cat_keys(kidx_pad, ikw[n_p:])
    qlim_s = jnp.full((ts, 1), s_len, I32)
    nkb_s = jnp.full((1,), s_pad // DSA_TK, I32)
    yb = _dsa(qb, iq, ikw, ks, vs, iks, 0, qlim_s, nkb_s, bs, ts, s_pad, ts, n_p, topk_s, yb)

    h = _outproj_ln(ya, yb, e_w_out[0].astype(BF16), h, ln1_g[0], ln1_b[0], tm)
    h = _moe_layer(h, router_w[0], router_b[0], exp_w_gu[0], exp_b_gu[0], exp_w_dn[0], exp_b_dn[0],
                   ln2_g[0], ln2_b[0], tm)

    po = _proj(h, _pad_cols(o_w_in[0], ODD_W).astype(BF16), tm)
    tq_p = min(BAND_TQ, tp)
    nkb_band = BAND_PAST // tq_p + 1
    bias_p = _band_bias(o_rel_bias[0], tq_p, nkb_band * tq_p, -(nkb_band - 1) * tq_p)
    yc = _band(po, po, po, (0, 1, 2), bias_p, bp, tp, tq_p, tq_p, nkb_band, 0, tp // tq_p, n, None)
    band_len = cache_band_k.shape[2]
    kc_new = po[n_p:, 512:1024].reshape(bs, ts, 512)
    vc_new = po[n_p:, 1024:1536].reshape(bs, ts, 512)
    kcat = jnp.concatenate([cache_band_k[0].reshape(bs, band_len, 512), kc_new], axis=1)
    vcat = jnp.concatenate([cache_band_v[0].reshape(bs, band_len, 512), vc_new], axis=1)
    wlen = band_len + ts
    bias_s = _band_bias(o_rel_bias[0], ts, wlen, -band_len)
    yc = _band(po, kcat.reshape(bs * wlen, 512), vcat.reshape(bs * wlen, 512), (0, 0, 0), bias_s,
               bs, ts, ts, wlen, 1, n_p, 1, n, yc)

    ssd_w = (o_conv_w[0], o_conv_b[0], o_dt_bias[0], o_a_log[0], o_d_skip[0], o_norm_g[0])
    cdim = o_conv_w.shape[2]
    yd, ssm_p, conv_p = _ssd(po, jnp.zeros((bp, SSD_HEADS, SSD_HD, SSD_STATE), F32),
                             jnp.zeros((bp, SSD_CONV - 1, cdim), F32), *ssd_w, bp, tp, 0, None)
    yd, ssm_s, conv_s = _ssd(po, state_ssm[0], state_conv[0], *ssd_w, bs, ts, n_p, yd)

    h = _outproj_ln(yc, yd, o_w_out[0].astype(BF16), h, ln1_g[1], ln1_b[1], tm)
    h = _moe_layer(h, router_w[1], router_b[1], exp_w_gu[1], exp_b_gu[1], exp_w_dn[1], exp_b_dn[1],
                   ln2_g[1], ln2_b[1], tm)

    keep = min(BAND_PAST, tp)
    kd = DSA_KV_HEADS * DSA_HD
    kc_p = po[:n_p, 512:1024].reshape(bp, tp, BAND_HEADS, BAND_HD)[:, -keep:]
    vc_p = po[:n_p, 1024:1536].reshape(bp, tp, BAND_HEADS, BAND_HD)[:, -keep:]
    return (h[:n_p].reshape(bp, tp, D_MODEL), h[n_p:].reshape(bs, ts, D_MODEL),
            ret_p[None],
            kb[:n_p].reshape(1, bp, tp, DSA_KV_HEADS, DSA_HD),
            pe[:n_p, 2176:2176 + kd].reshape(1, bp, tp, DSA_KV_HEADS, DSA_HD),
            ikw[:n_p, :IDX_DIM].reshape(1, bp, tp, IDX_DIM),
            kc_p[None], vc_p[None], ssm_p[None], conv_p[None],
            ret_s[None],
            kb[n_p:].reshape(1, bs, ts, DSA_KV_HEADS, DSA_HD),
            pe[n_p:, 2176:2176 + kd].reshape(1, bs, ts, DSA_KV_HEADS, DSA_HD),
            ikw[n_p:, :IDX_DIM].reshape(1, bs, ts, IDX_DIM),
            kc_new.reshape(1, bs, ts, BAND_HEADS, BAND_HD), vc_new.reshape(1, bs, ts, BAND_HEADS, BAND_HD),
            ssm_s[None], conv_s[None])
```

```python
import functools
import math

import jax
import jax.numpy as jnp
import numpy as np
from jax import lax
from jax.experimental import pallas as pl
from jax.experimental.pallas import tpu as pltpu

F32 = jnp.float32
BF16 = jnp.bfloat16
I32 = jnp.int32

D_MODEL = 1024
CHUNK = 64
RET_HEADS, RET_DK, RET_DV, RET_THETA = 8, 32, 64, 10000.0
DSA_HEADS, DSA_KV_HEADS, DSA_HD = 8, 2, 64
DSA_ROT = DSA_HD // 4
IDX_HEADS, IDX_DIM = 4, 64
DSA_TOPK_MAX = 256
ROPE_THETA = 500000.0
BAND_HEADS, BAND_HD, BAND_PREV = 8, 64, 8
BAND_PAST = BAND_PREV * CHUNK
REL_CLIP = 256
SSD_HEADS, SSD_HD, SSD_GROUPS, SSD_STATE, SSD_CONV = 8, 64, 2, 128, 4
SSD_INNER = SSD_HEADS * SSD_HD
N_EXPERTS, TOP_K, D_FF = 32, 4, 1024
SWIGLU_LIMIT, SWIGLU_ALPHA = 7.0, 1.702
DEPTH = 2
DN_ALPHA = (2 * DEPTH) ** 0.25
LN_EPS = 1e-5

LANE = 128
VMEM_LIMIT = 56 * 1024 * 1024
INT_MIN = -(2 ** 31)
NEG_BIG = -1e30

EVEN_IN = 2628
EVEN_W = 2688
ODD_IN = 3080
ODD_W = 3200

MOE_TM = 256
RET_LC = 256
SSD_LC = 256
DSA_TQ = 128
DSA_TK = 256
BAND_TQ = 256


def _cparams(n_axes):
    return pltpu.CompilerParams(dimension_semantics=("arbitrary",) * n_axes,
                                vmem_limit_bytes=VMEM_LIMIT)


def _dot(a, b):
    return jnp.dot(a, b, preferred_element_type=F32)


def _dot_nt(a, b):
    return lax.dot_general(a, b, (((1,), (1,)), ((), ())), preferred_element_type=F32)


def _dot_tn(a, b):
    return lax.dot_general(a, b, (((0,), (0,)), ((), ())), preferred_element_type=F32)


def _dot_f32(a, b):
    return jnp.dot(a, b, preferred_element_type=F32, precision=lax.Precision.HIGHEST)


def _layer_norm(x, g, b):
    mu = jnp.mean(x, axis=-1, keepdims=True)
    xc = x - mu
    var = jnp.mean(xc * xc, axis=-1, keepdims=True)
    return xc * lax.rsqrt(var + LN_EPS) * g + b


def _silu(x):
    return x * jax.nn.sigmoid(x)


def _proj_kernel(x_ref, w_ref, o_ref):
    o_ref[...] = _dot(x_ref[...].astype(BF16), w_ref[...])


def _proj(x, w, tm):
    n, k = x.shape
    wd = w.shape[1]
    return pl.pallas_call(
        _proj_kernel,
        grid=(n // tm,),
        in_specs=[pl.BlockSpec((tm, k), lambda i: (i, 0)),
                  pl.BlockSpec((k, wd), lambda i: (0, 0))],
        out_specs=pl.BlockSpec((tm, wd), lambda i: (i, 0)),
        out_shape=jax.ShapeDtypeStruct((n, wd), F32),
        compiler_params=_cparams(1),
        name="in_proj",
    )(x, w)


def _rope_tables(pos, n_heads, d, rot, theta, scale=1.0, pad_to=None):
    half = rot // 2
    inv = theta ** (-jnp.arange(half, dtype=F32) / half)
    ang = pos.astype(F32)[:, None] * inv[None, :]
    cos, sin = jnp.cos(ang), jnp.sin(ang)
    p = pos.shape[0]
    one = jnp.ones((p, d - rot), F32)
    zr = jnp.zeros((p, d - rot), F32)
    zh = jnp.zeros((p, half), F32)
    c = jnp.tile(jnp.concatenate([cos, cos, one], 1), (1, n_heads))
    a = jnp.tile(jnp.concatenate([-sin, zh, zr], 1), (1, n_heads))
    b = jnp.tile(jnp.concatenate([zh, sin, zr], 1), (1, n_heads))
    if pad_to is not None and pad_to > n_heads * d:
        extra = pad_to - n_heads * d
        c = jnp.concatenate([c, jnp.ones((p, extra), F32)], 1)
        a = jnp.concatenate([a, jnp.zeros((p, extra), F32)], 1)
        b = jnp.concatenate([b, jnp.zeros((p, extra), F32)], 1)
    return jnp.stack([c, a, b]) * scale


def _rope(x, tab_ref, half):
    w = x.shape[-1]
    return (x * tab_ref[0] + pltpu.roll(x, w - half, 1) * tab_ref[1]
            + pltpu.roll(x, half, 1) * tab_ref[2])


def _even_prep_kernel(qa_ref, ka_ref, qb_ref, kb_ref, iq_ref, ikw_ref, v_ref,
                      tq_ref, tk_ref, td_ref, ti_ref,
                      qa_o, ka_o, qb_o, kb_o, iq_o, ikw_o, qst_o, iqst_o, khm_o, ikb_o, vt_o, iwt_o):
    tm = qa_ref.shape[0]
    h = DSA_ROT // 2
    qa_o[...] = _rope(qa_ref[...], tq_ref, RET_DK // 2)
    ka_o[...] = _rope(ka_ref[...], tk_ref, RET_DK // 2)
    qb = (_rope(qb_ref[...], td_ref, h) * (DSA_HD ** -0.5)).astype(BF16)
    qb_o[...] = qb
    kb = kb_ref[...]
    kb = (kb * td_ref[0, :, :LANE] + pltpu.roll(kb, LANE - h, 1) * td_ref[1, :, :LANE]
          + pltpu.roll(kb, h, 1) * td_ref[2, :, :LANE])
    kb_o[...] = kb
    iq = iq_ref[...]
    w = iq.shape[-1]
    iq = (iq * td_ref[0, :, :w] + pltpu.roll(iq, w - h, 1) * td_ref[1, :, :w]
          + pltpu.roll(iq, h, 1) * td_ref[2, :, :w]).astype(BF16)
    iq_o[...] = iq
    ikw = _rope(ikw_ref[...], ti_ref, h)
    ikw_o[...] = ikw
    group = DSA_HEADS // DSA_KV_HEADS
    for jb in range(tm // DSA_TQ):
        rs = slice(jb * DSA_TQ, (jb + 1) * DSA_TQ)
        for hd in range(DSA_HEADS):
            n, g = divmod(hd, group)
            ro = (jb * group + g) * DSA_TQ
            qst_o[n, ro:ro + DSA_TQ, :] = qb[rs, hd * DSA_HD:(hd + 1) * DSA_HD]
        for hd in range(IDX_HEADS):
            ro = (jb * IDX_HEADS + hd) * DSA_TQ
            iqst_o[ro:ro + DSA_TQ, :] = iq[rs, hd * IDX_DIM:(hd + 1) * IDX_DIM]
    kbb = kb.astype(BF16)
    for n in range(DSA_KV_HEADS):
        khm_o[n] = kbb[:, n * DSA_HD:(n + 1) * DSA_HD]
    ikb_o[...] = ikw[:, :IDX_DIM].astype(BF16)
    v = v_ref[...]
    for j in range(tm // DSA_TK):
        vt_o[j] = v[j * DSA_TK:(j + 1) * DSA_TK, :].T.astype(BF16)
    iwt_o[...] = ikw.T[IDX_DIM:IDX_DIM + 8, :]


def _even_prep(pe, tabs, tm, n_prompt_blocks, tab_blocks):
    n = pe.shape[0]
    tq, tk, td, ti = tabs

    def tix(i):
        return (0, jnp.where(i < n_prompt_blocks, i % tab_blocks, tab_blocks), 0)

    def col(wd, j):
        return pl.BlockSpec((tm, wd), lambda i: (i, j))

    def tab(wd):
        return pl.BlockSpec((3, tm, wd), tix)

    def out(wd):
        return pl.BlockSpec((tm, wd), lambda i: (i, 0))

    group = DSA_HEADS // DSA_KV_HEADS
    return pl.pallas_call(
        _even_prep_kernel,
        grid=(n // tm,),
        in_specs=[col(256, 0), col(256, 1), col(512, 3), col(128, 16), col(256, 9), col(128, 20),
                  col(128, 17), tab(256), tab(256), tab(512), tab(128)],
        out_specs=[out(256), out(256), out(512), out(128), out(256), out(128),
                   pl.BlockSpec((DSA_KV_HEADS, group * tm, DSA_HD), lambda i: (0, i, 0)),
                   pl.BlockSpec((IDX_HEADS * tm, IDX_DIM), lambda i: (i, 0)),
                   pl.BlockSpec((DSA_KV_HEADS, tm, DSA_HD), lambda i: (0, i, 0)),
                   pl.BlockSpec((tm, IDX_DIM), lambda i: (i, 0)),
                   pl.BlockSpec((tm // DSA_TK, LANE, DSA_TK), lambda i: (i, 0, 0)),
                   pl.BlockSpec((8, tm), lambda i: (0, i))],
        out_shape=[jax.ShapeDtypeStruct((n, 256), F32), jax.ShapeDtypeStruct((n, 256), F32),
                   jax.ShapeDtypeStruct((n, 512), BF16), jax.ShapeDtypeStruct((n, 128), F32),
                   jax.ShapeDtypeStruct((n, 256), BF16), jax.ShapeDtypeStruct((n, 128), F32),
                   jax.ShapeDtypeStruct((DSA_KV_HEADS, group * n, DSA_HD), BF16),
                   jax.ShapeDtypeStruct((IDX_HEADS * n, IDX_DIM), BF16),
                   jax.ShapeDtypeStruct((DSA_KV_HEADS, n, DSA_HD), BF16),
                   jax.ShapeDtypeStruct((n, IDX_DIM), BF16),
                   jax.ShapeDtypeStruct((n // DSA_TK, LANE, DSA_TK), BF16),
                   jax.ShapeDtypeStruct((8, n), F32)],
        compiler_params=_cparams(1),
        name="even_rope",
    )(pe, pe, pe, pe, pe, pe, pe, tq, tk, td, ti)


def _ret_kernel(gch_ref, q_ref, k_ref, v_ref, g_ref, dm_ref, qd_ref, kd_ref, gng_ref, gnb_ref,
                s0_ref, *rest):
    o_ref, s_ref, s_sc = rest[-3], rest[-2], rest[-1]
    c = pl.program_id(1)

    @pl.when(c == 0)
    def _():
        s_sc[...] = s0_ref[0]

    q = q_ref[...]
    k = k_ref[...]
    qx = (q * qd_ref[...]).astype(BF16)
    kw = (k * kd_ref[...]).astype(BF16)
    qb = q.astype(BF16)
    kb = k.astype(BF16)
    vb = v_ref[...].astype(BF16)
    gate = g_ref[...]
    for h in range(RET_HEADS):
        ks = slice(h * RET_DK, (h + 1) * RET_DK)
        vs = slice(h * RET_DV, (h + 1) * RET_DV)
        att = _dot_nt(qb[:, ks], kb[:, ks]) * dm_ref[h]
        s_old = s_sc[h]
        y = _dot(att.astype(BF16), vb[:, vs]) + _dot(qx[:, ks], s_old.astype(BF16))
        s_sc[h] = s_old * gch_ref[h] + _dot_tn(kw[:, ks], vb[:, vs])
        yn = _layer_norm(y, gng_ref[:, vs], gnb_ref[:, vs])
        o_ref[:, vs] = _silu(gate[:, vs]) * yn

    @pl.when(c == pl.num_programs(1) - 1)
    def _():
        s_ref[0] = s_sc[...]


def _retention(qa, ka, pe, s0, gn_g, gn_b, n_seq, t, row0, prev_out):
    n = qa.shape[0]
    lc = min(RET_LC, t)
    nc = t // lc
    blk0 = row0 // lc
    log_g = jnp.log(1.0 - 2.0 ** (-5.0 - jnp.arange(RET_HEADS, dtype=F32)))
    pos = jnp.arange(lc, dtype=F32)
    diff = pos[:, None] - pos[None, :]
    dmask = jnp.where(diff >= 0, jnp.exp(jnp.maximum(diff, 0.0)[None] * log_g[:, None, None]), 0.0)
    w_end = jnp.exp((lc - 1 - pos)[:, None] * log_g[None, :])
    xi = jnp.exp((pos + 1.0)[:, None] * log_g[None, :])
    kdec = jnp.repeat(w_end, RET_DK, axis=1)
    qdec = jnp.repeat(xi, RET_DK, axis=1)
    gch = jnp.exp(lc * log_g)

    def rows(wd, j):
        return pl.BlockSpec((lc, wd), lambda b, c: (blk0 + b * nc + c, j))

    def const(shape):
        nd = len(shape)
        return pl.BlockSpec(shape, lambda b, c: (0,) * nd)

    in_specs = [pl.BlockSpec(memory_space=pltpu.SMEM),
                rows(256, 0), rows(256, 0), rows(512, 1), rows(512, 2),
                const((RET_HEADS, lc, lc)), const((lc, 256)), const((lc, 256)),
                const((1, 512)), const((1, 512)),
                pl.BlockSpec((1, RET_HEADS, RET_DK, RET_DV), lambda b, c: (b, 0, 0, 0))]
    args = [gch, qa, ka, pe, pe, dmask, qdec, kdec,
            gn_g.reshape(1, 512), gn_b.reshape(1, 512), s0]
    aliases = {}
    if prev_out is not None:
        in_specs.append(pl.BlockSpec(memory_space=pl.ANY))
        args.append(prev_out)
        aliases = {len(args) - 1: 0}
    return pl.pallas_call(
        _ret_kernel,
        grid=(n_seq, nc),
        in_specs=in_specs,
        out_specs=[pl.BlockSpec((lc, 512), lambda b, c: (blk0 + b * nc + c, 0)),
                   pl.BlockSpec((1, RET_HEADS, RET_DK, RET_DV), lambda b, c: (b, 0, 0, 0))],
        out_shape=[jax.ShapeDtypeStruct((n, 512), F32),
                   jax.ShapeDtypeStruct((n_seq, RET_HEADS, RET_DK, RET_DV), F32)],
        scratch_shapes=[pltpu.VMEM((RET_HEADS, RET_DK, RET_DV), F32)],
        input_output_aliases=aliases,
        compiler_params=_cparams(2),
        name="retention",
    )(*args)


def _col_reduce(x, op):
    r, c = x.shape
    return op(op(x.reshape(r // 8, 8, c), axis=0), axis=0, keepdims=True)


def _dsa_kernel(nkb_ref, q_ref, iq_ref, iwt_ref, qlim_ref, k_ref, vt_ref, ik_ref, *rest,
                topk, tq_out):
    o_ref, key_sc, m_sc, l_sc, acc_sc = rest[-5:]
    nkb = nkb_ref[pl.program_id(1)]
    tq = qlim_ref.shape[1]
    tk = key_sc.shape[1]
    group = DSA_HEADS // DSA_KV_HEADS
    qlim = qlim_ref[...]
    iwt = iwt_ref[...]
    iqs = iq_ref[...]
    idx_scale = (IDX_HEADS * IDX_DIM) ** -0.5
    krow = lax.broadcasted_iota(I32, (tk, tq), 0)

    def score_body(kb, carry):
        off = pl.multiple_of(kb * tk, tk)
        s_all = _dot_nt(ik_ref[pl.ds(off, tk), :], iqs)
        s = jnp.zeros((tk, tq), F32)
        for h in range(IDX_HEADS):
            s = s + iwt[h:h + 1, :] * jnp.maximum(s_all[:, h * tq:(h + 1) * tq], 0.0)
        s = s * idx_scale
        s = jnp.where(s == 0.0, 0.0, s)
        bits = pltpu.bitcast(s, I32)
        key = jnp.where(bits >= 0, bits, bits ^ jnp.int32(0x7FFFFFFF))
        adm = (off + krow) < qlim
        key_sc[kb] = jnp.where(adm, key, jnp.int32(INT_MIN))
        return carry

    lax.fori_loop(0, nkb, score_body, 0)

    def count(pred):
        def body(kb, acc):
            hit = jnp.where(pred(key_sc[kb]), 1.0, 0.0)
            return acc + jnp.sum(hit.reshape(tk // 8, 8, tq), axis=0)
        acc = lax.fori_loop(0, nkb, body, jnp.zeros((8, tq), F32))
        return jnp.sum(acc, axis=0, keepdims=True)

    def bit_body(it, ans):
        cand = ans + (jnp.int32(1) << (31 - it))
        return jnp.where(count(lambda k: k >= cand) >= topk, cand, ans)

    t = lax.fori_loop(0, 32, bit_body, jnp.full((1, tq), INT_MIN, I32))
    need = topk - count(lambda k: k > t)

    m_sc[...] = jnp.full(m_sc.shape, NEG_BIG, F32)
    l_sc[...] = jnp.zeros(l_sc.shape, F32)
    acc_sc[...] = jnp.zeros(acc_sc.shape, F32)
    r_i = lax.broadcasted_iota(I32, (tk, tk), 0)
    c_i = lax.broadcasted_iota(I32, (tk, tk), 1)
    lower = (c_i < r_i).astype(BF16)

    def att_body(kb, n_eq):
        off = pl.multiple_of(kb * tk, tk)
        key = key_sc[kb]
        adm = (off + krow) < qlim
        eq = jnp.logical_and(key == t, adm)
        eqf = jnp.where(eq, 1.0, 0.0)
        pref = _dot(lower, eqf.astype(BF16))
        sel = jnp.logical_or(key > t, jnp.logical_and(eq, (n_eq + pref) < need))
        vt = vt_ref[kb]
        for n in range(DSA_KV_HEADS):
            lg_all = _dot_nt(k_ref[n, pl.ds(off, tk), :], q_ref[n])
            ps, alphas = [], []
            for g in range(group):
                ls = slice(g * tq, (g + 1) * tq)
                lg = lg_all[:, ls]
                m_old = m_sc[n, :, ls]
                m_new = jnp.maximum(m_old, _col_reduce(jnp.where(sel, lg, NEG_BIG), jnp.max))
                p = jnp.where(sel, jnp.exp(lg - m_new), 0.0)
                alpha = jnp.exp(m_old - m_new)
                l_sc[n, :, ls] = alpha * l_sc[n, :, ls] + _col_reduce(p, jnp.sum)
                m_sc[n, :, ls] = m_new
                ps.append(p.astype(BF16))
                alphas.append(alpha)
            p_all = jnp.concatenate(ps, axis=1)
            alpha_all = jnp.concatenate(alphas, axis=1)
            pv = _dot(vt[n * DSA_HD:(n + 1) * DSA_HD, :], p_all)
            acc_sc[n] = alpha_all * acc_sc[n] + pv
        return n_eq + _col_reduce(eqf, jnp.sum)

    lax.fori_loop(0, nkb, att_body, jnp.zeros((1, tq), F32))
    pieces = []
    for n in range(DSA_KV_HEADS):
        o_n = acc_sc[n] / l_sc[n]
        for g in range(group):
            pieces.append(o_n[:, g * tq:(g + 1) * tq])
    o_ref[...] = jnp.concatenate(pieces, axis=0).T[:tq_out, :]


def _dsa(q_st, iq_st, iw_t, qlim, nkb, k_hm, v_t, ik_bf, n_seq, nq, s_len, tq_out, row0, n_out,
         topk, prev_out):
    tq = DSA_TQ
    group = DSA_HEADS // DSA_KV_HEADS
    blk0 = row0 // tq_out
    in_specs = [pl.BlockSpec((DSA_KV_HEADS, group * tq, DSA_HD), lambda b, i, s: (0, b * nq + i, 0)),
                pl.BlockSpec((IDX_HEADS * tq, IDX_DIM), lambda b, i, s: (b * nq + i, 0)),
                pl.BlockSpec((8, tq), lambda b, i, s: (0, b * nq + i)),
                pl.BlockSpec((1, tq), lambda b, i, s: (0, i)),
                pl.BlockSpec((DSA_KV_HEADS, s_len, DSA_HD), lambda b, i, s: (0, b, 0)),
                pl.BlockSpec((s_len // DSA_TK, LANE, DSA_TK), lambda b, i, s: (b, 0, 0)),
                pl.BlockSpec((s_len, IDX_DIM), lambda b, i, s: (b, 0))]
    args = [nkb, q_st, iq_st, iw_t, qlim, k_hm, v_t, ik_bf]
    aliases = {}
    if prev_out is not None:
        in_specs.append(pl.BlockSpec(memory_space=pl.ANY))
        args.append(prev_out)
        aliases = {len(args) - 1: 0}
    grid_spec = pltpu.PrefetchScalarGridSpec(
        num_scalar_prefetch=1,
        grid=(n_seq, nq),
        in_specs=in_specs,
        out_specs=pl.BlockSpec((tq_out, 512), lambda b, i, s: (blk0 + b * nq + i, 0)),
        scratch_shapes=[pltpu.VMEM((s_len // DSA_TK, DSA_TK, tq), I32),
                        pltpu.VMEM((DSA_KV_HEADS, 1, group * tq), F32),
                        pltpu.VMEM((DSA_KV_HEADS, 1, group * tq), F32),
                        pltpu.VMEM((DSA_KV_HEADS, DSA_HD, group * tq), F32)])
    return pl.pallas_call(
        functools.partial(_dsa_kernel, topk=topk, tq_out=tq_out),
        grid_spec=grid_spec,
        out_shape=jax.ShapeDtypeStruct((n_out, 512), F32),
        input_output_aliases=aliases,
        compiler_params=_cparams(2),
        name="dsa_attention",
    )(*args)


def _outproj_ln_kernel(ya_ref, yb_ref, w_ref, h_ref, g_ref, b_ref, o_ref):
    half = ya_ref.shape[1]
    y = (_dot(ya_ref[...].astype(BF16), w_ref[:half, :])
         + _dot(yb_ref[...].astype(BF16), w_ref[half:, :]))
    o_ref[...] = _layer_norm(DN_ALPHA * h_ref[...] + y, g_ref[...], b_ref[...])


def _outproj_ln(ya, yb, w, h, g, b, tm):
    n = h.shape[0]
    return pl.pallas_call(
        _outproj_ln_kernel,
        grid=(n // tm,),
        in_specs=[pl.BlockSpec((tm, 512), lambda i: (i, 0)),
                  pl.BlockSpec((tm, 512), lambda i: (i, 0)),
                  pl.BlockSpec((D_MODEL, D_MODEL), lambda i: (0, 0)),
                  pl.BlockSpec((tm, D_MODEL), lambda i: (i, 0)),
                  pl.BlockSpec((1, D_MODEL), lambda i: (0, 0)),
                  pl.BlockSpec((1, D_MODEL), lambda i: (0, 0))],
        out_specs=pl.BlockSpec((tm, D_MODEL), lambda i: (i, 0)),
        out_shape=jax.ShapeDtypeStruct((n, D_MODEL), F32),
        compiler_params=_cparams(1),
        name="out_proj_ln",
    )(ya, yb, w, h, g.reshape(1, -1), b.reshape(1, -1))


def _router_kernel(x_ref, w_ref, b_ref, idx_o, gate_o, rank_o, cnt_o, cnt_sc):
    i = pl.program_id(0)

    @pl.when(i == 0)
    def _():
        cnt_sc[...] = jnp.zeros(cnt_sc.shape, F32)

    tm = x_ref.shape[0]
    lane = lax.broadcasted_iota(I32, (tm, LANE), 1)
    logits = _dot_f32(x_ref[...], w_ref[...]) + b_ref[...]
    logits = jnp.where(lane < N_EXPERTS, logits, -jnp.inf)
    vals, idxs = [], []
    onehot = jnp.zeros((tm, LANE), F32)
    for _ in range(TOP_K):
        m = jnp.max(logits, axis=1, keepdims=True)
        ix = jnp.min(jnp.where(logits == m, lane, LANE), axis=1, keepdims=True)
        hit = lane == ix
        onehot = jnp.where(hit, 1.0, onehot)
        logits = jnp.where(hit, -jnp.inf, logits)
        vals.append(m)
        idxs.append(ix)
    es = [jnp.exp(v - vals[0]) for v in vals]
    den = es[0] + es[1] + es[2] + es[3]
    r_i = lax.broadcasted_iota(I32, (tm, tm), 0)
    c_i = lax.broadcasted_iota(I32, (tm, tm), 1)
    lower = (c_i < r_i).astype(BF16)
    rank_dense = _dot(lower, onehot.astype(BF16)) + cnt_sc[...]
    idx_out = jnp.zeros((tm, LANE), I32)
    gate_out = jnp.zeros((tm, LANE), F32)
    rank_out = jnp.zeros((tm, LANE), F32)
    for k in range(TOP_K):
        rk = jnp.sum(jnp.where(lane == idxs[k], rank_dense, 0.0), axis=1, keepdims=True)
        idx_out = jnp.where(lane == k, idxs[k], idx_out)
        gate_out = jnp.where(lane == k, es[k] / den, gate_out)
        rank_out = jnp.where(lane == k, rk, rank_out)
    idx_o[...] = idx_out
    gate_o[...] = gate_out
    rank_o[...] = rank_out.astype(I32)
    cnt = cnt_sc[...] + jnp.sum(onehot, axis=0, keepdims=True)
    cnt_sc[...] = cnt
    cnt_o[...] = cnt


def _router(x, w_r, b_r, tm):
    n = x.shape[0]
    w = jnp.zeros((D_MODEL, LANE), F32).at[:, :N_EXPERTS].set(w_r)
    b = jnp.zeros((1, LANE), F32).at[0, :N_EXPERTS].set(b_r)
    row = pl.BlockSpec((tm, LANE), lambda i: (i, 0))
    return pl.pallas_call(
        _router_kernel,
        grid=(n // tm,),
        in_specs=[pl.BlockSpec((tm, D_MODEL), lambda i: (i, 0)),
                  pl.BlockSpec((D_MODEL, LANE), lambda i: (0, 0)),
                  pl.BlockSpec((1, LANE), lambda i: (0, 0))],
        out_specs=[row, row, row, pl.BlockSpec((1, LANE), lambda i: (0, 0))],
        out_shape=[jax.ShapeDtypeStruct((n, LANE), I32), jax.ShapeDtypeStruct((n, LANE), F32),
                   jax.ShapeDtypeStruct((n, LANE), I32), jax.ShapeDtypeStruct((1, LANE), F32)],
        scratch_shapes=[pltpu.VMEM((1, LANE), F32)],
        compiler_params=_cparams(1),
        name="moe_router",
    )(x, w, b)


def _moe_kernel(be_ref, nu_ref, x_ref, wgu_ref, bgu_ref, wdn_ref, bdn_ref, o_ref, wgu_sc, wdn_sc):
    i = pl.program_id(0)

    @pl.when(jnp.logical_or(i == 0, be_ref[i] != be_ref[jnp.maximum(i - 1, 0)]))
    def _():
        wgu_sc[...] = wgu_ref[0, 0].astype(BF16)
        wdn_sc[...] = wdn_ref[0, 0].astype(BF16)

    @pl.when(i < nu_ref[0])
    def _():
        h = _dot(x_ref[...].astype(BF16), wgu_sc[...]) + bgu_ref[0, 0]
        g = jnp.minimum(h[:, :D_FF], SWIGLU_LIMIT)
        up = jnp.clip(h[:, D_FF:], -SWIGLU_LIMIT, SWIGLU_LIMIT)
        a = (up + 1.0) * g * jax.nn.sigmoid(SWIGLU_ALPHA * g)
        o_ref[...] = _dot(a.astype(BF16), wdn_sc[...]) + bdn_ref[0, 0]

    @pl.when(i >= nu_ref[0])
    def _():
        o_ref[...] = jnp.zeros(o_ref.shape, F32)


def _moe_experts(xs, blk_e, n_used, layer, w_gu, b_gu, w_dn, b_dn):
    n_rows = xs.shape[0]
    tm = MOE_TM
    depth = w_gu.shape[0]
    grid_spec = pltpu.PrefetchScalarGridSpec(
        num_scalar_prefetch=2,
        grid=(n_rows // tm,),
        in_specs=[pl.BlockSpec((tm, D_MODEL), lambda i, be, nu: (i, 0)),
                  pl.BlockSpec((1, 1, D_MODEL, 2 * D_FF), lambda i, be, nu: (layer, be[i], 0, 0)),
                  pl.BlockSpec((1, 1, 1, 2 * D_FF), lambda i, be, nu: (layer, be[i], 0, 0)),
                  pl.BlockSpec((1, 1, D_FF, D_MODEL), lambda i, be, nu: (layer, be[i], 0, 0)),
                  pl.BlockSpec((1, 1, 1, D_MODEL), lambda i, be, nu: (layer, be[i], 0, 0))],
        out_specs=pl.BlockSpec((tm, D_MODEL), lambda i, be, nu: (i, 0)),
        scratch_shapes=[pltpu.VMEM((D_MODEL, 2 * D_FF), BF16), pltpu.VMEM((D_FF, D_MODEL), BF16)])
    return pl.pallas_call(
        _moe_kernel,
        grid_spec=grid_spec,
        out_shape=jax.ShapeDtypeStruct((n_rows, D_MODEL), F32),
        compiler_params=_cparams(1),
        name="moe_experts",
    )(blk_e, n_used, xs, w_gu, b_gu.reshape(depth, N_EXPERTS, 1, -1), w_dn,
      b_dn.reshape(depth, N_EXPERTS, 1, -1))


def _combine_ln_kernel(h_ref, y0_ref, y1_ref, y2_ref, y3_ref, gate_ref, g_ref, b_ref, o_ref):
    gate = gate_ref[...]
    y = (gate[:, 0:1] * y0_ref[...] + gate[:, 1:2] * y1_ref[...]
         + gate[:, 2:3] * y2_ref[...] + gate[:, 3:4] * y3_ref[...])
    o_ref[...] = _layer_norm(DN_ALPHA * h_ref[...] + y, g_ref[...], b_ref[...])


def _combine_ln(h, ys, gate, g, b, tm):
    n = h.shape[0]
    row = pl.BlockSpec((tm, D_MODEL), lambda i: (i, 0))
    vec = pl.BlockSpec((1, D_MODEL), lambda i: (0, 0))
    return pl.pallas_call(
        _combine_ln_kernel,
        grid=(n // tm,),
        in_specs=[row, row, row, row, row, pl.BlockSpec((tm, LANE), lambda i: (i, 0)), vec, vec],
        out_specs=row,
        out_shape=jax.ShapeDtypeStruct((n, D_MODEL), F32),
        compiler_params=_cparams(1),
        name="moe_combine_ln",
    )(h, ys[0], ys[1], ys[2], ys[3], gate, g.reshape(1, -1), b.reshape(1, -1))


def _rows(x, idx):
    return x.at[idx].get(mode="promise_in_bounds")


def _moe_layer(h, layer, w_r, b_r, w_gu, b_gu, w_dn, b_dn, ln_g, ln_b, tm):
    n = h.shape[0]
    n_pair = n * TOP_K
    idx, gate, rank, cnt = _router(h, w_r, b_r, tm)
    top_i = idx[:, :TOP_K]
    counts = cnt[0, :N_EXPERTS].astype(I32)
    padded = (counts + MOE_TM - 1) // MOE_TM * MOE_TM
    pad_end = jnp.cumsum(padded)
    start = pad_end - padded
    first = jnp.cumsum(counts) - counts
    dest = _rows(start, top_i) + rank[:, :TOP_K]
    n_blk = -(-n_pair // MOE_TM) + N_EXPERTS
    n_used = (pad_end[-1] // MOE_TM).astype(I32)
    blk_row = jnp.minimum(jnp.arange(n_blk, dtype=I32), n_used - 1) * MOE_TM
    blk_e = jnp.sum((pad_end[None, :] <= blk_row[:, None]).astype(I32), axis=1)
    blk_e = jnp.minimum(blk_e, N_EXPERTS - 1)
    order = jnp.argsort(top_i.reshape(-1), stable=True).astype(I32)
    row_in_e = (jnp.arange(n_blk, dtype=I32)[:, None] * MOE_TM - _rows(start, blk_e)[:, None]
                + jnp.arange(MOE_TM, dtype=I32)[None, :])
    pair = jnp.clip(_rows(first, blk_e)[:, None] + row_in_e, 0, n_pair - 1).reshape(-1)
    src = _rows(order, pair) // TOP_K
    xs = _rows(h, src)
    ybuf = _moe_experts(xs, blk_e, n_used.reshape(1), layer, w_gu, b_gu, w_dn, b_dn)
    ys = [_rows(ybuf, dest[:, k]) for k in range(TOP_K)]
    return _combine_ln(h, ys, gate, ln_g, ln_b, tm)


def _band_kernel(q_ref, *rest, nkb):
    k_refs = rest[:nkb]
    v_refs = rest[nkb:2 * nkb]
    bias_ref = rest[2 * nkb]
    o_ref = rest[-1]
    c = pl.program_id(1)
    tq = q_ref.shape[0]
    tkb = k_refs[0].shape[0]
    scale = BAND_HD ** -0.5
    qb = q_ref[...].astype(BF16)
    kbs = [r[...].astype(BF16) for r in k_refs]
    vbs = [r[...].astype(BF16) for r in v_refs]
    valid = [(c + j - (nkb - 1)) >= 0 for j in range(nkb)]
    for h in range(BAND_HEADS):
        hs = slice(h * BAND_HD, (h + 1) * BAND_HD)
        lgs = []
        for j in range(nkb):
            lg = _dot_nt(qb[:, hs], kbs[j][:, hs]) * scale + bias_ref[h, :, j * tkb:(j + 1) * tkb]
            lgs.append(jnp.where(valid[j], lg, NEG_BIG))
        m = lgs[0].max(axis=1, keepdims=True)
        for j in range(1, nkb):
            m = jnp.maximum(m, lgs[j].max(axis=1, keepdims=True))
        ps = [jnp.exp(lg - m) for lg in lgs]
        den = ps[0].sum(axis=1, keepdims=True)
        for j in range(1, nkb):
            den = den + ps[j].sum(axis=1, keepdims=True)
        inv = 1.0 / den
        acc = _dot((ps[0] * inv).astype(BF16), vbs[0][:, hs])
        for j in range(1, nkb):
            acc = acc + _dot((ps[j] * inv).astype(BF16), vbs[j][:, hs])
        o_ref[:, hs] = acc


def _band(q_arr, k_arr, v_arr, cols, bias, n_seq, t, tq, tkb, nkb, q_row0, kv_blocks_per_seq,
          n_out, prev_out):
    nq = t // tq
    qblk0 = q_row0 // tq
    qcol, kcol, vcol = cols

    def kv_spec(j, col):
        def ix(b, c):
            return (b * kv_blocks_per_seq + jnp.maximum(c + j - (nkb - 1), 0), col)
        return pl.BlockSpec((tkb, 512), ix)

    in_specs = ([pl.BlockSpec((tq, 512), lambda b, c: (qblk0 + b * nq + c, qcol))]
                + [kv_spec(j, kcol) for j in range(nkb)]
                + [kv_spec(j, vcol) for j in range(nkb)]
                + [pl.BlockSpec(bias.shape, lambda b, c: (0, 0, 0))])
    args = [q_arr] + [k_arr] * nkb + [v_arr] * nkb + [bias]
    aliases = {}
    if prev_out is not None:
        in_specs.append(pl.BlockSpec(memory_space=pl.ANY))
        args.append(prev_out)
        aliases = {len(args) - 1: 0}
    return pl.pallas_call(
        functools.partial(_band_kernel, nkb=nkb),
        grid=(n_seq, nq),
        in_specs=in_specs,
        out_specs=pl.BlockSpec((tq, 512), lambda b, c: (qblk0 + b * nq + c, 0)),
        out_shape=jax.ShapeDtypeStruct((n_out, 512), F32),
        input_output_aliases=aliases,
        compiler_params=_cparams(2),
        name="band_attention",
    )(*args)


def _band_bias(rel_bias, tq, n_keys, key0):
    n_off = tq + n_keys - 1
    d_min = -key0 - (n_keys - 1)
    rel = np.clip(d_min + np.arange(n_off), -REL_CLIP, REL_CLIP) + REL_CLIP
    vals = rel_bias[:, rel]
    hank = jnp.tile(vals, (1, tq + 1))[:, :tq * (n_off + 1)].reshape(-1, tq, n_off + 1)[:, :, :n_keys]
    toep = hank[:, :, ::-1]
    qp = np.arange(tq)[:, None]
    kp = key0 + np.arange(n_keys)[None, :]
    cs = (qp // CHUNK) * CHUNK
    band = np.logical_and(kp >= cs - BAND_PAST, kp < cs + CHUNK)
    return jnp.where(jnp.asarray(band)[None], toep, NEG_BIG).astype(F32)


def _ssd_kernel(dsk_ref, z_ref, xbc_ref, dt_ref, cw_ref, cb_ref, dtb_ref, alog_ref, ng_ref,
                h0_ref, c0_ref, *rest):
    o_ref, h_ref, cl_ref, h_sc, xe_sc, y_sc = rest[-6:]
    c = pl.program_id(1)
    lc = z_ref.shape[0]

    @pl.when(c == 0)
    def _():
        h_sc[...] = h0_ref[0]
        xe_sc[0:8, :] = jnp.zeros((8, xe_sc.shape[1]), F32)
        xe_sc[8 - (SSD_CONV - 1):8, :] = c0_ref[0]

    xe_sc[8:8 + lc, :] = xbc_ref[...]
    conv = cb_ref[...] + cw_ref[SSD_CONV - 1:SSD_CONV, :] * xe_sc[8:8 + lc, :]
    for s in range(1, SSD_CONV):
        conv = conv + cw_ref[SSD_CONV - 1 - s:SSD_CONV - s, :] * xe_sc[8 - s:8 - s + lc, :]
    u = _silu(conv)
    gs = SSD_GROUPS * SSD_STATE
    xs = u[:, :SSD_INNER]
    bm = u[:, SSD_INNER:SSD_INNER + gs].astype(BF16)
    cm = u[:, SSD_INNER + gs:].astype(BF16)
    dx = dt_ref[...] + dtb_ref[...]
    dtv = jnp.maximum(dx, 0.0) + jnp.log1p(jnp.exp(-jnp.abs(dx)))
    a = dtv * (-jnp.exp(alog_ref[...]))
    r_i = lax.broadcasted_iota(I32, (lc, lc), 0)
    c_i = lax.broadcasted_iota(I32, (lc, lc), 1)
    causal = c_i <= r_i
    acum = _dot_f32(causal.astype(F32), a)
    acum_t = acum.T
    hpg = SSD_HEADS // SSD_GROUPS
    for g in range(SSD_GROUPS):
        ss = slice(g * SSD_STATE, (g + 1) * SSD_STATE)
        cb = _dot_nt(cm[:, ss], bm[:, ss])
        for jj in range(hpg):
            j = g * hpg + jj
            ps = slice(j * SSD_HD, (j + 1) * SSD_HD)
            col = acum[:, j:j + 1]
            row = acum_t[j:j + 1, :]
            lmat = jnp.exp(jnp.where(causal, col - row, -jnp.inf))
            x_j = xs[:, ps]
            xdt = x_j * dtv[:, j:j + 1]
            h_old = h_sc[j]
            y = _dot((cb * lmat).astype(BF16), xdt.astype(BF16))
            y = y + _dot_nt(cm[:, ss], h_old.astype(BF16)) * jnp.exp(col)
            y = y + dsk_ref[j] * x_j
            last = acum[lc - 1:lc, j:j + 1]
            st = _dot_tn((xdt * jnp.exp(last - col)).astype(BF16), bm[:, ss])
            h_sc[j] = h_old * jnp.exp(last) + st
            y_sc[:, ps] = y
    yd = y_sc[...] * _silu(z_ref[...])
    gw = SSD_INNER // SSD_GROUPS
    for g in range(SSD_GROUPS):
        ws = slice(g * gw, (g + 1) * gw)
        yg = yd[:, ws]
        ms = jnp.mean(yg * yg, axis=-1, keepdims=True)
        o_ref[:, ws] = yg * lax.rsqrt(ms + LN_EPS) * ng_ref[:, ws]
    xe_sc[0:8, :] = xe_sc[lc:lc + 8, :]

    @pl.when(c == pl.num_programs(1) - 1)
    def _():
        h_ref[0] = h_sc[...]
        cl_ref[0] = xe_sc[8 - (SSD_CONV - 1):8, :]


def _ssd(po, h0, c0, conv_w, conv_b, dt_bias, a_log, d_skip, norm_g, n_seq, t, row0, prev_out):
    n = po.shape[0]
    lc = min(SSD_LC, t)
    nc = t // lc
    blk0 = row0 // lc
    cdim = conv_w.shape[1]

    def rows(wd, j):
        return pl.BlockSpec((lc, wd), lambda b, c: (blk0 + b * nc + c, j))

    def const(shape):
        nd = len(shape)
        return pl.BlockSpec(shape, lambda b, c: (0,) * nd)

    pad8 = lambda v: jnp.zeros((1, LANE), F32).at[0, :SSD_HEADS].set(v)
    in_specs = [pl.BlockSpec(memory_space=pltpu.SMEM),
                rows(512, 3), rows(cdim, 2), rows(LANE, 24),
                const((SSD_CONV, cdim)), const((1, cdim)), const((1, LANE)), const((1, LANE)),
                const((1, SSD_INNER)),
                pl.BlockSpec((1, SSD_HEADS, SSD_HD, SSD_STATE), lambda b, c: (b, 0, 0, 0)),
                pl.BlockSpec((1, SSD_CONV - 1, cdim), lambda b, c: (b, 0, 0))]
    args = [d_skip, po, po, po, conv_w, conv_b.reshape(1, -1), pad8(dt_bias), pad8(a_log),
            norm_g.reshape(1, -1), h0, c0]
    aliases = {}
    if prev_out is not None:
        in_specs.append(pl.BlockSpec(memory_space=pl.ANY))
        args.append(prev_out)
        aliases = {len(args) - 1: 0}
    return pl.pallas_call(
        _ssd_kernel,
        grid=(n_seq, nc),
        in_specs=in_specs,
        out_specs=[pl.BlockSpec((lc, 512), lambda b, c: (blk0 + b * nc + c, 0)),
                   pl.BlockSpec((1, SSD_HEADS, SSD_HD, SSD_STATE), lambda b, c: (b, 0, 0, 0)),
                   pl.BlockSpec((1, SSD_CONV - 1, cdim), lambda b, c: (b, 0, 0))],
        out_shape=[jax.ShapeDtypeStruct((n, 512), F32),
                   jax.ShapeDtypeStruct((n_seq, SSD_HEADS, SSD_HD, SSD_STATE), F32),
                   jax.ShapeDtypeStruct((n_seq, SSD_CONV - 1, cdim), F32)],
        scratch_shapes=[pltpu.VMEM((SSD_HEADS, SSD_HD, SSD_STATE), F32),
                        pltpu.VMEM((lc + 8, cdim), F32),
                        pltpu.VMEM((lc, 512), F32)],
        input_output_aliases=aliases,
        compiler_params=_cparams(2),
        name="ssd_scan",
    )(*args)


def _pad_cols(w, width):
    return jnp.concatenate([w, jnp.zeros((w.shape[0], width - w.shape[1]), w.dtype)], axis=1)


def kernel(x_prompt, x_sample, state_ret, cache_dsa_k, cache_dsa_v, cache_dsa_kidx, cache_band_k, cache_band_v, state_ssm, state_conv, e_w_in, e_w_out, e_gn_g, e_gn_b, o_w_in, o_w_out, o_rel_bias, o_conv_w, o_conv_b, o_dt_bias, o_a_log, o_d_skip, o_norm_g, ln1_g, ln1_b, ln2_g, ln2_b, router_w, router_b, exp_w_gu, exp_b_gu, exp_w_dn, exp_b_dn):
    bp, tp, _ = x_prompt.shape
    bs, ts, _ = x_sample.shape
    past = cache_dsa_k.shape[2]
    n_p, n_s = bp * tp, bs * ts
    n = n_p + n_s
    tm = math.gcd(512, math.gcd(n_p, n_s))
    assert tp % tm == 0 and tm % ts == 0 and ts == CHUNK

    h = jnp.concatenate([x_prompt.reshape(n_p, D_MODEL), x_sample.reshape(n_s, D_MODEL)], axis=0)

    pe = _proj(h, _pad_cols(e_w_in[0], EVEN_W).astype(BF16), tm)
    pos_p = jnp.arange(tp, dtype=I32)
    pos_s = past + jnp.arange(ts, dtype=I32)
    pos_tab = jnp.concatenate([pos_p, jnp.tile(pos_s, tm // ts)])
    tabs = (_rope_tables(pos_tab, RET_HEADS, RET_DK, RET_DK, RET_THETA),
            _rope_tables(pos_tab, RET_HEADS, RET_DK, RET_DK, RET_THETA, scale=RET_DK ** -0.5),
            _rope_tables(pos_tab, DSA_HEADS, DSA_HD, DSA_ROT, ROPE_THETA),
            _rope_tables(pos_tab, 1, IDX_DIM, DSA_ROT, ROPE_THETA, pad_to=LANE))
    (qa, ka, qb, kb, iq, ikw, q_st, iq_st, k_hm, ik_bf, v_t, iw_t) = _even_prep(
        pe, tabs, tm, n_p // tm, tp // tm)

    ya, ret_p = _retention(qa, ka, pe, jnp.zeros((bp, RET_HEADS, RET_DK, RET_DV), F32),
                           e_gn_g[0], e_gn_b[0], bp, tp, 0, None)
    ya, ret_s = _retention(qa, ka, pe, state_ret[0], e_gn_g[0], e_gn_b[0], bs, ts, n_p, ya)

    topk_p = min(DSA_TOPK_MAX, tp // 4)
    qlim_p = (((pos_p // CHUNK) + 1) * CHUNK).reshape(1, tp)
    nq_p = tp // DSA_TQ
    nkb_p = ((jnp.arange(nq_p, dtype=I32) + 1) * DSA_TQ + DSA_TK - 1) // DSA_TK
    yb = _dsa(q_st, iq_st, iw_t, qlim_p, nkb_p, k_hm, v_t, ik_bf, bp, nq_p, tp, DSA_TQ, 0, n,
              topk_p, None)

    s_len = past + ts
    s_pad = -(-s_len // DSA_TK) * DSA_TK
    topk_s = min(DSA_TOPK_MAX, s_len // 4)
    group = DSA_HEADS // DSA_KV_HEADS

    def cat_keys(cache, new, wd):
        zpad = jnp.zeros((bs, s_pad - s_len, wd), F32)
        return jnp.concatenate([cache, new.reshape(bs, ts, wd), zpad], axis=1)

    def pad_q(x):
        return jnp.concatenate([x, jnp.zeros((bs, DSA_TQ - ts) + x.shape[2:], x.dtype)], axis=1)

    ks = cat_keys(cache_dsa_k[0].reshape(bs, past, LANE), kb[n_p:], LANE)
    vs = cat_keys(cache_dsa_v[0].reshape(bs, past, LANE), pe[n_p:, 2176:2304], LANE)
    iks = cat_keys(cache_dsa_kidx[0], ikw[n_p:, :IDX_DIM], IDX_DIM)
    k_hm_s = ks.reshape(bs, s_pad, DSA_KV_HEADS, DSA_HD).transpose(2, 0, 1, 3).reshape(
        DSA_KV_HEADS, bs * s_pad, DSA_HD).astype(BF16)
    v_t_s = vs.reshape(bs, s_pad // DSA_TK, DSA_TK, LANE).transpose(0, 1, 3, 2).reshape(
        bs * (s_pad // DSA_TK), LANE, DSA_TK).astype(BF16)
    ik_s = iks.reshape(bs * s_pad, IDX_DIM).astype(BF16)
    q_s = pad_q(qb[n_p:].reshape(bs, ts, DSA_KV_HEADS, group, DSA_HD))
    q_st_s = q_s.transpose(2, 0, 3, 1, 4).reshape(DSA_KV_HEADS, bs * group * DSA_TQ, DSA_HD)
    iq_s = pad_q(iq[n_p:].reshape(bs, ts, IDX_HEADS, IDX_DIM))
    iq_st_s = iq_s.transpose(0, 2, 1, 3).reshape(bs * IDX_HEADS * DSA_TQ, IDX_DIM)
    iw_t_s = pad_q(ikw[n_p:, IDX_DIM:IDX_DIM + 8].reshape(bs, ts, 8)).reshape(bs * DSA_TQ, 8).T
    qlim_s = jnp.full((1, DSA_TQ), s_len, I32)
    nkb_s = jnp.full((1,), s_pad // DSA_TK, I32)
    yb = _dsa(q_st_s, iq_st_s, iw_t_s, qlim_s, nkb_s, k_hm_s, v_t_s, ik_s, bs, 1, s_pad, ts, n_p, n,
              topk_s, yb)

    h = _outproj_ln(ya, yb, e_w_out[0].astype(BF16), h, ln1_g[0], ln1_b[0], tm)
    h = _moe_layer(h, 0, router_w[0], router_b[0], exp_w_gu, exp_b_gu, exp_w_dn, exp_b_dn,
                   ln2_g[0], ln2_b[0], tm)

    po = _proj(h, _pad_cols(o_w_in[0], ODD_W).astype(BF16), tm)
    tq_p = min(BAND_TQ, tp)
    nkb_band = BAND_PAST // tq_p + 1
    bias_p = _band_bias(o_rel_bias[0], tq_p, nkb_band * tq_p, -(nkb_band - 1) * tq_p)
    yc = _band(po, po, po, (0, 1, 2), bias_p, bp, tp, tq_p, tq_p, nkb_band, 0, tp // tq_p, n, None)
    band_len = cache_band_k.shape[2]
    kc_new = po[n_p:, 512:1024].reshape(bs, ts, 512)
    vc_new = po[n_p:, 1024:1536].reshape(bs, ts, 512)
    kcat = jnp.concatenate([cache_band_k[0].reshape(bs, band_len, 512), kc_new], axis=1)
    vcat = jnp.concatenate([cache_band_v[0].reshape(bs, band_len, 512), vc_new], axis=1)
    wlen = band_len + ts
    bias_s = _band_bias(o_rel_bias[0], ts, wlen, -band_len)
    yc = _band(po, kcat.reshape(bs * wlen, 512), vcat.reshape(bs * wlen, 512), (0, 0, 0), bias_s,
               bs, ts, ts, wlen, 1, n_p, 1, n, yc)

    ssd_w = (o_conv_w[0], o_conv_b[0], o_dt_bias[0], o_a_log[0], o_d_skip[0], o_norm_g[0])
    cdim = o_conv_w.shape[2]
    yd, ssm_p, conv_p = _ssd(po, jnp.zeros((bp, SSD_HEADS, SSD_HD, SSD_STATE), F32),
                             jnp.zeros((bp, SSD_CONV - 1, cdim), F32), *ssd_w, bp, tp, 0, None)
    yd, ssm_s, conv_s = _ssd(po, state_ssm[0], state_conv[0], *ssd_w, bs, ts, n_p, yd)

    h = _outproj_ln(yc, yd, o_w_out[0].astype(BF16), h, ln1_g[1], ln1_b[1], tm)
    h = _moe_layer(h, 1, router_w[1], router_b[1], exp_w_gu, exp_b_gu, exp_w_dn, exp_b_dn,
                   ln2_g[1], ln2_b[1], tm)

    keep = min(BAND_PAST, tp)
    kd = DSA_KV_HEADS * DSA_HD
    kc_p = po[:n_p, 512:1024].reshape(bp, tp, BAND_HEADS, BAND_HD)[:, -keep:]
    vc_p = po[:n_p, 1024:1536].reshape(bp, tp, BAND_HEADS, BAND_HD)[:, -keep:]
    return (h[:n_p].reshape(bp, tp, D_MODEL), h[n_p:].reshape(bs, ts, D_MODEL),
            ret_p[None],
            kb[:n_p].reshape(1, bp, tp, DSA_KV_HEADS, DSA_HD),
            pe[:n_p, 2176:2176 + kd].reshape(1, bp, tp, DSA_KV_HEADS, DSA_HD),
            ikw[:n_p, :IDX_DIM].reshape(1, bp, tp, IDX_DIM),
            kc_p[None], vc_p[None], ssm_p[None], conv_p[None],
            ret_s[None],
            kb[n_p:].reshape(1, bs, ts, DSA_KV_HEADS, DSA_HD),
            pe[n_p:, 2176:2176 + kd].reshape(1, bs, ts, DSA_KV_HEADS, DSA_HD),
            ikw[n_p:, :IDX_DIM].reshape(1, bs, ts, IDX_DIM),
            kc_new.reshape(1, bs, ts, BAND_HEADS, BAND_HD), vc_new.reshape(1, bs, ts, BAND_HEADS, BAND_HD),
            ssm_s[None], conv_s[None])
```

```python
import functools
import math

import jax
import jax.numpy as jnp
import numpy as np
from jax import lax
from jax.experimental import pallas as pl
from jax.experimental.pallas import tpu as pltpu

F32 = jnp.float32
BF16 = jnp.bfloat16
I32 = jnp.int32

D_MODEL = 1024
CHUNK = 64
RET_HEADS, RET_DK, RET_DV, RET_THETA = 8, 32, 64, 10000.0
DSA_HEADS, DSA_KV_HEADS, DSA_HD = 8, 2, 64
DSA_ROT = DSA_HD // 4
IDX_HEADS, IDX_DIM = 4, 64
DSA_TOPK_MAX = 256
ROPE_THETA = 500000.0
BAND_HEADS, BAND_HD, BAND_PREV = 8, 64, 8
BAND_PAST = BAND_PREV * CHUNK
REL_CLIP = 256
SSD_HEADS, SSD_HD, SSD_GROUPS, SSD_STATE, SSD_CONV = 8, 64, 2, 128, 4
SSD_INNER = SSD_HEADS * SSD_HD
N_EXPERTS, TOP_K, D_FF = 32, 4, 1024
SWIGLU_LIMIT, SWIGLU_ALPHA = 7.0, 1.702
DEPTH = 2
DN_ALPHA = (2 * DEPTH) ** 0.25
LN_EPS = 1e-5

LANE = 128
VMEM_LIMIT = 56 * 1024 * 1024
INT_MIN = -(2 ** 31)
NEG_BIG = -1e30

EVEN_IN = 2628
EVEN_W = 2688
ODD_IN = 3080
ODD_W = 3200

MOE_TM = 512
RET_LC = 256
SSD_LC = 256
DSA_TQ = 128
DSA_TK = 256
BAND_TQ = 256


def _cparams(n_axes):
    return pltpu.CompilerParams(dimension_semantics=("arbitrary",) * n_axes,
                                vmem_limit_bytes=VMEM_LIMIT)


def _dot(a, b):
    return jnp.dot(a, b, preferred_element_type=F32)


def _dot_nt(a, b):
    return lax.dot_general(a, b, (((1,), (1,)), ((), ())), preferred_element_type=F32)


def _dot_tn(a, b):
    return lax.dot_general(a, b, (((0,), (0,)), ((), ())), preferred_element_type=F32)


def _dot_f32(a, b):
    return jnp.dot(a, b, preferred_element_type=F32, precision=lax.Precision.HIGHEST)


def _layer_norm(x, g, b):
    mu = jnp.mean(x, axis=-1, keepdims=True)
    xc = x - mu
    var = jnp.mean(xc * xc, axis=-1, keepdims=True)
    return xc * lax.rsqrt(var + LN_EPS) * g + b


def _silu(x):
    return x * jax.nn.sigmoid(x)


def _proj_kernel(x_ref, w_ref, o_ref):
    o_ref[...] = _dot(x_ref[...].astype(BF16), w_ref[...])


def _proj(x, w, tm):
    n, k = x.shape
    wd = w.shape[1]
    return pl.pallas_call(
        _proj_kernel,
        grid=(n // tm,),
        in_specs=[pl.BlockSpec((tm, k), lambda i: (i, 0)),
                  pl.BlockSpec((k, wd), lambda i: (0, 0))],
        out_specs=pl.BlockSpec((tm, wd), lambda i: (i, 0)),
        out_shape=jax.ShapeDtypeStruct((n, wd), F32),
        compiler_params=_cparams(1),
        name="in_proj",
    )(x, w)


def _rope_tables(pos, n_heads, d, rot, theta, scale=1.0, pad_to=None):
    half = rot // 2
    inv = theta ** (-jnp.arange(half, dtype=F32) / half)
    ang = pos.astype(F32)[:, None] * inv[None, :]
    cos, sin = jnp.cos(ang), jnp.sin(ang)
    p = pos.shape[0]
    one = jnp.ones((p, d - rot), F32)
    zr = jnp.zeros((p, d - rot), F32)
    zh = jnp.zeros((p, half), F32)
    c = jnp.tile(jnp.concatenate([cos, cos, one], 1), (1, n_heads))
    a = jnp.tile(jnp.concatenate([-sin, zh, zr], 1), (1, n_heads))
    b = jnp.tile(jnp.concatenate([zh, sin, zr], 1), (1, n_heads))
    if pad_to is not None and pad_to > n_heads * d:
        extra = pad_to - n_heads * d
        c = jnp.concatenate([c, jnp.ones((p, extra), F32)], 1)
        a = jnp.concatenate([a, jnp.zeros((p, extra), F32)], 1)
        b = jnp.concatenate([b, jnp.zeros((p, extra), F32)], 1)
    return jnp.stack([c, a, b]) * scale


def _rope(x, tab_ref, half):
    w = x.shape[-1]
    return (x * tab_ref[0] + pltpu.roll(x, w - half, 1) * tab_ref[1]
            + pltpu.roll(x, half, 1) * tab_ref[2])


def _even_prep_kernel(qa_ref, ka_ref, qb_ref, kb_ref, iq_ref, ikw_ref, v_ref,
                      tq_ref, tk_ref, td_ref, ti_ref,
                      qa_o, ka_o, qb_o, kb_o, iq_o, ikw_o, qst_o, iqst_o, khm_o, ikb_o, vt_o, iwt_o):
    tm = qa_ref.shape[0]
    h = DSA_ROT // 2
    qa_o[...] = _rope(qa_ref[...], tq_ref, RET_DK // 2)
    ka_o[...] = _rope(ka_ref[...], tk_ref, RET_DK // 2)
    qb = (_rope(qb_ref[...], td_ref, h) * (DSA_HD ** -0.5 * math.log2(math.e))).astype(BF16)
    qb_o[...] = qb
    kb = kb_ref[...]
    kb = (kb * td_ref[0, :, :LANE] + pltpu.roll(kb, LANE - h, 1) * td_ref[1, :, :LANE]
          + pltpu.roll(kb, h, 1) * td_ref[2, :, :LANE])
    kb_o[...] = kb
    iq = iq_ref[...]
    w = iq.shape[-1]
    iq = (iq * td_ref[0, :, :w] + pltpu.roll(iq, w - h, 1) * td_ref[1, :, :w]
          + pltpu.roll(iq, h, 1) * td_ref[2, :, :w]).astype(BF16)
    iq_o[...] = iq
    ikw = _rope(ikw_ref[...], ti_ref, h)
    ikw_o[...] = ikw
    group = DSA_HEADS // DSA_KV_HEADS
    for jb in range(tm // DSA_TQ):
        rs = slice(jb * DSA_TQ, (jb + 1) * DSA_TQ)
        for hd in range(DSA_HEADS):
            n, g = divmod(hd, group)
            ro = (jb * group + g) * DSA_TQ
            qst_o[n, ro:ro + DSA_TQ, :] = qb[rs, hd * DSA_HD:(hd + 1) * DSA_HD]
        for hd in range(IDX_HEADS):
            ro = (jb * IDX_HEADS + hd) * DSA_TQ
            iqst_o[ro:ro + DSA_TQ, :] = iq[rs, hd * IDX_DIM:(hd + 1) * IDX_DIM]
    kbb = kb.astype(BF16)
    for n in range(DSA_KV_HEADS):
        khm_o[n] = kbb[:, n * DSA_HD:(n + 1) * DSA_HD]
    ikb_o[...] = ikw[:, :IDX_DIM].astype(BF16)
    v = v_ref[...]
    for j in range(tm // DSA_TK):
        vt_o[j] = v[j * DSA_TK:(j + 1) * DSA_TK, :].T.astype(BF16)
    iwt_o[...] = ikw.T[IDX_DIM:IDX_DIM + 8, :]


def _even_prep(pe, tabs, tm, n_prompt_blocks, tab_blocks):
    n = pe.shape[0]
    tq, tk, td, ti = tabs

    def tix(i):
        return (0, jnp.where(i < n_prompt_blocks, i % tab_blocks, tab_blocks), 0)

    def col(wd, j):
        return pl.BlockSpec((tm, wd), lambda i: (i, j))

    def tab(wd):
        return pl.BlockSpec((3, tm, wd), tix)

    def out(wd):
        return pl.BlockSpec((tm, wd), lambda i: (i, 0))

    group = DSA_HEADS // DSA_KV_HEADS
    return pl.pallas_call(
        _even_prep_kernel,
        grid=(n // tm,),
        in_specs=[col(256, 0), col(256, 1), col(512, 3), col(128, 16), col(256, 9), col(128, 20),
                  col(128, 17), tab(256), tab(256), tab(512), tab(128)],
        out_specs=[out(256), out(256), out(512), out(128), out(256), out(128),
                   pl.BlockSpec((DSA_KV_HEADS, group * tm, DSA_HD), lambda i: (0, i, 0)),
                   pl.BlockSpec((IDX_HEADS * tm, IDX_DIM), lambda i: (i, 0)),
                   pl.BlockSpec((DSA_KV_HEADS, tm, DSA_HD), lambda i: (0, i, 0)),
                   pl.BlockSpec((tm, IDX_DIM), lambda i: (i, 0)),
                   pl.BlockSpec((tm // DSA_TK, LANE, DSA_TK), lambda i: (i, 0, 0)),
                   pl.BlockSpec((8, tm), lambda i: (0, i))],
        out_shape=[jax.ShapeDtypeStruct((n, 256), F32), jax.ShapeDtypeStruct((n, 256), F32),
                   jax.ShapeDtypeStruct((n, 512), BF16), jax.ShapeDtypeStruct((n, 128), F32),
                   jax.ShapeDtypeStruct((n, 256), BF16), jax.ShapeDtypeStruct((n, 128), F32),
                   jax.ShapeDtypeStruct((DSA_KV_HEADS, group * n, DSA_HD), BF16),
                   jax.ShapeDtypeStruct((IDX_HEADS * n, IDX_DIM), BF16),
                   jax.ShapeDtypeStruct((DSA_KV_HEADS, n, DSA_HD), BF16),
                   jax.ShapeDtypeStruct((n, IDX_DIM), BF16),
                   jax.ShapeDtypeStruct((n // DSA_TK, LANE, DSA_TK), BF16),
                   jax.ShapeDtypeStruct((8, n), F32)],
        compiler_params=_cparams(1),
        name="even_rope",
    )(pe, pe, pe, pe, pe, pe, pe, tq, tk, td, ti)


def _ret_kernel(gch_ref, q_ref, k_ref, v_ref, g_ref, dm_ref, qd_ref, kd_ref, gng_ref, gnb_ref,
                s0_ref, *rest):
    o_ref, s_ref, s_sc = rest[-3], rest[-2], rest[-1]
    c = pl.program_id(1)

    @pl.when(c == 0)
    def _():
        s_sc[...] = s0_ref[0]

    q = q_ref[...]
    k = k_ref[...]
    qx = (q * qd_ref[...]).astype(BF16)
    kw = (k * kd_ref[...]).astype(BF16)
    qb = q.astype(BF16)
    kb = k.astype(BF16)
    vb = v_ref[...].astype(BF16)
    gate = g_ref[...]
    for h in range(RET_HEADS):
        ks = slice(h * RET_DK, (h + 1) * RET_DK)
        vs = slice(h * RET_DV, (h + 1) * RET_DV)
        att = _dot_nt(qb[:, ks], kb[:, ks]) * dm_ref[h]
        s_old = s_sc[h]
        y = _dot(att.astype(BF16), vb[:, vs]) + _dot(qx[:, ks], s_old.astype(BF16))
        s_sc[h] = s_old * gch_ref[h] + _dot_tn(kw[:, ks], vb[:, vs])
        yn = _layer_norm(y, gng_ref[:, vs], gnb_ref[:, vs])
        o_ref[:, vs] = _silu(gate[:, vs]) * yn

    @pl.when(c == pl.num_programs(1) - 1)
    def _():
        s_ref[0] = s_sc[...]


def _retention(qa, ka, pe, s0, gn_g, gn_b, n_seq, t, row0, prev_out):
    n = qa.shape[0]
    lc = min(RET_LC, t)
    nc = t // lc
    blk0 = row0 // lc
    log_g = jnp.log(1.0 - 2.0 ** (-5.0 - jnp.arange(RET_HEADS, dtype=F32)))
    pos = jnp.arange(lc, dtype=F32)
    diff = pos[:, None] - pos[None, :]
    dmask = jnp.where(diff >= 0, jnp.exp(jnp.maximum(diff, 0.0)[None] * log_g[:, None, None]), 0.0)
    w_end = jnp.exp((lc - 1 - pos)[:, None] * log_g[None, :])
    xi = jnp.exp((pos + 1.0)[:, None] * log_g[None, :])
    kdec = jnp.repeat(w_end, RET_DK, axis=1)
    qdec = jnp.repeat(xi, RET_DK, axis=1)
    gch = jnp.exp(lc * log_g)

    def rows(wd, j):
        return pl.BlockSpec((lc, wd), lambda b, c: (blk0 + b * nc + c, j))

    def const(shape):
        nd = len(shape)
        return pl.BlockSpec(shape, lambda b, c: (0,) * nd)

    in_specs = [pl.BlockSpec(memory_space=pltpu.SMEM),
                rows(256, 0), rows(256, 0), rows(512, 1), rows(512, 2),
                const((RET_HEADS, lc, lc)), const((lc, 256)), const((lc, 256)),
                const((1, 512)), const((1, 512)),
                pl.BlockSpec((1, RET_HEADS, RET_DK, RET_DV), lambda b, c: (b, 0, 0, 0))]
    args = [gch, qa, ka, pe, pe, dmask, qdec, kdec,
            gn_g.reshape(1, 512), gn_b.reshape(1, 512), s0]
    aliases = {}
    if prev_out is not None:
        in_specs.append(pl.BlockSpec(memory_space=pl.ANY))
        args.append(prev_out)
        aliases = {len(args) - 1: 0}
    return pl.pallas_call(
        _ret_kernel,
        grid=(n_seq, nc),
        in_specs=in_specs,
        out_specs=[pl.BlockSpec((lc, 512), lambda b, c: (blk0 + b * nc + c, 0)),
                   pl.BlockSpec((1, RET_HEADS, RET_DK, RET_DV), lambda b, c: (b, 0, 0, 0))],
        out_shape=[jax.ShapeDtypeStruct((n, 512), F32),
                   jax.ShapeDtypeStruct((n_seq, RET_HEADS, RET_DK, RET_DV), F32)],
        scratch_shapes=[pltpu.VMEM((RET_HEADS, RET_DK, RET_DV), F32)],
        input_output_aliases=aliases,
        compiler_params=_cparams(2),
        name="retention",
    )(*args)


def _col_reduce(x, op):
    r, c = x.shape
    return op(op(x.reshape(r // 8, 8, c), axis=0), axis=0, keepdims=True)


def _dsa_kernel(nkb_ref, q_ref, iq_ref, iwt_ref, qlim_ref, k_ref, vt_ref, ik_ref, *rest,
                topk, tq_out):
    o_ref, key_sc, m_sc, l_sc, acc_sc, lga_sc, lgb_sc = rest[-7:]
    nkb = nkb_ref[pl.program_id(1)]
    tq = qlim_ref.shape[1]
    tk = key_sc.shape[1]
    group = DSA_HEADS // DSA_KV_HEADS
    qlim = qlim_ref[...]
    iwt = iwt_ref[...]
    iqs = iq_ref[...]
    idx_scale = (IDX_HEADS * IDX_DIM) ** -0.5
    krow = lax.broadcasted_iota(I32, (tk, tq), 0)

    def score_body(kb, carry):
        off = pl.multiple_of(kb * tk, tk)
        s_all = _dot_nt(ik_ref[pl.ds(off, tk), :], iqs)
        s = jnp.zeros((tk, tq), F32)
        for h in range(IDX_HEADS):
            s = s + iwt[h:h + 1, :] * jnp.maximum(s_all[:, h * tq:(h + 1) * tq], 0.0)
        s = s * idx_scale
        s = jnp.where(s == 0.0, 0.0, s)
        bits = pltpu.bitcast(s, I32)
        key = jnp.where(bits >= 0, bits, bits ^ jnp.int32(0x7FFFFFFF))
        adm = (off + krow) < qlim
        key_sc[kb] = jnp.where(adm, key, jnp.int32(INT_MIN))
        return carry

    lax.fori_loop(0, nkb, score_body, 0)

    def count(pred):
        def body(kb, acc):
            hit = jnp.where(pred(key_sc[kb]), 1.0, 0.0)
            return acc + jnp.sum(hit.reshape(tk // 64, 64, tq), axis=0)
        acc = lax.fori_loop(0, nkb, body, jnp.zeros((64, tq), F32))
        return jnp.sum(acc, axis=0, keepdims=True)

    def bit_body(it, ans):
        cand = ans + (jnp.int32(1) << (31 - it))
        return jnp.where(count(lambda k: k >= cand) >= topk, cand, ans)

    t = lax.fori_loop(0, 32, bit_body, jnp.full((1, tq), INT_MIN, I32))
    need = topk - count(lambda k: k > t)

    m_sc[...] = jnp.full(m_sc.shape, 0.1 * NEG_BIG, F32)
    l_sc[...] = jnp.zeros(l_sc.shape, F32)
    acc_sc[...] = jnp.zeros(acc_sc.shape, F32)
    r_i = lax.broadcasted_iota(I32, (tk, tk), 0)
    c_i = lax.broadcasted_iota(I32, (tk, tk), 1)
    lower = (c_i < r_i).astype(BF16)

    def logits_stage(kb, n_eq, dst):
        off = pl.multiple_of(kb * tk, tk)
        key = key_sc[kb]
        adm = (off + krow) < qlim
        eq = jnp.logical_and(key == t, adm)
        eqf = jnp.where(eq, 1.0, 0.0)
        pref = _dot(lower, eqf.astype(BF16))
        sel = jnp.logical_and(adm, jnp.logical_or(
            key > t, jnp.logical_and(eq, (n_eq + pref) < need)))
        bias = jnp.where(sel, 0.0, NEG_BIG)
        for n in range(DSA_KV_HEADS):
            lg_all = _dot_nt(k_ref[n, pl.ds(off, tk), :], q_ref[n])
            for g in range(group):
                ls = slice(g * tq, (g + 1) * tq)
                dst[n, :, ls] = lg_all[:, ls] + bias
        return n_eq + _col_reduce(eqf, jnp.sum)

    def softmax_stage(kb, src):
        vt = vt_ref[kb]
        for n in range(DSA_KV_HEADS):
            ps, alphas = [], []
            for g in range(group):
                ls = slice(g * tq, (g + 1) * tq)
                lg = src[n, :, ls]
                m_old = m_sc[n, :, ls]
                m_new = jnp.maximum(m_old, _col_reduce(lg, jnp.max))
                p = jnp.exp2(lg - m_new)
                alpha = jnp.exp2(m_old - m_new)
                l_sc[n, :, ls] = alpha * l_sc[n, :, ls] + _col_reduce(p, jnp.sum)
                m_sc[n, :, ls] = m_new
                ps.append(p.astype(BF16))
                alphas.append(alpha)
            p_all = jnp.concatenate(ps, axis=1)
            alpha_all = jnp.concatenate(alphas, axis=1)
            pv = _dot(vt[n * DSA_HD:(n + 1) * DSA_HD, :], p_all)
            acc_sc[n] = alpha_all * acc_sc[n] + pv

    n_pair = (nkb + 1) // 2
    last = 2 * n_pair - 1

    @pl.when(last >= nkb)
    def _():
        key_sc[last] = jnp.full((tk, tq), INT_MIN, I32)

    def pair_body(j, n_eq):
        kb0 = 2 * j
        softmax_stage(kb0, lga_sc)
        n_eq = logits_stage(kb0 + 1, n_eq, lgb_sc)
        softmax_stage(kb0 + 1, lgb_sc)
        return logits_stage(jnp.minimum(kb0 + 2, last), n_eq, lga_sc)

    lax.fori_loop(0, n_pair, pair_body, logits_stage(0, jnp.zeros((1, tq), F32), lga_sc))
    pieces = []
    for n in range(DSA_KV_HEADS):
        o_n = acc_sc[n] / l_sc[n]
        for g in range(group):
            pieces.append(o_n[:, g * tq:(g + 1) * tq])
    o_ref[...] = jnp.concatenate(pieces, axis=0).T[:tq_out, :]


def _dsa(q_st, iq_st, iw_t, qlim, nkb, k_hm, v_t, ik_bf, n_seq, nq, s_len, tq_out, row0, n_out,
         topk, prev_out):
    tq = DSA_TQ
    group = DSA_HEADS // DSA_KV_HEADS
    blk0 = row0 // tq_out
    in_specs = [pl.BlockSpec((DSA_KV_HEADS, group * tq, DSA_HD), lambda b, i, s: (0, b * nq + i, 0)),
                pl.BlockSpec((IDX_HEADS * tq, IDX_DIM), lambda b, i, s: (b * nq + i, 0)),
                pl.BlockSpec((8, tq), lambda b, i, s: (0, b * nq + i)),
                pl.BlockSpec((1, tq), lambda b, i, s: (0, i)),
                pl.BlockSpec((DSA_KV_HEADS, s_len, DSA_HD), lambda b, i, s: (0, b, 0)),
                pl.BlockSpec((s_len // DSA_TK, LANE, DSA_TK), lambda b, i, s: (b, 0, 0)),
                pl.BlockSpec((s_len, IDX_DIM), lambda b, i, s: (b, 0))]
    args = [nkb, q_st, iq_st, iw_t, qlim, k_hm, v_t, ik_bf]
    aliases = {}
    if prev_out is not None:
        in_specs.append(pl.BlockSpec(memory_space=pl.ANY))
        args.append(prev_out)
        aliases = {len(args) - 1: 0}
    grid_spec = pltpu.PrefetchScalarGridSpec(
        num_scalar_prefetch=1,
        grid=(n_seq, nq),
        in_specs=in_specs,
        out_specs=pl.BlockSpec((tq_out, 512), lambda b, i, s: (blk0 + b * nq + i, 0)),
        scratch_shapes=[pltpu.VMEM((s_len // DSA_TK, DSA_TK, tq), I32),
                        pltpu.VMEM((DSA_KV_HEADS, 1, group * tq), F32),
                        pltpu.VMEM((DSA_KV_HEADS, 1, group * tq), F32),
                        pltpu.VMEM((DSA_KV_HEADS, DSA_HD, group * tq), F32),
                        pltpu.VMEM((DSA_KV_HEADS, DSA_TK, group * tq), F32),
                        pltpu.VMEM((DSA_KV_HEADS, DSA_TK, group * tq), F32)])
    return pl.pallas_call(
        functools.partial(_dsa_kernel, topk=topk, tq_out=tq_out),
        grid_spec=grid_spec,
        out_shape=jax.ShapeDtypeStruct((n_out, 512), F32),
        input_output_aliases=aliases,
        compiler_params=_cparams(2),
        name="dsa_attention",
    )(*args)


def _outproj_ln_kernel(ya_ref, yb_ref, w_ref, h_ref, g_ref, b_ref, o_ref):
    half = ya_ref.shape[1]
    y = (_dot(ya_ref[...].astype(BF16), w_ref[:half, :])
         + _dot(yb_ref[...].astype(BF16), w_ref[half:, :]))
    o_ref[...] = _layer_norm(DN_ALPHA * h_ref[...] + y, g_ref[...], b_ref[...])


def _outproj_ln(ya, yb, w, h, g, b, tm):
    n = h.shape[0]
    return pl.pallas_call(
        _outproj_ln_kernel,
        grid=(n // tm,),
        in_specs=[pl.BlockSpec((tm, 512), lambda i: (i, 0)),
                  pl.BlockSpec((tm, 512), lambda i: (i, 0)),
                  pl.BlockSpec((D_MODEL, D_MODEL), lambda i: (0, 0)),
                  pl.BlockSpec((tm, D_MODEL), lambda i: (i, 0)),
                  pl.BlockSpec((1, D_MODEL), lambda i: (0, 0)),
                  pl.BlockSpec((1, D_MODEL), lambda i: (0, 0))],
        out_specs=pl.BlockSpec((tm, D_MODEL), lambda i: (i, 0)),
        out_shape=jax.ShapeDtypeStruct((n, D_MODEL), F32),
        compiler_params=_cparams(1),
        name="out_proj_ln",
    )(ya, yb, w, h, g.reshape(1, -1), b.reshape(1, -1))


def _router_kernel(x_ref, w_ref, b_ref, idx_o, gate_o, rank_o, cnt_o, cnt_sc):
    i = pl.program_id(0)

    @pl.when(i == 0)
    def _():
        cnt_sc[...] = jnp.zeros(cnt_sc.shape, F32)

    tm = x_ref.shape[0]
    lane = lax.broadcasted_iota(I32, (tm, LANE), 1)
    logits = _dot_f32(x_ref[...], w_ref[...]) + b_ref[...]
    logits = jnp.where(lane < N_EXPERTS, logits, -jnp.inf)
    vals, idxs = [], []
    onehot = jnp.zeros((tm, LANE), F32)
    for _ in range(TOP_K):
        m = jnp.max(logits, axis=1, keepdims=True)
        ix = jnp.min(jnp.where(logits == m, lane, LANE), axis=1, keepdims=True)
        hit = lane == ix
        onehot = jnp.where(hit, 1.0, onehot)
        logits = jnp.where(hit, -jnp.inf, logits)
        vals.append(m)
        idxs.append(ix)
    es = [jnp.exp(v - vals[0]) for v in vals]
    den = es[0] + es[1] + es[2] + es[3]
    r_i = lax.broadcasted_iota(I32, (tm, tm), 0)
    c_i = lax.broadcasted_iota(I32, (tm, tm), 1)
    lower = (c_i < r_i).astype(BF16)
    rank_dense = _dot(lower, onehot.astype(BF16)) + cnt_sc[...]
    idx_out = jnp.zeros((tm, LANE), I32)
    gate_out = jnp.zeros((tm, LANE), F32)
    rank_out = jnp.zeros((tm, LANE), F32)
    for k in range(TOP_K):
        rk = jnp.sum(jnp.where(lane == idxs[k], rank_dense, 0.0), axis=1, keepdims=True)
        idx_out = jnp.where(lane == k, idxs[k], idx_out)
        gate_out = jnp.where(lane == k, es[k] / den, gate_out)
        rank_out = jnp.where(lane == k, rk, rank_out)
    idx_o[...] = idx_out
    gate_o[...] = gate_out
    rank_o[...] = rank_out.astype(I32)
    cnt = cnt_sc[...] + jnp.sum(onehot, axis=0, keepdims=True)
    cnt_sc[...] = cnt
    cnt_o[...] = cnt


def _router(x, w_r, b_r, tm):
    n = x.shape[0]
    w = jnp.zeros((D_MODEL, LANE), F32).at[:, :N_EXPERTS].set(w_r)
    b = jnp.zeros((1, LANE), F32).at[0, :N_EXPERTS].set(b_r)
    row = pl.BlockSpec((tm, LANE), lambda i: (i, 0))
    return pl.pallas_call(
        _router_kernel,
        grid=(n // tm,),
        in_specs=[pl.BlockSpec((tm, D_MODEL), lambda i: (i, 0)),
                  pl.BlockSpec((D_MODEL, LANE), lambda i: (0, 0)),
                  pl.BlockSpec((1, LANE), lambda i: (0, 0))],
        out_specs=[row, row, row, pl.BlockSpec((1, LANE), lambda i: (0, 0))],
        out_shape=[jax.ShapeDtypeStruct((n, LANE), I32), jax.ShapeDtypeStruct((n, LANE), F32),
                   jax.ShapeDtypeStruct((n, LANE), I32), jax.ShapeDtypeStruct((1, LANE), F32)],
        scratch_shapes=[pltpu.VMEM((1, LANE), F32)],
        compiler_params=_cparams(1),
        name="moe_router",
    )(x, w, b)


def _moe_kernel(be_ref, nu_ref, x_ref, wgu_ref, bgu_ref, wdn_ref, bdn_ref, o_ref, wgu_sc, wdn_sc):
    i = pl.program_id(0)

    @pl.when(jnp.logical_or(i == 0, be_ref[i] != be_ref[jnp.maximum(i - 1, 0)]))
    def _():
        wgu_sc[...] = wgu_ref[0, 0].astype(BF16)
        wdn_sc[...] = wdn_ref[0, 0].astype(BF16)

    @pl.when(i < nu_ref[0])
    def _():
        h = _dot(x_ref[...].astype(BF16), wgu_sc[...]) + bgu_ref[0, 0]
        g = jnp.minimum(h[:, :D_FF], SWIGLU_LIMIT)
        up = jnp.clip(h[:, D_FF:], -SWIGLU_LIMIT, SWIGLU_LIMIT)
        a = (up + 1.0) * g * jax.nn.sigmoid(SWIGLU_ALPHA * g)
        o_ref[...] = _dot(a.astype(BF16), wdn_sc[...]) + bdn_ref[0, 0]

    @pl.when(i >= nu_ref[0])
    def _():
        o_ref[...] = jnp.zeros(o_ref.shape, F32)


def _moe_experts(xs, blk_e, n_used, layer, w_gu, b_gu, w_dn, b_dn):
    n_rows = xs.shape[0]
    tm = MOE_TM
    depth = w_gu.shape[0]
    grid_spec = pltpu.PrefetchScalarGridSpec(
        num_scalar_prefetch=2,
        grid=(n_rows // tm,),
        in_specs=[pl.BlockSpec((tm, D_MODEL), lambda i, be, nu: (i, 0)),
                  pl.BlockSpec((1, 1, D_MODEL, 2 * D_FF), lambda i, be, nu: (layer, be[i], 0, 0)),
                  pl.BlockSpec((1, 1, 1, 2 * D_FF), lambda i, be, nu: (layer, be[i], 0, 0)),
                  pl.BlockSpec((1, 1, D_FF, D_MODEL), lambda i, be, nu: (layer, be[i], 0, 0)),
                  pl.BlockSpec((1, 1, 1, D_MODEL), lambda i, be, nu: (layer, be[i], 0, 0))],
        out_specs=pl.BlockSpec((tm, D_MODEL), lambda i, be, nu: (i, 0)),
        scratch_shapes=[pltpu.VMEM((D_MODEL, 2 * D_FF), BF16), pltpu.VMEM((D_FF, D_MODEL), BF16)])
    return pl.pallas_call(
        _moe_kernel,
        grid_spec=grid_spec,
        out_shape=jax.ShapeDtypeStruct((n_rows, D_MODEL), F32),
        compiler_params=_cparams(1),
        name="moe_experts",
    )(blk_e, n_used, xs, w_gu, b_gu.reshape(depth, N_EXPERTS, 1, -1), w_dn,
      b_dn.reshape(depth, N_EXPERTS, 1, -1))


def _combine_ln_kernel(h_ref, y0_ref, y1_ref, y2_ref, y3_ref, gate_ref, g_ref, b_ref, o_ref):
    gate = gate_ref[...]
    y = (gate[:, 0:1] * y0_ref[...] + gate[:, 1:2] * y1_ref[...]
         + gate[:, 2:3] * y2_ref[...] + gate[:, 3:4] * y3_ref[...])
    o_ref[...] = _layer_norm(DN_ALPHA * h_ref[...] + y, g_ref[...], b_ref[...])


def _combine_ln(h, ys, gate, g, b, tm):
    n = h.shape[0]
    row = pl.BlockSpec((tm, D_MODEL), lambda i: (i, 0))
    vec = pl.BlockSpec((1, D_MODEL), lambda i: (0, 0))
    return pl.pallas_call(
        _combine_ln_kernel,
        grid=(n // tm,),
        in_specs=[row, row, row, row, row, pl.BlockSpec((tm, LANE), lambda i: (i, 0)), vec, vec],
        out_specs=row,
        out_shape=jax.ShapeDtypeStruct((n, D_MODEL), F32),
        compiler_params=_cparams(1),
        name="moe_combine_ln",
    )(h, ys[0], ys[1], ys[2], ys[3], gate, g.reshape(1, -1), b.reshape(1, -1))


def _rows(x, idx):
    return x.at[idx].get(mode="promise_in_bounds")


def _moe_layer(h, layer, w_r, b_r, w_gu, b_gu, w_dn, b_dn, ln_g, ln_b, tm):
    n = h.shape[0]
    n_pair = n * TOP_K
    idx, gate, rank, cnt = _router(h, w_r, b_r, tm)
    top_i = idx[:, :TOP_K]
    counts = cnt[0, :N_EXPERTS].astype(I32)
    padded = (counts + MOE_TM - 1) // MOE_TM * MOE_TM
    pad_end = jnp.cumsum(padded)
    start = pad_end - padded
    first = jnp.cumsum(counts) - counts
    dest = _rows(start, top_i) + rank[:, :TOP_K]
    n_blk = -(-n_pair // MOE_TM) + N_EXPERTS
    n_used = (pad_end[-1] // MOE_TM).astype(I32)
    blk_row = jnp.minimum(jnp.arange(n_blk, dtype=I32), n_used - 1) * MOE_TM
    blk_e = jnp.sum((pad_end[None, :] <= blk_row[:, None]).astype(I32), axis=1)
    blk_e = jnp.minimum(blk_e, N_EXPERTS - 1)
    order = jnp.argsort(top_i.reshape(-1), stable=True).astype(I32)
    row_in_e = (jnp.arange(n_blk, dtype=I32)[:, None] * MOE_TM - _rows(start, blk_e)[:, None]
                + jnp.arange(MOE_TM, dtype=I32)[None, :])
    pair = jnp.clip(_rows(first, blk_e)[:, None] + row_in_e, 0, n_pair - 1).reshape(-1)
    src = _rows(order, pair) // TOP_K
    xs = _rows(h, src)
    ybuf = _moe_experts(xs, blk_e, n_used.reshape(1), layer, w_gu, b_gu, w_dn, b_dn)
    ys = [_rows(ybuf, dest[:, k]) for k in range(TOP_K)]
    return _combine_ln(h, ys, gate, ln_g, ln_b, tm)


def _band_kernel(q_ref, *rest, nkb):
    k_refs = rest[:nkb]
    v_refs = rest[nkb:2 * nkb]
    bias_ref = rest[2 * nkb]
    o_ref = rest[-1]
    c = pl.program_id(1)
    tq = q_ref.shape[0]
    tkb = k_refs[0].shape[0]
    scale = BAND_HD ** -0.5
    qb = q_ref[...].astype(BF16)
    kbs = [r[...].astype(BF16) for r in k_refs]
    vbs = [r[...].astype(BF16) for r in v_refs]
    valid = [(c + j - (nkb - 1)) >= 0 for j in range(nkb)]
    for h in range(BAND_HEADS):
        hs = slice(h * BAND_HD, (h + 1) * BAND_HD)
        lgs = []
        for j in range(nkb):
            lg = _dot_nt(qb[:, hs], kbs[j][:, hs]) * scale + bias_ref[h, :, j * tkb:(j + 1) * tkb]
            lgs.append(jnp.where(valid[j], lg, NEG_BIG))
        m = lgs[0].max(axis=1, keepdims=True)
        for j in range(1, nkb):
            m = jnp.maximum(m, lgs[j].max(axis=1, keepdims=True))
        ps = [jnp.exp(lg - m) for lg in lgs]
        den = ps[0].sum(axis=1, keepdims=True)
        for j in range(1, nkb):
            den = den + ps[j].sum(axis=1, keepdims=True)
        inv = 1.0 / den
        acc = _dot((ps[0] * inv).astype(BF16), vbs[0][:, hs])
        for j in range(1, nkb):
            acc = acc + _dot((ps[j] * inv).astype(BF16), vbs[j][:, hs])
        o_ref[:, hs] = acc


def _band(q_arr, k_arr, v_arr, cols, bias, n_seq, t, tq, tkb, nkb, q_row0, kv_blocks_per_seq,
          n_out, prev_out):
    nq = t // tq
    qblk0 = q_row0 // tq
    qcol, kcol, vcol = cols

    def kv_spec(j, col):
        def ix(b, c):
            return (b * kv_blocks_per_seq + jnp.maximum(c + j - (nkb - 1), 0), col)
        return pl.BlockSpec((tkb, 512), ix)

    in_specs = ([pl.BlockSpec((tq, 512), lambda b, c: (qblk0 + b * nq + c, qcol))]
                + [kv_spec(j, kcol) for j in range(nkb)]
                + [kv_spec(j, vcol) for j in range(nkb)]
                + [pl.BlockSpec(bias.shape, lambda b, c: (0, 0, 0))])
    args = [q_arr] + [k_arr] * nkb + [v_arr] * nkb + [bias]
    aliases = {}
    if prev_out is not None:
        in_specs.append(pl.BlockSpec(memory_space=pl.ANY))
        args.append(prev_out)
        aliases = {len(args) - 1: 0}
    return pl.pallas_call(
        functools.partial(_band_kernel, nkb=nkb),
        grid=(n_seq, nq),
        in_specs=in_specs,
        out_specs=pl.BlockSpec((tq, 512), lambda b, c: (qblk0 + b * nq + c, 0)),
        out_shape=jax.ShapeDtypeStruct((n_out, 512), F32),
        input_output_aliases=aliases,
        compiler_params=_cparams(2),
        name="band_attention",
    )(*args)


def _band_bias(rel_bias, tq, n_keys, key0):
    n_off = tq + n_keys - 1
    d_max = tq - 1 - key0
    rel = np.clip(d_max - np.arange(n_off), -REL_CLIP, REL_CLIP) + REL_CLIP
    vals = jnp.concatenate([rel_bias[:, rel], jnp.zeros((rel_bias.shape[0], 1), F32)], axis=1)
    rot = jnp.tile(vals, (1, tq))[:, :tq * n_off].reshape(-1, tq, n_off)
    toep = rot[:, :, tq - 1:tq - 1 + n_keys]
    qp = np.arange(tq)[:, None]
    kp = key0 + np.arange(n_keys)[None, :]
    cs = (qp // CHUNK) * CHUNK
    band = np.logical_and(kp >= cs - BAND_PAST, kp < cs + CHUNK)
    return jnp.where(jnp.asarray(band)[None], toep, NEG_BIG).astype(F32)


def _ssd_kernel(dsk_ref, z_ref, xbc_ref, dt_ref, cw_ref, cb_ref, dtb_ref, alog_ref, ng_ref,
                h0_ref, c0_ref, *rest):
    o_ref, h_ref, cl_ref, h_sc, xe_sc, y_sc = rest[-6:]
    c = pl.program_id(1)
    lc = z_ref.shape[0]

    @pl.when(c == 0)
    def _():
        h_sc[...] = h0_ref[0]
        xe_sc[0:8, :] = jnp.zeros((8, xe_sc.shape[1]), F32)
        xe_sc[8 - (SSD_CONV - 1):8, :] = c0_ref[0]

    xe_sc[8:8 + lc, :] = xbc_ref[...]
    conv = cb_ref[...] + cw_ref[SSD_CONV - 1:SSD_CONV, :] * xe_sc[8:8 + lc, :]
    for s in range(1, SSD_CONV):
        conv = conv + cw_ref[SSD_CONV - 1 - s:SSD_CONV - s, :] * xe_sc[8 - s:8 - s + lc, :]
    u = _silu(conv)
    gs = SSD_GROUPS * SSD_STATE
    xs = u[:, :SSD_INNER]
    bm = u[:, SSD_INNER:SSD_INNER + gs].astype(BF16)
    cm = u[:, SSD_INNER + gs:].astype(BF16)
    dx = dt_ref[...] + dtb_ref[...]
    dtv = jnp.maximum(dx, 0.0) + jnp.log1p(jnp.exp(-jnp.abs(dx)))
    a = dtv * (-jnp.exp(alog_ref[...]))
    r_i = lax.broadcasted_iota(I32, (lc, lc), 0)
    c_i = lax.broadcasted_iota(I32, (lc, lc), 1)
    causal = c_i <= r_i
    acum = _dot_f32(causal.astype(F32), a)
    acum_t = acum.T
    hpg = SSD_HEADS // SSD_GROUPS
    for g in range(SSD_GROUPS):
        ss = slice(g * SSD_STATE, (g + 1) * SSD_STATE)
        cb = _dot_nt(cm[:, ss], bm[:, ss])
        for jj in range(hpg):
            j = g * hpg + jj
            ps = slice(j * SSD_HD, (j + 1) * SSD_HD)
            col = acum[:, j:j + 1]
            row = acum_t[j:j + 1, :]
            lmat = jnp.exp(jnp.where(causal, col - row, -jnp.inf))
            x_j = xs[:, ps]
            xdt = x_j * dtv[:, j:j + 1]
            h_old = h_sc[j]
            y = _dot((cb * lmat).astype(BF16), xdt.astype(BF16))
            y = y + _dot_nt(cm[:, ss], h_old.astype(BF16)) * jnp.exp(col)
            y = y + dsk_ref[j] * x_j
            last = acum[lc - 1:lc, j:j + 1]
            st = _dot_tn((xdt * jnp.exp(last - col)).astype(BF16), bm[:, ss])
            h_sc[j] = h_old * jnp.exp(last) + st
            y_sc[:, ps] = y
    yd = y_sc[...] * _silu(z_ref[...])
    gw = SSD_INNER // SSD_GROUPS
    for g in range(SSD_GROUPS):
        ws = slice(g * gw, (g + 1) * gw)
        yg = yd[:, ws]
        ms = jnp.mean(yg * yg, axis=-1, keepdims=True)
        o_ref[:, ws] = yg * lax.rsqrt(ms + LN_EPS) * ng_ref[:, ws]
    xe_sc[0:8, :] = xe_sc[lc:lc + 8, :]

    @pl.when(c == pl.num_programs(1) - 1)
    def _():
        h_ref[0] = h_sc[...]
        cl_ref[0] = xe_sc[8 - (SSD_CONV - 1):8, :]


def _ssd(po, h0, c0, conv_w, conv_b, dt_bias, a_log, d_skip, norm_g, n_seq, t, row0, prev_out):
    n = po.shape[0]
    lc = min(SSD_LC, t)
    nc = t // lc
    blk0 = row0 // lc
    cdim = conv_w.shape[1]

    def rows(wd, j):
        return pl.BlockSpec((lc, wd), lambda b, c: (blk0 + b * nc + c, j))

    def const(shape):
        nd = len(shape)
        return pl.BlockSpec(shape, lambda b, c: (0,) * nd)

    pad8 = lambda v: jnp.zeros((1, LANE), F32).at[0, :SSD_HEADS].set(v)
    in_specs = [pl.BlockSpec(memory_space=pltpu.SMEM),
                rows(512, 3), rows(cdim, 2), rows(LANE, 24),
                const((SSD_CONV, cdim)), const((1, cdim)), const((1, LANE)), const((1, LANE)),
                const((1, SSD_INNER)),
                pl.BlockSpec((1, SSD_HEADS, SSD_HD, SSD_STATE), lambda b, c: (b, 0, 0, 0)),
                pl.BlockSpec((1, SSD_CONV - 1, cdim), lambda b, c: (b, 0, 0))]
    args = [d_skip, po, po, po, conv_w, conv_b.reshape(1, -1), pad8(dt_bias), pad8(a_log),
            norm_g.reshape(1, -1), h0, c0]
    aliases = {}
    if prev_out is not None:
        in_specs.append(pl.BlockSpec(memory_space=pl.ANY))
        args.append(prev_out)
        aliases = {len(args) - 1: 0}
    return pl.pallas_call(
        _ssd_kernel,
        grid=(n_seq, nc),
        in_specs=in_specs,
        out_specs=[pl.BlockSpec((lc, 512), lambda b, c: (blk0 + b * nc + c, 0)),
                   pl.BlockSpec((1, SSD_HEADS, SSD_HD, SSD_STATE), lambda b, c: (b, 0, 0, 0)),
                   pl.BlockSpec((1, SSD_CONV - 1, cdim), lambda b, c: (b, 0, 0))],
        out_shape=[jax.ShapeDtypeStruct((n, 512), F32),
                   jax.ShapeDtypeStruct((n_seq, SSD_HEADS, SSD_HD, SSD_STATE), F32),
                   jax.ShapeDtypeStruct((n_seq, SSD_CONV - 1, cdim), F32)],
        scratch_shapes=[pltpu.VMEM((SSD_HEADS, SSD_HD, SSD_STATE), F32),
                        pltpu.VMEM((lc + 8, cdim), F32),
                        pltpu.VMEM((lc, 512), F32)],
        input_output_aliases=aliases,
        compiler_params=_cparams(2),
        name="ssd_scan",
    )(*args)


def _pad_cols(w, width):
    return jnp.concatenate([w, jnp.zeros((w.shape[0], width - w.shape[1]), w.dtype)], axis=1)


def kernel(x_prompt, x_sample, state_ret, cache_dsa_k, cache_dsa_v, cache_dsa_kidx, cache_band_k, cache_band_v, state_ssm, state_conv, e_w_in, e_w_out, e_gn_g, e_gn_b, o_w_in, o_w_out, o_rel_bias, o_conv_w, o_conv_b, o_dt_bias, o_a_log, o_d_skip, o_norm_g, ln1_g, ln1_b, ln2_g, ln2_b, router_w, router_b, exp_w_gu, exp_b_gu, exp_w_dn, exp_b_dn):
    bp, tp, _ = x_prompt.shape
    bs, ts, _ = x_sample.shape
    past = cache_dsa_k.shape[2]
    n_p, n_s = bp * tp, bs * ts
    n = n_p + n_s
    tm = math.gcd(512, math.gcd(n_p, n_s))
    assert tp % tm == 0 and tm % ts == 0 and ts == CHUNK

    h = jnp.concatenate([x_prompt.reshape(n_p, D_MODEL), x_sample.reshape(n_s, D_MODEL)], axis=0)

    pe = _proj(h, _pad_cols(e_w_in[0], EVEN_W).astype(BF16), tm)
    pos_p = jnp.arange(tp, dtype=I32)
    pos_s = past + jnp.arange(ts, dtype=I32)
    pos_tab = jnp.concatenate([pos_p, jnp.tile(pos_s, tm // ts)])
    tabs = (_rope_tables(pos_tab, RET_HEADS, RET_DK, RET_DK, RET_THETA),
            _rope_tables(pos_tab, RET_HEADS, RET_DK, RET_DK, RET_THETA, scale=RET_DK ** -0.5),
            _rope_tables(pos_tab, DSA_HEADS, DSA_HD, DSA_ROT, ROPE_THETA),
            _rope_tables(pos_tab, 1, IDX_DIM, DSA_ROT, ROPE_THETA, pad_to=LANE))
    (qa, ka, qb, kb, iq, ikw, q_st, iq_st, k_hm, ik_bf, v_t, iw_t) = _even_prep(
        pe, tabs, tm, n_p // tm, tp // tm)

    ya, ret_p = _retention(qa, ka, pe, jnp.zeros((bp, RET_HEADS, RET_DK, RET_DV), F32),
                           e_gn_g[0], e_gn_b[0], bp, tp, 0, None)
    ya, ret_s = _retention(qa, ka, pe, state_ret[0], e_gn_g[0], e_gn_b[0], bs, ts, n_p, ya)

    topk_p = min(DSA_TOPK_MAX, tp // 4)
    qlim_p = (((pos_p // CHUNK) + 1) * CHUNK).reshape(1, tp)
    nq_p = tp // DSA_TQ
    nkb_p = ((jnp.arange(nq_p, dtype=I32) + 1) * DSA_TQ + DSA_TK - 1) // DSA_TK
    yb = _dsa(q_st, iq_st, iw_t, qlim_p, nkb_p, k_hm, v_t, ik_bf, bp, nq_p, tp, DSA_TQ, 0, n,
              topk_p, None)

    s_len = past + ts
    s_pad = -(-s_len // (2 * DSA_TK)) * (2 * DSA_TK)
    topk_s = min(DSA_TOPK_MAX, s_len // 4)
    group = DSA_HEADS // DSA_KV_HEADS

    def cat_keys(cache, new, wd):
        zpad = jnp.zeros((bs, s_pad - s_len, wd), F32)
        return jnp.concatenate([cache, new.reshape(bs, ts, wd), zpad], axis=1)

    def pad_q(x):
        return jnp.concatenate([x, jnp.zeros((bs, DSA_TQ - ts) + x.shape[2:], x.dtype)], axis=1)

    ks = cat_keys(cache_dsa_k[0].reshape(bs, past, LANE), kb[n_p:], LANE)
    vs = cat_keys(cache_dsa_v[0].reshape(bs, past, LANE), pe[n_p:, 2176:2304], LANE)
    iks = cat_keys(cache_dsa_kidx[0], ikw[n_p:, :IDX_DIM], IDX_DIM)
    k_hm_s = ks.reshape(bs, s_pad, DSA_KV_HEADS, DSA_HD).transpose(2, 0, 1, 3).reshape(
        DSA_KV_HEADS, bs * s_pad, DSA_HD).astype(BF16)
    v_t_s = vs.reshape(bs, s_pad // DSA_TK, DSA_TK, LANE).transpose(0, 1, 3, 2).reshape(
        bs * (s_pad // DSA_TK), LANE, DSA_TK).astype(BF16)
    ik_s = iks.reshape(bs * s_pad, IDX_DIM).astype(BF16)
    q_s = pad_q(qb[n_p:].reshape(bs, ts, DSA_KV_HEADS, group, DSA_HD))
    q_st_s = q_s.transpose(2, 0, 3, 1, 4).reshape(DSA_KV_HEADS, bs * group * DSA_TQ, DSA_HD)
    iq_s = pad_q(iq[n_p:].reshape(bs, ts, IDX_HEADS, IDX_DIM))
    iq_st_s = iq_s.transpose(0, 2, 1, 3).reshape(bs * IDX_HEADS * DSA_TQ, IDX_DIM)
    iw_t_s = pad_q(ikw[n_p:, IDX_DIM:IDX_DIM + 8].reshape(bs, ts, 8)).reshape(bs * DSA_TQ, 8).T
    qlim_s = jnp.full((1, DSA_TQ), s_len, I32)
    nkb_s = jnp.full((1,), -(-s_len // DSA_TK), I32)
    yb = _dsa(q_st_s, iq_st_s, iw_t_s, qlim_s, nkb_s, k_hm_s, v_t_s, ik_s, bs, 1, s_pad, ts, n_p, n,
              topk_s, yb)

    h = _outproj_ln(ya, yb, e_w_out[0].astype(BF16), h, ln1_g[0], ln1_b[0], tm)
    h = _moe_layer(h, 0, router_w[0], router_b[0], exp_w_gu, exp_b_gu, exp_w_dn, exp_b_dn,
                   ln2_g[0], ln2_b[0], tm)

    po = _proj(h, _pad_cols(o_w_in[0], ODD_W).astype(BF16), tm)
    tq_p = min(BAND_TQ, tp)
    nkb_band = BAND_PAST // tq_p + 1
    bias_p = _band_bias(o_rel_bias[0], tq_p, nkb_band * tq_p, -(nkb_band - 1) * tq_p)
    yc = _band(po, po, po, (0, 1, 2), bias_p, bp, tp, tq_p, tq_p, nkb_band, 0, tp // tq_p, n, None)
    band_len = cache_band_k.shape[2]
    kc_new = po[n_p:, 512:1024].reshape(bs, ts, 512)
    vc_new = po[n_p:, 1024:1536].reshape(bs, ts, 512)
    kcat = jnp.concatenate([cache_band_k[0].reshape(bs, band_len, 512), kc_new], axis=1)
    vcat = jnp.concatenate([cache_band_v[0].reshape(bs, band_len, 512), vc_new], axis=1)
    wlen = band_len + ts
    bias_s = _band_bias(o_rel_bias[0], ts, wlen, -band_len)
    yc = _band(po, kcat.reshape(bs * wlen, 512), vcat.reshape(bs * wlen, 512), (0, 0, 0), bias_s,
               bs, ts, ts, wlen, 1, n_p, 1, n, yc)

    ssd_w = (o_conv_w[0], o_conv_b[0], o_dt_bias[0], o_a_log[0], o_d_skip[0], o_norm_g[0])
    cdim = o_conv_w.shape[2]
    yd, ssm_p, conv_p = _ssd(po, jnp.zeros((bp, SSD_HEADS, SSD_HD, SSD_STATE), F32),
                             jnp.zeros((bp, SSD_CONV - 1, cdim), F32), *ssd_w, bp, tp, 0, None)
    yd, ssm_s, conv_s = _ssd(po, state_ssm[0], state_conv[0], *ssd_w, bs, ts, n_p, yd)

    h = _outproj_ln(yc, yd, o_w_out[0].astype(BF16), h, ln1_g[1], ln1_b[1], tm)
    h = _moe_layer(h, 1, router_w[1], router_b[1], exp_w_gu, exp_b_gu, exp_w_dn, exp_b_dn,
                   ln2_g[1], ln2_b[1], tm)

    keep = min(BAND_PAST, tp)
    kd = DSA_KV_HEADS * DSA_HD
    kc_p = po[:n_p, 512:1024].reshape(bp, tp, BAND_HEADS, BAND_HD)[:, -keep:]
    vc_p = po[:n_p, 1024:1536].reshape(bp, tp, BAND_HEADS, BAND_HD)[:, -keep:]
    return (h[:n_p].reshape(bp, tp, D_MODEL), h[n_p:].reshape(bs, ts, D_MODEL),
            ret_p[None],
            kb[:n_p].reshape(1, bp, tp, DSA_KV_HEADS, DSA_HD),
            pe[:n_p, 2176:2176 + kd].reshape(1, bp, tp, DSA_KV_HEADS, DSA_HD),
            ikw[:n_p, :IDX_DIM].reshape(1, bp, tp, IDX_DIM),
            kc_p[None], vc_p[None], ssm_p[None], conv_p[None],
            ret_s[None],
            kb[n_p:].reshape(1, bs, ts, DSA_KV_HEADS, DSA_HD),
            pe[n_p:, 2176:2176 + kd].reshape(1, bs, ts, DSA_KV_HEADS, DSA_HD),
            ikw[n_p:, :IDX_DIM].reshape(1, bs, ts, IDX_DIM),
            kc_new.reshape(1, bs, ts, BAND_HEADS, BAND_HD), vc_new.reshape(1, bs, ts, BAND_HEADS, BAND_HD),
            ssm_s[None], conv_s[None])
```

```python
import functools
import math

import jax
import jax.numpy as jnp
import numpy as np
from jax import lax
from jax.experimental import pallas as pl
from jax.experimental.pallas import tpu as pltpu

F32 = jnp.float32
BF16 = jnp.bfloat16
I32 = jnp.int32
U32 = jnp.uint32

D_MODEL = 1024
CHUNK = 64
RET_HEADS, RET_DK, RET_DV, RET_THETA = 8, 32, 64, 10000.0
DSA_HEADS, DSA_KV_HEADS, DSA_HD = 8, 2, 64
DSA_ROT = DSA_HD // 4
IDX_HEADS, IDX_DIM = 4, 64
DSA_TOPK_MAX = 256
ROPE_THETA = 500000.0
BAND_HEADS, BAND_HD, BAND_PREV = 8, 64, 8
BAND_PAST = BAND_PREV * CHUNK
REL_CLIP = 256
SSD_HEADS, SSD_HD, SSD_GROUPS, SSD_STATE, SSD_CONV = 8, 64, 2, 128, 4
SSD_INNER = SSD_HEADS * SSD_HD
N_EXPERTS, TOP_K, D_FF = 32, 4, 1024
SWIGLU_LIMIT, SWIGLU_ALPHA = 7.0, 1.702
DEPTH = 2
DN_ALPHA = (2 * DEPTH) ** 0.25
LN_EPS = 1e-5

LANE = 128
VMEM_LIMIT = 56 * 1024 * 1024
INT_MIN = -(2 ** 31)
NEG_BIG = -1e30

EVEN_IN = 2628
EVEN_W = 2688
ODD_IN = 3080
ODD_W = 3200

MOE_TM = 512
RET_LC = 256
SSD_LC = 256
DSA_TQ = 128
DSA_TK = 256
BAND_TQ = 256


def _cparams(n_axes):
    return pltpu.CompilerParams(dimension_semantics=("arbitrary",) * n_axes,
                                vmem_limit_bytes=VMEM_LIMIT)


def _dot(a, b):
    return jnp.dot(a, b, preferred_element_type=F32)


def _dot_nt(a, b):
    return lax.dot_general(a, b, (((1,), (1,)), ((), ())), preferred_element_type=F32)


def _dot_tn(a, b):
    return lax.dot_general(a, b, (((0,), (0,)), ((), ())), preferred_element_type=F32)


def _dot_f32(a, b):
    return jnp.dot(a, b, preferred_element_type=F32, precision=lax.Precision.HIGHEST)


def _layer_norm(x, g, b):
    mu = jnp.mean(x, axis=-1, keepdims=True)
    xc = x - mu
    var = jnp.mean(xc * xc, axis=-1, keepdims=True)
    return xc * lax.rsqrt(var + LN_EPS) * g + b


def _silu(x):
    return x * jax.nn.sigmoid(x)


def _pack_pairs(x):
    c = x.shape[1] // 2
    hi = pltpu.bitcast(x[:, :c].astype(jnp.bfloat16).astype(F32), U32)
    lo = pltpu.bitcast(x[:, c:].astype(jnp.bfloat16).astype(F32), U32)
    return hi | (lo >> 16)


def _unpack_pairs(w):
    return (pltpu.bitcast(w & jnp.uint32(0xFFFF0000), F32), pltpu.bitcast(w << 16, F32))


def _proj_kernel(x_ref, w_ref, o_ref):
    o_ref[...] = _dot(x_ref[...].astype(BF16), w_ref[...])


def _proj(x, w, tm):
    n, k = x.shape
    wd = w.shape[1]
    return pl.pallas_call(
        _proj_kernel,
        grid=(n // tm,),
        in_specs=[pl.BlockSpec((tm, k), lambda i: (i, 0)),
                  pl.BlockSpec((k, wd), lambda i: (0, 0))],
        out_specs=pl.BlockSpec((tm, wd), lambda i: (i, 0)),
        out_shape=jax.ShapeDtypeStruct((n, wd), F32),
        compiler_params=_cparams(1),
        name="in_proj",
    )(x, w)


def _rope_tables(pos, n_heads, d, rot, theta, scale=1.0, pad_to=None):
    half = rot // 2
    inv = theta ** (-jnp.arange(half, dtype=F32) / half)
    ang = pos.astype(F32)[:, None] * inv[None, :]
    cos, sin = jnp.cos(ang), jnp.sin(ang)
    p = pos.shape[0]
    one = jnp.ones((p, d - rot), F32)
    zr = jnp.zeros((p, d - rot), F32)
    zh = jnp.zeros((p, half), F32)
    c = jnp.tile(jnp.concatenate([cos, cos, one], 1), (1, n_heads))
    a = jnp.tile(jnp.concatenate([-sin, zh, zr], 1), (1, n_heads))
    b = jnp.tile(jnp.concatenate([zh, sin, zr], 1), (1, n_heads))
    if pad_to is not None and pad_to > n_heads * d:
        extra = pad_to - n_heads * d
        c = jnp.concatenate([c, jnp.ones((p, extra), F32)], 1)
        a = jnp.concatenate([a, jnp.zeros((p, extra), F32)], 1)
        b = jnp.concatenate([b, jnp.zeros((p, extra), F32)], 1)
    return jnp.stack([c, a, b]) * scale


def _rope(x, tab_ref, half):
    w = x.shape[-1]
    return (x * tab_ref[0] + pltpu.roll(x, w - half, 1) * tab_ref[1]
            + pltpu.roll(x, half, 1) * tab_ref[2])


def _even_prep_kernel(qa_ref, ka_ref, qb_ref, kb_ref, iq_ref, ikw_ref, v_ref,
                      tq_ref, tk_ref, td_ref, ti_ref,
                      qa_o, ka_o, qb_o, kb_o, iq_o, ikw_o, qst_o, iqst_o, khm_o, ikb_o, vt_o, iwt_o):
    tm = qa_ref.shape[0]
    h = DSA_ROT // 2
    qa_o[...] = _rope(qa_ref[...], tq_ref, RET_DK // 2)
    ka_o[...] = _rope(ka_ref[...], tk_ref, RET_DK // 2)
    qb = (_rope(qb_ref[...], td_ref, h) * (DSA_HD ** -0.5 * math.log2(math.e))).astype(BF16)
    qb_o[...] = qb
    kb = kb_ref[...]
    kb = (kb * td_ref[0, :, :LANE] + pltpu.roll(kb, LANE - h, 1) * td_ref[1, :, :LANE]
          + pltpu.roll(kb, h, 1) * td_ref[2, :, :LANE])
    kb_o[...] = kb
    iq = iq_ref[...]
    w = iq.shape[-1]
    iq = (iq * td_ref[0, :, :w] + pltpu.roll(iq, w - h, 1) * td_ref[1, :, :w]
          + pltpu.roll(iq, h, 1) * td_ref[2, :, :w]).astype(BF16)
    iq_o[...] = iq
    ikw = _rope(ikw_ref[...], ti_ref, h)
    ikw_o[...] = ikw
    group = DSA_HEADS // DSA_KV_HEADS
    for jb in range(tm // DSA_TQ):
        rs = slice(jb * DSA_TQ, (jb + 1) * DSA_TQ)
        for hd in range(DSA_HEADS):
            n, g = divmod(hd, group)
            ro = (jb * group + g) * DSA_TQ
            qst_o[n, ro:ro + DSA_TQ, :] = qb[rs, hd * DSA_HD:(hd + 1) * DSA_HD]
        for hd in range(IDX_HEADS):
            ro = (jb * IDX_HEADS + hd) * DSA_TQ
            iqst_o[ro:ro + DSA_TQ, :] = iq[rs, hd * IDX_DIM:(hd + 1) * IDX_DIM]
    kbb = kb.astype(BF16)
    for n in range(DSA_KV_HEADS):
        khm_o[n] = kbb[:, n * DSA_HD:(n + 1) * DSA_HD]
    ikb_o[...] = ikw[:, :IDX_DIM].astype(BF16)
    v = v_ref[...]
    for j in range(tm // DSA_TK):
        vt_o[j] = v[j * DSA_TK:(j + 1) * DSA_TK, :].T.astype(BF16)
    iwt_o[...] = ikw.T[IDX_DIM:IDX_DIM + 8, :]


def _even_prep(pe, tabs, tm, n_prompt_blocks, tab_blocks):
    n = pe.shape[0]
    tq, tk, td, ti = tabs

    def tix(i):
        return (0, jnp.where(i < n_prompt_blocks, i % tab_blocks, tab_blocks), 0)

    def col(wd, j):
        return pl.BlockSpec((tm, wd), lambda i: (i, j))

    def tab(wd):
        return pl.BlockSpec((3, tm, wd), tix)

    def out(wd):
        return pl.BlockSpec((tm, wd), lambda i: (i, 0))

    group = DSA_HEADS // DSA_KV_HEADS
    return pl.pallas_call(
        _even_prep_kernel,
        grid=(n // tm,),
        in_specs=[col(256, 0), col(256, 1), col(512, 3), col(128, 16), col(256, 9), col(128, 20),
                  col(128, 17), tab(256), tab(256), tab(512), tab(128)],
        out_specs=[out(256), out(256), out(512), out(128), out(256), out(128),
                   pl.BlockSpec((DSA_KV_HEADS, group * tm, DSA_HD), lambda i: (0, i, 0)),
                   pl.BlockSpec((IDX_HEADS * tm, IDX_DIM), lambda i: (i, 0)),
                   pl.BlockSpec((DSA_KV_HEADS, tm, DSA_HD), lambda i: (0, i, 0)),
                   pl.BlockSpec((tm, IDX_DIM), lambda i: (i, 0)),
                   pl.BlockSpec((tm // DSA_TK, LANE, DSA_TK), lambda i: (i, 0, 0)),
                   pl.BlockSpec((8, tm), lambda i: (0, i))],
        out_shape=[jax.ShapeDtypeStruct((n, 256), F32), jax.ShapeDtypeStruct((n, 256), F32),
                   jax.ShapeDtypeStruct((n, 512), BF16), jax.ShapeDtypeStruct((n, 128), F32),
                   jax.ShapeDtypeStruct((n, 256), BF16), jax.ShapeDtypeStruct((n, 128), F32),
                   jax.ShapeDtypeStruct((DSA_KV_HEADS, group * n, DSA_HD), BF16),
                   jax.ShapeDtypeStruct((IDX_HEADS * n, IDX_DIM), BF16),
                   jax.ShapeDtypeStruct((DSA_KV_HEADS, n, DSA_HD), BF16),
                   jax.ShapeDtypeStruct((n, IDX_DIM), BF16),
                   jax.ShapeDtypeStruct((n // DSA_TK, LANE, DSA_TK), BF16),
                   jax.ShapeDtypeStruct((8, n), F32)],
        compiler_params=_cparams(1),
        name="even_rope",
    )(pe, pe, pe, pe, pe, pe, pe, tq, tk, td, ti)


def _ret_kernel(gch_ref, q_ref, k_ref, v_ref, g_ref, dm_ref, qd_ref, kd_ref, gng_ref, gnb_ref,
                s0_ref, *rest):
    o_ref, s_ref, s_sc = rest[-3], rest[-2], rest[-1]
    c = pl.program_id(1)

    @pl.when(c == 0)
    def _():
        s_sc[...] = s0_ref[0]

    q = q_ref[...]
    k = k_ref[...]
    qx = (q * qd_ref[...]).astype(BF16)
    kw = (k * kd_ref[...]).astype(BF16)
    qb = q.astype(BF16)
    kb = k.astype(BF16)
    vb = v_ref[...].astype(BF16)
    gate = g_ref[...]
    for h in range(RET_HEADS):
        ks = slice(h * RET_DK, (h + 1) * RET_DK)
        vs = slice(h * RET_DV, (h + 1) * RET_DV)
        att = _dot_nt(qb[:, ks], kb[:, ks]) * dm_ref[h]
        s_old = s_sc[h]
        y = _dot(att.astype(BF16), vb[:, vs]) + _dot(qx[:, ks], s_old.astype(BF16))
        s_sc[h] = s_old * gch_ref[h] + _dot_tn(kw[:, ks], vb[:, vs])
        yn = _layer_norm(y, gng_ref[:, vs], gnb_ref[:, vs])
        o_ref[:, vs] = _silu(gate[:, vs]) * yn

    @pl.when(c == pl.num_programs(1) - 1)
    def _():
        s_ref[0] = s_sc[...]


def _retention(qa, ka, pe, s0, gn_g, gn_b, n_seq, t, row0, prev_out):
    n = qa.shape[0]
    lc = min(RET_LC, t)
    nc = t // lc
    blk0 = row0 // lc
    log_g = jnp.log(1.0 - 2.0 ** (-5.0 - jnp.arange(RET_HEADS, dtype=F32)))
    pos = jnp.arange(lc, dtype=F32)
    diff = pos[:, None] - pos[None, :]
    dmask = jnp.where(diff >= 0, jnp.exp(jnp.maximum(diff, 0.0)[None] * log_g[:, None, None]), 0.0)
    w_end = jnp.exp((lc - 1 - pos)[:, None] * log_g[None, :])
    xi = jnp.exp((pos + 1.0)[:, None] * log_g[None, :])
    kdec = jnp.repeat(w_end, RET_DK, axis=1)
    qdec = jnp.repeat(xi, RET_DK, axis=1)
    gch = jnp.exp(lc * log_g)

    def rows(wd, j):
        return pl.BlockSpec((lc, wd), lambda b, c: (blk0 + b * nc + c, j))

    def const(shape):
        nd = len(shape)
        return pl.BlockSpec(shape, lambda b, c: (0,) * nd)

    in_specs = [pl.BlockSpec(memory_space=pltpu.SMEM),
                rows(256, 0), rows(256, 0), rows(512, 1), rows(512, 2),
                const((RET_HEADS, lc, lc)), const((lc, 256)), const((lc, 256)),
                const((1, 512)), const((1, 512)),
                pl.BlockSpec((1, RET_HEADS, RET_DK, RET_DV), lambda b, c: (b, 0, 0, 0))]
    args = [gch, qa, ka, pe, pe, dmask, qdec, kdec,
            gn_g.reshape(1, 512), gn_b.reshape(1, 512), s0]
    aliases = {}
    if prev_out is not None:
        in_specs.append(pl.BlockSpec(memory_space=pl.ANY))
        args.append(prev_out)
        aliases = {len(args) - 1: 0}
    return pl.pallas_call(
        _ret_kernel,
        grid=(n_seq, nc),
        in_specs=in_specs,
        out_specs=[pl.BlockSpec((lc, 512), lambda b, c: (blk0 + b * nc + c, 0)),
                   pl.BlockSpec((1, RET_HEADS, RET_DK, RET_DV), lambda b, c: (b, 0, 0, 0))],
        out_shape=[jax.ShapeDtypeStruct((n, 512), F32),
                   jax.ShapeDtypeStruct((n_seq, RET_HEADS, RET_DK, RET_DV), F32)],
        scratch_shapes=[pltpu.VMEM((RET_HEADS, RET_DK, RET_DV), F32)],
        input_output_aliases=aliases,
        compiler_params=_cparams(2),
        name="retention",
    )(*args)


def _col_reduce(x, op):
    r, c = x.shape
    return op(op(x.reshape(r // 8, 8, c), axis=0), axis=0, keepdims=True)


def _dsa_kernel(nkb_ref, q_ref, iq_ref, iwt_ref, qlim_ref, k_ref, vt_ref, ik_ref, *rest,
                topk, tq_out):
    o_ref, key_sc, m_sc, l_sc, acc_sc, lga_sc, lgb_sc = rest[-7:]
    nkb = nkb_ref[pl.program_id(1)]
    tq = qlim_ref.shape[1]
    tk = key_sc.shape[1]
    group = DSA_HEADS // DSA_KV_HEADS
    qlim = qlim_ref[...]
    iwt = iwt_ref[...]
    iqs = iq_ref[...]
    idx_scale = (IDX_HEADS * IDX_DIM) ** -0.5
    krow = lax.broadcasted_iota(I32, (tk, tq), 0)

    def score_body(kb, carry):
        off = pl.multiple_of(kb * tk, tk)
        s_all = _dot_nt(ik_ref[pl.ds(off, tk), :], iqs)
        s = jnp.zeros((tk, tq), F32)
        for h in range(IDX_HEADS):
            s = s + iwt[h:h + 1, :] * jnp.maximum(s_all[:, h * tq:(h + 1) * tq], 0.0)
        s = s * idx_scale
        s = jnp.where(s == 0.0, 0.0, s)
        bits = pltpu.bitcast(s, I32)
        key = jnp.where(bits >= 0, bits, bits ^ jnp.int32(0x7FFFFFFF))
        adm = (off + krow) < qlim
        key_sc[kb] = jnp.where(adm, key, jnp.int32(INT_MIN))
        return carry

    n_pair = (nkb + 1) // 2
    last = 2 * n_pair - 1

    def score_pair(j, carry):
        return score_body(2 * j + 1, score_body(2 * j, carry))

    lax.fori_loop(0, n_pair, score_pair, 0)

    def count(pred):
        def body(kb, acc):
            hit = jnp.where(pred(key_sc[kb]), 1.0, 0.0)
            return acc + jnp.sum(hit.reshape(tk // 64, 64, tq), axis=0)
        acc = lax.fori_loop(0, nkb, body, jnp.zeros((64, tq), F32))
        return jnp.sum(acc, axis=0, keepdims=True)

    def bit_body(it, ans):
        cand = ans + (jnp.int32(1) << (31 - it))
        return jnp.where(count(lambda k: k >= cand) >= topk, cand, ans)

    t = lax.fori_loop(0, 32, bit_body, jnp.full((1, tq), INT_MIN, I32))
    need = topk - count(lambda k: k > t)

    m_sc[...] = jnp.full(m_sc.shape, 0.1 * NEG_BIG, F32)
    l_sc[...] = jnp.zeros(l_sc.shape, F32)
    acc_sc[...] = jnp.zeros(acc_sc.shape, F32)
    r_i = lax.broadcasted_iota(I32, (tk, tk), 0)
    c_i = lax.broadcasted_iota(I32, (tk, tk), 1)
    lower = (c_i < r_i).astype(BF16)

    def logits_stage(kb, n_eq, dst):
        off = pl.multiple_of(kb * tk, tk)
        key = key_sc[kb]
        adm = (off + krow) < qlim
        eq = jnp.logical_and(key == t, adm)
        eqf = jnp.where(eq, 1.0, 0.0)
        pref = _dot(lower, eqf.astype(BF16))
        sel = jnp.logical_and(adm, jnp.logical_or(
            key > t, jnp.logical_and(eq, (n_eq + pref) < need)))
        bias = jnp.where(sel, 0.0, NEG_BIG)
        for n in range(DSA_KV_HEADS):
            lg_all = _dot_nt(k_ref[n, pl.ds(off, tk), :], q_ref[n])
            for g in range(group):
                ls = slice(g * tq, (g + 1) * tq)
                dst[n, :, ls] = lg_all[:, ls] + bias
        return n_eq + _col_reduce(eqf, jnp.sum)

    def softmax_stage(kb, src):
        vt = vt_ref[kb]
        for n in range(DSA_KV_HEADS):
            ps, alphas = [], []
            for g in range(group):
                ls = slice(g * tq, (g + 1) * tq)
                lg = src[n, :, ls]
                m_old = m_sc[n, :, ls]
                m_new = jnp.maximum(m_old, _col_reduce(lg, jnp.max))
                p = jnp.exp2(lg - m_new)
                alpha = jnp.exp2(m_old - m_new)
                l_sc[n, :, ls] = alpha * l_sc[n, :, ls] + _col_reduce(p, jnp.sum)
                m_sc[n, :, ls] = m_new
                ps.append(p.astype(BF16))
                alphas.append(alpha)
            p_all = jnp.concatenate(ps, axis=1)
            alpha_all = jnp.concatenate(alphas, axis=1)
            pv = _dot(vt[n * DSA_HD:(n + 1) * DSA_HD, :], p_all)
            acc_sc[n] = alpha_all * acc_sc[n] + pv

    def pair_body(j, n_eq):
        kb0 = 2 * j
        softmax_stage(kb0, lga_sc)
        n_eq = logits_stage(kb0 + 1, n_eq, lgb_sc)
        softmax_stage(kb0 + 1, lgb_sc)
        return logits_stage(jnp.minimum(kb0 + 2, last), n_eq, lga_sc)

    lax.fori_loop(0, n_pair, pair_body, logits_stage(0, jnp.zeros((1, tq), F32), lga_sc))
    pieces = []
    for n in range(DSA_KV_HEADS):
        o_n = acc_sc[n] / l_sc[n]
        for g in range(group):
            pieces.append(o_n[:, g * tq:(g + 1) * tq])
    o_ref[...] = jnp.concatenate(pieces, axis=0).T[:tq_out, :]


def _dsa(q_st, iq_st, iw_t, qlim, nkb, k_hm, v_t, ik_bf, n_seq, nq, s_len, tq_out, row0, n_out,
         topk, prev_out):
    tq = DSA_TQ
    group = DSA_HEADS // DSA_KV_HEADS
    blk0 = row0 // tq_out
    in_specs = [pl.BlockSpec((DSA_KV_HEADS, group * tq, DSA_HD), lambda b, i, s: (0, b * nq + i, 0)),
                pl.BlockSpec((IDX_HEADS * tq, IDX_DIM), lambda b, i, s: (b * nq + i, 0)),
                pl.BlockSpec((8, tq), lambda b, i, s: (0, b * nq + i)),
                pl.BlockSpec((1, tq), lambda b, i, s: (0, i)),
                pl.BlockSpec((DSA_KV_HEADS, s_len, DSA_HD), lambda b, i, s: (0, b, 0)),
                pl.BlockSpec((s_len // DSA_TK, LANE, DSA_TK), lambda b, i, s: (b, 0, 0)),
                pl.BlockSpec((s_len, IDX_DIM), lambda b, i, s: (b, 0))]
    args = [nkb, q_st, iq_st, iw_t, qlim, k_hm, v_t, ik_bf]
    aliases = {}
    if prev_out is not None:
        in_specs.append(pl.BlockSpec(memory_space=pl.ANY))
        args.append(prev_out)
        aliases = {len(args) - 1: 0}
    grid_spec = pltpu.PrefetchScalarGridSpec(
        num_scalar_prefetch=1,
        grid=(n_seq, nq),
        in_specs=in_specs,
        out_specs=pl.BlockSpec((tq_out, 512), lambda b, i, s: (blk0 + b * nq + i, 0)),
        scratch_shapes=[pltpu.VMEM((s_len // DSA_TK, DSA_TK, tq), I32),
                        pltpu.VMEM((DSA_KV_HEADS, 1, group * tq), F32),
                        pltpu.VMEM((DSA_KV_HEADS, 1, group * tq), F32),
                        pltpu.VMEM((DSA_KV_HEADS, DSA_HD, group * tq), F32),
                        pltpu.VMEM((DSA_KV_HEADS, DSA_TK, group * tq), F32),
                        pltpu.VMEM((DSA_KV_HEADS, DSA_TK, group * tq), F32)])
    return pl.pallas_call(
        functools.partial(_dsa_kernel, topk=topk, tq_out=tq_out),
        grid_spec=grid_spec,
        out_shape=jax.ShapeDtypeStruct((n_out, 512), F32),
        input_output_aliases=aliases,
        compiler_params=_cparams(2),
        name="dsa_attention",
    )(*args)


def _outproj_ln_kernel(ya_ref, yb_ref, w_ref, h_ref, g_ref, b_ref, o_ref, op_ref):
    half = ya_ref.shape[1]
    y = (_dot(ya_ref[...].astype(BF16), w_ref[:half, :])
         + _dot(yb_ref[...].astype(BF16), w_ref[half:, :]))
    out = _layer_norm(DN_ALPHA * h_ref[...] + y, g_ref[...], b_ref[...])
    o_ref[...] = out
    op_ref[...] = _pack_pairs(out)


def _outproj_ln(ya, yb, w, h, g, b, tm):
    n = h.shape[0]
    return pl.pallas_call(
        _outproj_ln_kernel,
        grid=(n // tm,),
        in_specs=[pl.BlockSpec((tm, 512), lambda i: (i, 0)),
                  pl.BlockSpec((tm, 512), lambda i: (i, 0)),
                  pl.BlockSpec((D_MODEL, D_MODEL), lambda i: (0, 0)),
                  pl.BlockSpec((tm, D_MODEL), lambda i: (i, 0)),
                  pl.BlockSpec((1, D_MODEL), lambda i: (0, 0)),
                  pl.BlockSpec((1, D_MODEL), lambda i: (0, 0))],
        out_specs=[pl.BlockSpec((tm, D_MODEL), lambda i: (i, 0)),
                   pl.BlockSpec((tm, D_MODEL // 2), lambda i: (i, 0))],
        out_shape=[jax.ShapeDtypeStruct((n, D_MODEL), F32),
                   jax.ShapeDtypeStruct((n, D_MODEL // 2), U32)],
        compiler_params=_cparams(1),
        name="out_proj_ln",
    )(ya, yb, w, h, g.reshape(1, -1), b.reshape(1, -1))


def _router_kernel(x_ref, w_ref, b_ref, idx_o, gate_o, rank_o, cnt_o, cnt_sc):
    i = pl.program_id(0)

    @pl.when(i == 0)
    def _():
        cnt_sc[...] = jnp.zeros(cnt_sc.shape, F32)

    tm = x_ref.shape[0]
    ne = w_ref.shape[0]
    x = x_ref[...]
    w = w_ref[...]
    x_hi = x.astype(BF16)
    x_lo = (x - x_hi.astype(F32)).astype(BF16)
    w_hi = w.astype(BF16)
    w_lo = (w - w_hi.astype(F32)).astype(BF16)
    logits = (_dot_nt(w_hi, x_hi) + _dot_nt(w_lo, x_hi) + _dot_nt(w_hi, x_lo)) + b_ref[...]
    erow = lax.broadcasted_iota(I32, (ne, tm), 0)
    vals, idxs = [], []
    onehot = jnp.zeros((ne, tm), F32)
    for _ in range(TOP_K):
        m = jnp.max(logits, axis=0, keepdims=True)
        ix = jnp.min(jnp.where(logits == m, erow, ne), axis=0, keepdims=True)
        hit = erow == ix
        onehot = jnp.where(hit, 1.0, onehot)
        logits = jnp.where(hit, -jnp.inf, logits)
        vals.append(m)
        idxs.append(ix)
    es = [jnp.exp(v - vals[0]) for v in vals]
    den = es[0] + es[1] + es[2] + es[3]
    r_i = lax.broadcasted_iota(I32, (tm, tm), 0)
    c_i = lax.broadcasted_iota(I32, (tm, tm), 1)
    upper = (r_i < c_i).astype(BF16)
    rank_dense = _dot(onehot.astype(BF16), upper) + cnt_sc[...]
    prow = lax.broadcasted_iota(I32, (8, tm), 0)
    idx_out = jnp.zeros((8, tm), I32)
    gate_out = jnp.zeros((8, tm), F32)
    rank_out = jnp.zeros((8, tm), F32)
    for k in range(TOP_K):
        rk = jnp.sum(jnp.where(erow == idxs[k], rank_dense, 0.0), axis=0, keepdims=True)
        idx_out = jnp.where(prow == k, idxs[k], idx_out)
        gate_out = jnp.where(prow == k, es[k] / den, gate_out)
        rank_out = jnp.where(prow == k, rk, rank_out)
    idx_o[...] = idx_out
    gate_o[...] = gate_out
    rank_o[...] = rank_out.astype(I32)
    cnt = cnt_sc[...] + jnp.sum(onehot, axis=1, keepdims=True)
    cnt_sc[...] = cnt
    cnt_o[...] = cnt


def _router(x, w_r, b_r, tm):
    n = x.shape[0]
    row = pl.BlockSpec((8, tm), lambda i: (0, i))
    return pl.pallas_call(
        _router_kernel,
        grid=(n // tm,),
        in_specs=[pl.BlockSpec((tm, D_MODEL), lambda i: (i, 0)),
                  pl.BlockSpec((N_EXPERTS, D_MODEL), lambda i: (0, 0)),
                  pl.BlockSpec((N_EXPERTS, 1), lambda i: (0, 0))],
        out_specs=[row, row, row, pl.BlockSpec((N_EXPERTS, 1), lambda i: (0, 0))],
        out_shape=[jax.ShapeDtypeStruct((8, n), I32), jax.ShapeDtypeStruct((8, n), F32),
                   jax.ShapeDtypeStruct((8, n), I32), jax.ShapeDtypeStruct((N_EXPERTS, 1), F32)],
        scratch_shapes=[pltpu.VMEM((N_EXPERTS, 1), F32)],
        compiler_params=_cparams(1),
        name="moe_router",
    )(x, w_r.T, b_r.reshape(N_EXPERTS, 1))


def _moe_kernel(be_ref, nu_ref, x_ref, wgu_ref, bgu_ref, wdn_ref, bdn_ref, o_ref, wgu_sc, wdn_sc):
    i = pl.program_id(0)

    @pl.when(jnp.logical_or(i == 0, be_ref[i] != be_ref[jnp.maximum(i - 1, 0)]))
    def _():
        wgu_sc[...] = wgu_ref[0, 0].astype(BF16)
        wdn_sc[...] = wdn_ref[0, 0].astype(BF16)

    @pl.when(i < nu_ref[0])
    def _():
        xa, xb = _unpack_pairs(x_ref[...])
        x = jnp.concatenate([xa.astype(BF16), xb.astype(BF16)], axis=1)
        h = _dot(x, wgu_sc[...]) + bgu_ref[0, 0]
        g = jnp.minimum(h[:, :D_FF], SWIGLU_LIMIT)
        up = jnp.clip(h[:, D_FF:], -SWIGLU_LIMIT, SWIGLU_LIMIT)
        a = (up + 1.0) * g * jax.nn.sigmoid(SWIGLU_ALPHA * g)
        o_ref[...] = _pack_pairs(_dot(a.astype(BF16), wdn_sc[...]) + bdn_ref[0, 0])

    @pl.when(i >= nu_ref[0])
    def _():
        o_ref[...] = jnp.zeros(o_ref.shape, U32)


def _moe_experts(xs, blk_e, n_used, layer, w_gu, b_gu, w_dn, b_dn):
    n_rows = xs.shape[0]
    tm = MOE_TM
    depth = w_gu.shape[0]
    grid_spec = pltpu.PrefetchScalarGridSpec(
        num_scalar_prefetch=2,
        grid=(n_rows // tm,),
        in_specs=[pl.BlockSpec((tm, D_MODEL // 2), lambda i, be, nu: (i, 0)),
                  pl.BlockSpec((1, 1, D_MODEL, 2 * D_FF), lambda i, be, nu: (layer, be[i], 0, 0)),
                  pl.BlockSpec((1, 1, 1, 2 * D_FF), lambda i, be, nu: (layer, be[i], 0, 0)),
                  pl.BlockSpec((1, 1, D_FF, D_MODEL), lambda i, be, nu: (layer, be[i], 0, 0)),
                  pl.BlockSpec((1, 1, 1, D_MODEL), lambda i, be, nu: (layer, be[i], 0, 0))],
        out_specs=pl.BlockSpec((tm, D_MODEL // 2), lambda i, be, nu: (i, 0)),
        scratch_shapes=[pltpu.VMEM((D_MODEL, 2 * D_FF), BF16), pltpu.VMEM((D_FF, D_MODEL), BF16)])
    return pl.pallas_call(
        _moe_kernel,
        grid_spec=grid_spec,
        out_shape=jax.ShapeDtypeStruct((n_rows, D_MODEL // 2), U32),
        compiler_params=_cparams(1),
        name="moe_experts",
    )(blk_e, n_used, xs, w_gu, b_gu.reshape(depth, N_EXPERTS, 1, -1), w_dn,
      b_dn.reshape(depth, N_EXPERTS, 1, -1))


def _combine_ln_kernel(h_ref, y0_ref, y1_ref, y2_ref, y3_ref, gate_ref, g_ref, b_ref, *o_refs,
                       n_first):
    gate = gate_ref[...]
    ya, yb = None, None
    for k, y_ref in enumerate((y0_ref, y1_ref, y2_ref, y3_ref)):
        a, b = _unpack_pairs(y_ref[...])
        gk = gate[:, k:k + 1]
        ya = gk * a if ya is None else ya + gk * a
        yb = gk * b if yb is None else yb + gk * b
    y = jnp.concatenate([ya, yb], axis=1)
    out = _layer_norm(DN_ALPHA * h_ref[...] + y, g_ref[...], b_ref[...])
    if len(o_refs) == 1:
        o_refs[0][...] = out
    else:
        @pl.when(pl.program_id(0) < n_first)
        def _():
            o_refs[0][...] = out

        @pl.when(pl.program_id(0) >= n_first)
        def _():
            o_refs[1][...] = out


def _combine_ln(h, ys, gate, g, b, tm, split_rows=None):
    n = h.shape[0]
    row = pl.BlockSpec((tm, D_MODEL), lambda i: (i, 0))
    half = pl.BlockSpec((tm, D_MODEL // 2), lambda i: (i, 0))
    vec = pl.BlockSpec((1, D_MODEL), lambda i: (0, 0))
    if split_rows is None:
        n_first = n // tm
        out_specs = row
        out_shape = jax.ShapeDtypeStruct((n, D_MODEL), F32)
    else:
        n_first = split_rows // tm
        out_specs = [pl.BlockSpec((tm, D_MODEL), lambda i: (jnp.minimum(i, n_first - 1), 0)),
                     pl.BlockSpec((tm, D_MODEL), lambda i: (jnp.maximum(i - n_first, 0), 0))]
        out_shape = [jax.ShapeDtypeStruct((split_rows, D_MODEL), F32),
                     jax.ShapeDtypeStruct((n - split_rows, D_MODEL), F32)]
    return pl.pallas_call(
        functools.partial(_combine_ln_kernel, n_first=n_first),
        grid=(n // tm,),
        in_specs=[row, half, half, half, half, pl.BlockSpec((tm, TOP_K), lambda i: (i, 0)), vec, vec],
        out_specs=out_specs,
        out_shape=out_shape,
        compiler_params=_cparams(1),
        name="moe_combine_ln",
    )(h, ys[0], ys[1], ys[2], ys[3], gate, g.reshape(1, -1), b.reshape(1, -1))


def _rows(x, idx):
    return x.at[idx].get(mode="promise_in_bounds")


def _moe_layer(h, h_packed, layer, w_r, b_r, w_gu, b_gu, w_dn, b_dn, ln_g, ln_b, tm, split_rows=None):
    n = h.shape[0]
    n_pair = n * TOP_K
    idx_t, gate_t, rank_t, cnt = _router(h, w_r, b_r, tm)
    top_i = idx_t[:TOP_K].T
    counts = cnt[:, 0].astype(I32)
    padded = (counts + MOE_TM - 1) // MOE_TM * MOE_TM
    pad_end = jnp.cumsum(padded)
    start = pad_end - padded
    first = jnp.cumsum(counts) - counts
    dest = _rows(start, top_i) + rank_t[:TOP_K].T
    n_blk = -(-n_pair // MOE_TM) + N_EXPERTS
    n_used = (pad_end[-1] // MOE_TM).astype(I32)
    blk_row = jnp.minimum(jnp.arange(n_blk, dtype=I32), n_used - 1) * MOE_TM
    blk_e = jnp.sum((pad_end[None, :] <= blk_row[:, None]).astype(I32), axis=1)
    blk_e = jnp.minimum(blk_e, N_EXPERTS - 1)
    order = jnp.argsort(top_i.reshape(-1), stable=True).astype(I32)
    row_in_e = (jnp.arange(n_blk, dtype=I32)[:, None] * MOE_TM - _rows(start, blk_e)[:, None]
                + jnp.arange(MOE_TM, dtype=I32)[None, :])
    pair = jnp.clip(_rows(first, blk_e)[:, None] + row_in_e, 0, n_pair - 1).reshape(-1)
    src = _rows(order, pair) // TOP_K
    xs = _rows(h_packed, src)
    ybuf = _moe_experts(xs, blk_e, n_used.reshape(1), layer, w_gu, b_gu, w_dn, b_dn)
    ys = [_rows(ybuf, dest[:, k]) for k in range(TOP_K)]
    return _combine_ln(h, ys, gate_t[:TOP_K].T, ln_g, ln_b, tm, split_rows)


def _band_kernel(q_ref, *rest, nkb):
    k_refs = rest[:nkb]
    v_refs = rest[nkb:2 * nkb]
    bias_ref = rest[2 * nkb]
    o_ref = rest[-1]
    c = pl.program_id(1)
    tq = q_ref.shape[0]
    tkb = k_refs[0].shape[0]
    q_scale = BAND_HD ** -0.5 * math.log2(math.e)
    kbs = [r[...].astype(BF16) for r in k_refs]
    vbs = [r[...].astype(BF16) for r in v_refs]
    off = [jnp.where((c + j - (nkb - 1)) >= 0, 0.0, NEG_BIG) for j in range(nkb)]
    low = lax.broadcasted_iota(I32, (tq, LANE), 1) < BAND_HD
    for pair in range(BAND_HEADS // 2):
        ls = slice(pair * LANE, (pair + 1) * LANE)
        qp = q_ref[:, ls] * q_scale
        outs = []
        for par in range(2):
            h = 2 * pair + par
            qh = jnp.where(low == (par == 0), qp, 0.0).astype(BF16)
            lgs = [_dot_nt(qh, kbs[j][:, ls]) + (bias_ref[h, :, j * tkb:(j + 1) * tkb] + off[j])
                   for j in range(nkb)]
            mx = lgs[0]
            for j in range(1, nkb):
                mx = jnp.maximum(mx, lgs[j])
            m = mx.max(axis=1, keepdims=True)
            ps = [jnp.exp2(lg - m) for lg in lgs]
            sm = ps[0]
            for j in range(1, nkb):
                sm = sm + ps[j]
            den = sm.sum(axis=1, keepdims=True)
            acc = _dot(ps[0].astype(BF16), vbs[0][:, ls])
            for j in range(1, nkb):
                acc = acc + _dot(ps[j].astype(BF16), vbs[j][:, ls])
            outs.append(acc * (1.0 / den))
        o_ref[:, ls] = jnp.where(low, outs[0], outs[1])


def _band(q_arr, k_arr, v_arr, cols, bias, n_seq, t, tq, tkb, nkb, q_row0, kv_blocks_per_seq,
          n_out, prev_out):
    nq = t // tq
    qblk0 = q_row0 // tq
    qcol, kcol, vcol = cols

    def kv_spec(j, col):
        def ix(b, c):
            return (b * kv_blocks_per_seq + jnp.maximum(c + j - (nkb - 1), 0), col)
        return pl.BlockSpec((tkb, 512), ix)

    in_specs = ([pl.BlockSpec((tq, 512), lambda b, c: (qblk0 + b * nq + c, qcol))]
                + [kv_spec(j, kcol) for j in range(nkb)]
                + [kv_spec(j, vcol) for j in range(nkb)]
                + [pl.BlockSpec(bias.shape, lambda b, c: (0, 0, 0))])
    args = [q_arr] + [k_arr] * nkb + [v_arr] * nkb + [bias]
    aliases = {}
    if prev_out is not None:
        in_specs.append(pl.BlockSpec(memory_space=pl.ANY))
        args.append(prev_out)
        aliases = {len(args) - 1: 0}
    return pl.pallas_call(
        functools.partial(_band_kernel, nkb=nkb),
        grid=(n_seq, nq),
        in_specs=in_specs,
        out_specs=pl.BlockSpec((tq, 512), lambda b, c: (qblk0 + b * nq + c, 0)),
        out_shape=jax.ShapeDtypeStruct((n_out, 512), F32),
        input_output_aliases=aliases,
        compiler_params=_cparams(2),
        name="band_attention",
    )(*args)


def _band_bias(rel_bias, tq, n_keys, key0):
    n_off = tq + n_keys - 1
    d_max = tq - 1 - key0
    rel = np.clip(d_max - np.arange(n_off), -REL_CLIP, REL_CLIP) + REL_CLIP
    vals = jnp.concatenate([rel_bias[:, rel], jnp.zeros((rel_bias.shape[0], 1), F32)], axis=1)
    rot = jnp.tile(vals, (1, tq))[:, :tq * n_off].reshape(-1, tq, n_off)
    toep = rot[:, :, tq - 1:tq - 1 + n_keys]
    qp = np.arange(tq)[:, None]
    kp = key0 + np.arange(n_keys)[None, :]
    cs = (qp // CHUNK) * CHUNK
    band = np.logical_and(kp >= cs - BAND_PAST, kp < cs + CHUNK)
    return jnp.where(jnp.asarray(band)[None], toep * math.log2(math.e), NEG_BIG).astype(F32)


def _ssd_kernel(dsk_ref, z_ref, xbc_ref, dt_ref, cw_ref, cb_ref, dtb_ref, alog_ref, ng_ref,
                h0_ref, c0_ref, *rest):
    o_ref, h_ref, cl_ref, h_sc, xe_sc, y_sc = rest[-6:]
    c = pl.program_id(1)
    lc = z_ref.shape[0]

    @pl.when(c == 0)
    def _():
        h_sc[...] = h0_ref[0]
        xe_sc[0:8, :] = jnp.zeros((8, xe_sc.shape[1]), F32)
        xe_sc[8 - (SSD_CONV - 1):8, :] = c0_ref[0]

    xe_sc[8:8 + lc, :] = xbc_ref[...]
    conv = cb_ref[...] + cw_ref[SSD_CONV - 1:SSD_CONV, :] * xe_sc[8:8 + lc, :]
    for s in range(1, SSD_CONV):
        conv = conv + cw_ref[SSD_CONV - 1 - s:SSD_CONV - s, :] * xe_sc[8 - s:8 - s + lc, :]
    u = _silu(conv)
    gs = SSD_GROUPS * SSD_STATE
    xs = u[:, :SSD_INNER]
    bm = u[:, SSD_INNER:SSD_INNER + gs].astype(BF16)
    cm = u[:, SSD_INNER + gs:].astype(BF16)
    dx = dt_ref[...] + dtb_ref[...]
    dtv = jnp.maximum(dx, 0.0) + jnp.log1p(jnp.exp(-jnp.abs(dx)))
    a = dtv * (-jnp.exp(alog_ref[...]))
    r_i = lax.broadcasted_iota(I32, (lc, lc), 0)
    c_i = lax.broadcasted_iota(I32, (lc, lc), 1)
    causal = c_i <= r_i
    acum = _dot_f32(causal.astype(F32), a)
    acum_t = acum.T
    hpg = SSD_HEADS // SSD_GROUPS
    for g in range(SSD_GROUPS):
        ss = slice(g * SSD_STATE, (g + 1) * SSD_STATE)
        cb = _dot_nt(cm[:, ss], bm[:, ss])
        for jj in range(hpg):
            j = g * hpg + jj
            ps = slice(j * SSD_HD, (j + 1) * SSD_HD)
            col = acum[:, j:j + 1]
            row = acum_t[j:j + 1, :]
            lmat = jnp.exp(jnp.where(causal, col - row, -jnp.inf))
            x_j = xs[:, ps]
            xdt = x_j * dtv[:, j:j + 1]
            h_old = h_sc[j]
            y = _dot((cb * lmat).astype(BF16), xdt.astype(BF16))
            y = y + _dot_nt(cm[:, ss], h_old.astype(BF16)) * jnp.exp(col)
            y = y + dsk_ref[j] * x_j
            last = acum[lc - 1:lc, j:j + 1]
            st = _dot_tn((xdt * jnp.exp(last - col)).astype(BF16), bm[:, ss])
            h_sc[j] = h_old * jnp.exp(last) + st
            y_sc[:, ps] = y
    yd = y_sc[...] * _silu(z_ref[...])
    gw = SSD_INNER // SSD_GROUPS
    for g in range(SSD_GROUPS):
        ws = slice(g * gw, (g + 1) * gw)
        yg = yd[:, ws]
        ms = jnp.mean(yg * yg, axis=-1, keepdims=True)
        o_ref[:, ws] = yg * lax.rsqrt(ms + LN_EPS) * ng_ref[:, ws]
    xe_sc[0:8, :] = xe_sc[lc:lc + 8, :]

    @pl.when(c == pl.num_programs(1) - 1)
    def _():
        h_ref[0] = h_sc[...]
        cl_ref[0] = xe_sc[8 - (SSD_CONV - 1):8, :]


def _ssd(po, h0, c0, conv_w, conv_b, dt_bias, a_log, d_skip, norm_g, n_seq, t, row0, prev_out):
    n = po.shape[0]
    lc = min(SSD_LC, t)
    nc = t // lc
    blk0 = row0 // lc
    cdim = conv_w.shape[1]

    def rows(wd, j):
        return pl.BlockSpec((lc, wd), lambda b, c: (blk0 + b * nc + c, j))

    def const(shape):
        nd = len(shape)
        return pl.BlockSpec(shape, lambda b, c: (0,) * nd)

    pad8 = lambda v: jnp.zeros((1, LANE), F32).at[0, :SSD_HEADS].set(v)
    in_specs = [pl.BlockSpec(memory_space=pltpu.SMEM),
                rows(512, 3), rows(cdim, 2), rows(LANE, 24),
                const((SSD_CONV, cdim)), const((1, cdim)), const((1, LANE)), const((1, LANE)),
                const((1, SSD_INNER)),
                pl.BlockSpec((1, SSD_HEADS, SSD_HD, SSD_STATE), lambda b, c: (b, 0, 0, 0)),
                pl.BlockSpec((1, SSD_CONV - 1, cdim), lambda b, c: (b, 0, 0))]
    args = [d_skip, po, po, po, conv_w, conv_b.reshape(1, -1), pad8(dt_bias), pad8(a_log),
            norm_g.reshape(1, -1), h0, c0]
    aliases = {}
    if prev_out is not None:
        in_specs.append(pl.BlockSpec(memory_space=pl.ANY))
        args.append(prev_out)
        aliases = {len(args) - 1: 0}
    return pl.pallas_call(
        _ssd_kernel,
        grid=(n_seq, nc),
        in_specs=in_specs,
        out_specs=[pl.BlockSpec((lc, 512), lambda b, c: (blk0 + b * nc + c, 0)),
                   pl.BlockSpec((1, SSD_HEADS, SSD_HD, SSD_STATE), lambda b, c: (b, 0, 0, 0)),
                   pl.BlockSpec((1, SSD_CONV - 1, cdim), lambda b, c: (b, 0, 0))],
        out_shape=[jax.ShapeDtypeStruct((n, 512), F32),
                   jax.ShapeDtypeStruct((n_seq, SSD_HEADS, SSD_HD, SSD_STATE), F32),
                   jax.ShapeDtypeStruct((n_seq, SSD_CONV - 1, cdim), F32)],
        scratch_shapes=[pltpu.VMEM((SSD_HEADS, SSD_HD, SSD_STATE), F32),
                        pltpu.VMEM((lc + 8, cdim), F32),
                        pltpu.VMEM((lc, 512), F32)],
        input_output_aliases=aliases,
        compiler_params=_cparams(2),
        name="ssd_scan",
    )(*args)


def _pad_cols(w, width):
    return jnp.concatenate([w, jnp.zeros((w.shape[0], width - w.shape[1]), w.dtype)], axis=1)


def kernel(x_prompt, x_sample, state_ret, cache_dsa_k, cache_dsa_v, cache_dsa_kidx, cache_band_k, cache_band_v, state_ssm, state_conv, e_w_in, e_w_out, e_gn_g, e_gn_b, o_w_in, o_w_out, o_rel_bias, o_conv_w, o_conv_b, o_dt_bias, o_a_log, o_d_skip, o_norm_g, ln1_g, ln1_b, ln2_g, ln2_b, router_w, router_b, exp_w_gu, exp_b_gu, exp_w_dn, exp_b_dn):
    bp, tp, _ = x_prompt.shape
    bs, ts, _ = x_sample.shape
    past = cache_dsa_k.shape[2]
    n_p, n_s = bp * tp, bs * ts
    n = n_p + n_s
    tm = math.gcd(512, math.gcd(n_p, n_s))
    assert tp % tm == 0 and tm % ts == 0 and ts == CHUNK

    h = jnp.concatenate([x_prompt.reshape(n_p, D_MODEL), x_sample.reshape(n_s, D_MODEL)], axis=0)

    def blank():
        return jnp.zeros((n, 512), F32)

    pe = _proj(h, _pad_cols(e_w_in[0], EVEN_W).astype(BF16), tm)
    pos_p = jnp.arange(tp, dtype=I32)
    pos_s = past + jnp.arange(ts, dtype=I32)
    pos_tab = jnp.concatenate([pos_p, jnp.tile(pos_s, tm // ts)])
    tabs = (_rope_tables(pos_tab, RET_HEADS, RET_DK, RET_DK, RET_THETA),
            _rope_tables(pos_tab, RET_HEADS, RET_DK, RET_DK, RET_THETA, scale=RET_DK ** -0.5),
            _rope_tables(pos_tab, DSA_HEADS, DSA_HD, DSA_ROT, ROPE_THETA),
            _rope_tables(pos_tab, 1, IDX_DIM, DSA_ROT, ROPE_THETA, pad_to=LANE))
    (qa, ka, qb, kb, iq, ikw, q_st, iq_st, k_hm, ik_bf, v_t, iw_t) = _even_prep(
        pe, tabs, tm, n_p // tm, tp // tm)

    ya, ret_p = _retention(qa, ka, pe, jnp.zeros((bp, RET_HEADS, RET_DK, RET_DV), F32),
                           e_gn_g[0], e_gn_b[0], bp, tp, 0, blank())
    ya, ret_s = _retention(qa, ka, pe, state_ret[0], e_gn_g[0], e_gn_b[0], bs, ts, n_p, ya)

    topk_p = min(DSA_TOPK_MAX, tp // 4)
    qlim_p = (((pos_p // CHUNK) + 1) * CHUNK).reshape(1, tp)
    nq_p = tp // DSA_TQ
    nkb_p = ((jnp.arange(nq_p, dtype=I32) + 1) * DSA_TQ + DSA_TK - 1) // DSA_TK
    yb = _dsa(q_st, iq_st, iw_t, qlim_p, nkb_p, k_hm, v_t, ik_bf, bp, nq_p, tp, DSA_TQ, 0, n,
              topk_p, blank())

    s_len = past + ts
    s_pad = -(-s_len // (2 * DSA_TK)) * (2 * DSA_TK)
    topk_s = min(DSA_TOPK_MAX, s_len // 4)
    group = DSA_HEADS // DSA_KV_HEADS

    def cat_keys(cache, new, wd):
        zpad = jnp.zeros((bs, s_pad - s_len, wd), F32)
        return jnp.concatenate([cache, new.reshape(bs, ts, wd), zpad], axis=1)

    def pad_q(x):
        return jnp.concatenate([x, jnp.zeros((bs, DSA_TQ - ts) + x.shape[2:], x.dtype)], axis=1)

    ks = cat_keys(cache_dsa_k[0].reshape(bs, past, LANE), kb[n_p:], LANE)
    vs = cat_keys(cache_dsa_v[0].reshape(bs, past, LANE), pe[n_p:, 2176:2304], LANE)
    iks = cat_keys(cache_dsa_kidx[0], ikw[n_p:, :IDX_DIM], IDX_DIM)
    k_hm_s = ks.reshape(bs, s_pad, DSA_KV_HEADS, DSA_HD).transpose(2, 0, 1, 3).reshape(
        DSA_KV_HEADS, bs * s_pad, DSA_HD).astype(BF16)
    v_t_s = vs.reshape(bs, s_pad // DSA_TK, DSA_TK, LANE).transpose(0, 1, 3, 2).reshape(
        bs * (s_pad // DSA_TK), LANE, DSA_TK).astype(BF16)
    ik_s = iks.reshape(bs * s_pad, IDX_DIM).astype(BF16)
    q_s = pad_q(qb[n_p:].reshape(bs, ts, DSA_KV_HEADS, group, DSA_HD))
    q_st_s = q_s.transpose(2, 0, 3, 1, 4).reshape(DSA_KV_HEADS, bs * group * DSA_TQ, DSA_HD)
    iq_s = pad_q(iq[n_p:].reshape(bs, ts, IDX_HEADS, IDX_DIM))
    iq_st_s = iq_s.transpose(0, 2, 1, 3).reshape(bs * IDX_HEADS * DSA_TQ, IDX_DIM)
    iw_t_s = pad_q(ikw[n_p:, IDX_DIM:IDX_DIM + 8].reshape(bs, ts, 8)).reshape(bs * DSA_TQ, 8).T
    qlim_s = jnp.full((1, DSA_TQ), s_len, I32)
    nkb_s = jnp.full((1,), -(-s_len // DSA_TK), I32)
    yb = _dsa(q_st_s, iq_st_s, iw_t_s, qlim_s, nkb_s, k_hm_s, v_t_s, ik_s, bs, 1, s_pad, ts, n_p, n,
              topk_s, yb)

    h, h_packed = _outproj_ln(ya, yb, e_w_out[0].astype(BF16), h, ln1_g[0], ln1_b[0], tm)
    h = _moe_layer(h, h_packed, 0, router_w[0], router_b[0], exp_w_gu, exp_b_gu, exp_w_dn, exp_b_dn,
                   ln2_g[0], ln2_b[0], tm)

    po = _proj(h, _pad_cols(o_w_in[0], ODD_W).astype(BF16), tm)
    tq_p = min(BAND_TQ, tp)
    nkb_band = BAND_PAST // tq_p + 1
    bias_p = _band_bias(o_rel_bias[0], tq_p, nkb_band * tq_p, -(nkb_band - 1) * tq_p)
    yc = _band(po, po, po, (0, 1, 2), bias_p, bp, tp, tq_p, tq_p, nkb_band, 0, tp // tq_p, n,
               blank())
    band_len = cache_band_k.shape[2]
    kc_new = po[n_p:, 512:1024].reshape(bs, ts, 512)
    vc_new = po[n_p:, 1024:1536].reshape(bs, ts, 512)
    kcat = jnp.concatenate([cache_band_k[0].reshape(bs, band_len, 512), kc_new], axis=1)
    vcat = jnp.concatenate([cache_band_v[0].reshape(bs, band_len, 512), vc_new], axis=1)
    wlen = band_len + ts
    bias_s = _band_bias(o_rel_bias[0], ts, wlen, -band_len)
    yc = _band(po, kcat.reshape(bs * wlen, 512), vcat.reshape(bs * wlen, 512), (0, 0, 0), bias_s,
               bs, ts, ts, wlen, 1, n_p, 1, n, yc)

    ssd_w = (o_conv_w[0], o_conv_b[0], o_dt_bias[0], o_a_log[0], o_d_skip[0], o_norm_g[0])
    cdim = o_conv_w.shape[2]
    yd, ssm_p, conv_p = _ssd(po, jnp.zeros((bp, SSD_HEADS, SSD_HD, SSD_STATE), F32),
                             jnp.zeros((bp, SSD_CONV - 1, cdim), F32), *ssd_w, bp, tp, 0, blank())
    yd, ssm_s, conv_s = _ssd(po, state_ssm[0], state_conv[0], *ssd_w, bs, ts, n_p, yd)

    h, h_packed = _outproj_ln(yc, yd, o_w_out[0].astype(BF16), h, ln1_g[1], ln1_b[1], tm)
    h_p, h_s = _moe_layer(h, h_packed, 1, router_w[1], router_b[1], exp_w_gu, exp_b_gu, exp_w_dn, exp_b_dn,
                          ln2_g[1], ln2_b[1], tm, split_rows=n_p)

    keep = min(BAND_PAST, tp)
    kd = DSA_KV_HEADS * DSA_HD
    kc_p = po[:n_p, 512:1024].reshape(bp, tp, BAND_HEADS, BAND_HD)[:, -keep:]
    vc_p = po[:n_p, 1024:1536].reshape(bp, tp, BAND_HEADS, BAND_HD)[:, -keep:]
    return (h_p.reshape(bp, tp, D_MODEL), h_s.reshape(bs, ts, D_MODEL),
            ret_p[None],
            kb[:n_p].reshape(1, bp, tp, DSA_KV_HEADS, DSA_HD),
            pe[:n_p, 2176:2176 + kd].reshape(1, bp, tp, DSA_KV_HEADS, DSA_HD),
            ikw[:n_p, :IDX_DIM].reshape(1, bp, tp, IDX_DIM),
            kc_p[None], vc_p[None], ssm_p[None], conv_p[None],
            ret_s[None],
            kb[n_p:].reshape(1, bs, ts, DSA_KV_HEADS, DSA_HD),
            pe[n_p:, 2176:2176 + kd].reshape(1, bs, ts, DSA_KV_HEADS, DSA_HD),
            ikw[n_p:, :IDX_DIM].reshape(1, bs, ts, IDX_DIM),
            kc_new.reshape(1, bs, ts, BAND_HEADS, BAND_HD), vc_new.reshape(1, bs, ts, BAND_HEADS, BAND_HD),
            ssm_s[None], conv_s[None])
```

```python
import functools
import math

import jax
import jax.numpy as jnp
import numpy as np
from jax import lax
from jax.experimental import pallas as pl
from jax.experimental.pallas import tpu as pltpu

F32 = jnp.float32
BF16 = jnp.bfloat16
I32 = jnp.int32
U32 = jnp.uint32

D_MODEL = 1024
CHUNK = 64
RET_HEADS, RET_DK, RET_DV, RET_THETA = 8, 32, 64, 10000.0
DSA_HEADS, DSA_KV_HEADS, DSA_HD = 8, 2, 64
DSA_ROT = DSA_HD // 4
IDX_HEADS, IDX_DIM = 4, 64
DSA_TOPK_MAX = 256
ROPE_THETA = 500000.0
BAND_HEADS, BAND_HD, BAND_PREV = 8, 64, 8
BAND_PAST = BAND_PREV * CHUNK
REL_CLIP = 256
SSD_HEADS, SSD_HD, SSD_GROUPS, SSD_STATE, SSD_CONV = 8, 64, 2, 128, 4
SSD_INNER = SSD_HEADS * SSD_HD
N_EXPERTS, TOP_K, D_FF = 32, 4, 1024
SWIGLU_LIMIT, SWIGLU_ALPHA = 7.0, 1.702
DEPTH = 2
DN_ALPHA = (2 * DEPTH) ** 0.25
LN_EPS = 1e-5

LANE = 128
VMEM_LIMIT = 56 * 1024 * 1024
INT_MIN = -(2 ** 31)
NEG_BIG = -1e30

EVEN_IN = 2628
EVEN_W = 2688
ODD_IN = 3080
ODD_W = 3200

MOE_TM = 512
MOE_PARTS = 4
RET_LC = 256
SSD_LC = 256
DSA_TQ = 128
DSA_TK = 256
BAND_TQ = 256


def _cparams(n_axes):
    return pltpu.CompilerParams(dimension_semantics=("arbitrary",) * n_axes,
                                vmem_limit_bytes=VMEM_LIMIT)


def _dot(a, b):
    return jnp.dot(a, b, preferred_element_type=F32)


def _dot_nt(a, b):
    return lax.dot_general(a, b, (((1,), (1,)), ((), ())), preferred_element_type=F32)


def _dot_tn(a, b):
    return lax.dot_general(a, b, (((0,), (0,)), ((), ())), preferred_element_type=F32)


def _dot_f32(a, b):
    return jnp.dot(a, b, preferred_element_type=F32, precision=lax.Precision.HIGHEST)


def _layer_norm(x, g, b):
    mu = jnp.mean(x, axis=-1, keepdims=True)
    xc = x - mu
    var = jnp.mean(xc * xc, axis=-1, keepdims=True)
    return xc * lax.rsqrt(var + LN_EPS) * g + b


def _silu(x):
    return x * jax.nn.sigmoid(x)


def _pack_pairs(x):
    c = x.shape[1] // 2
    hi = pltpu.bitcast(x[:, :c].astype(jnp.bfloat16).astype(F32), U32)
    lo = pltpu.bitcast(x[:, c:].astype(jnp.bfloat16).astype(F32), U32)
    return hi | (lo >> 16)


def _unpack_pairs(w):
    return (pltpu.bitcast(w & jnp.uint32(0xFFFF0000), F32), pltpu.bitcast(w << 16, F32))


def _proj_kernel(x_ref, w_ref, o_ref):
    o_ref[...] = _dot(x_ref[...].astype(BF16), w_ref[...])


def _proj(x, w, tm):
    n, k = x.shape
    wd = w.shape[1]
    return pl.pallas_call(
        _proj_kernel,
        grid=(n // tm,),
        in_specs=[pl.BlockSpec((tm, k), lambda i: (i, 0)),
                  pl.BlockSpec((k, wd), lambda i: (0, 0))],
        out_specs=pl.BlockSpec((tm, wd), lambda i: (i, 0)),
        out_shape=jax.ShapeDtypeStruct((n, wd), F32),
        compiler_params=_cparams(1),
        name="in_proj",
    )(x, w)


def _rope_tables(pos, n_heads, d, rot, theta, scale=1.0, pad_to=None):
    half = rot // 2
    inv = theta ** (-jnp.arange(half, dtype=F32) / half)
    ang = pos.astype(F32)[:, None] * inv[None, :]
    cos, sin = jnp.cos(ang), jnp.sin(ang)
    p = pos.shape[0]
    one = jnp.ones((p, d - rot), F32)
    zr = jnp.zeros((p, d - rot), F32)
    zh = jnp.zeros((p, half), F32)
    c = jnp.tile(jnp.concatenate([cos, cos, one], 1), (1, n_heads))
    a = jnp.tile(jnp.concatenate([-sin, zh, zr], 1), (1, n_heads))
    b = jnp.tile(jnp.concatenate([zh, sin, zr], 1), (1, n_heads))
    if pad_to is not None and pad_to > n_heads * d:
        extra = pad_to - n_heads * d
        c = jnp.concatenate([c, jnp.ones((p, extra), F32)], 1)
        a = jnp.concatenate([a, jnp.zeros((p, extra), F32)], 1)
        b = jnp.concatenate([b, jnp.zeros((p, extra), F32)], 1)
    return jnp.stack([c, a, b]) * scale


def _rope(x, tab_ref, half):
    w = x.shape[-1]
    return (x * tab_ref[0] + pltpu.roll(x, w - half, 1) * tab_ref[1]
            + pltpu.roll(x, half, 1) * tab_ref[2])


def _even_prep_kernel(qa_ref, ka_ref, qb_ref, kb_ref, iq_ref, ikw_ref, v_ref,
                      tq_ref, tk_ref, td_ref, ti_ref,
                      qa_o, ka_o, qb_o, kb_o, iq_o, ikw_o, qst_o, iqst_o, khm_o, ikb_o, vt_o, iwt_o):
    tm = qa_ref.shape[0]
    h = DSA_ROT // 2
    qa_o[...] = _rope(qa_ref[...], tq_ref, RET_DK // 2)
    ka_o[...] = _rope(ka_ref[...], tk_ref, RET_DK // 2)
    qb = (_rope(qb_ref[...], td_ref, h) * (DSA_HD ** -0.5 * math.log2(math.e))).astype(BF16)
    qb_o[...] = qb
    kb = kb_ref[...]
    kb = (kb * td_ref[0, :, :LANE] + pltpu.roll(kb, LANE - h, 1) * td_ref[1, :, :LANE]
          + pltpu.roll(kb, h, 1) * td_ref[2, :, :LANE])
    kb_o[...] = kb
    iq = iq_ref[...]
    w = iq.shape[-1]
    iq = (iq * td_ref[0, :, :w] + pltpu.roll(iq, w - h, 1) * td_ref[1, :, :w]
          + pltpu.roll(iq, h, 1) * td_ref[2, :, :w]).astype(BF16)
    iq_o[...] = iq
    ikw = _rope(ikw_ref[...], ti_ref, h)
    ikw_o[...] = ikw
    group = DSA_HEADS // DSA_KV_HEADS
    for jb in range(tm // DSA_TQ):
        rs = slice(jb * DSA_TQ, (jb + 1) * DSA_TQ)
        for hd in range(DSA_HEADS):
            n, g = divmod(hd, group)
            ro = (jb * group + g) * DSA_TQ
            qst_o[n, ro:ro + DSA_TQ, :] = qb[rs, hd * DSA_HD:(hd + 1) * DSA_HD]
        for hd in range(IDX_HEADS):
            ro = (jb * IDX_HEADS + hd) * DSA_TQ
            iqst_o[ro:ro + DSA_TQ, :] = iq[rs, hd * IDX_DIM:(hd + 1) * IDX_DIM]
    kbb = kb.astype(BF16)
    for n in range(DSA_KV_HEADS):
        khm_o[n] = kbb[:, n * DSA_HD:(n + 1) * DSA_HD]
    ikb_o[...] = ikw[:, :IDX_DIM].astype(BF16)
    v = v_ref[...]
    for j in range(tm // DSA_TK):
        vt_o[j] = v[j * DSA_TK:(j + 1) * DSA_TK, :].T.astype(BF16)
    iwt_o[...] = ikw.T[IDX_DIM:IDX_DIM + 8, :]


def _even_prep(pe, tabs, tm, n_prompt_blocks, tab_blocks):
    n = pe.shape[0]
    tq, tk, td, ti = tabs

    def tix(i):
        return (0, jnp.where(i < n_prompt_blocks, i % tab_blocks, tab_blocks), 0)

    def col(wd, j):
        return pl.BlockSpec((tm, wd), lambda i: (i, j))

    def tab(wd):
        return pl.BlockSpec((3, tm, wd), tix)

    def out(wd):
        return pl.BlockSpec((tm, wd), lambda i: (i, 0))

    group = DSA_HEADS // DSA_KV_HEADS
    return pl.pallas_call(
        _even_prep_kernel,
        grid=(n // tm,),
        in_specs=[col(256, 0), col(256, 1), col(512, 3), col(128, 16), col(256, 9), col(128, 20),
                  col(128, 17), tab(256), tab(256), tab(512), tab(128)],
        out_specs=[out(256), out(256), out(512), out(128), out(256), out(128),
                   pl.BlockSpec((DSA_KV_HEADS, group * tm, DSA_HD), lambda i: (0, i, 0)),
                   pl.BlockSpec((IDX_HEADS * tm, IDX_DIM), lambda i: (i, 0)),
                   pl.BlockSpec((DSA_KV_HEADS, tm, DSA_HD), lambda i: (0, i, 0)),
                   pl.BlockSpec((tm, IDX_DIM), lambda i: (i, 0)),
                   pl.BlockSpec((tm // DSA_TK, LANE, DSA_TK), lambda i: (i, 0, 0)),
                   pl.BlockSpec((8, tm), lambda i: (0, i))],
        out_shape=[jax.ShapeDtypeStruct((n, 256), F32), jax.ShapeDtypeStruct((n, 256), F32),
                   jax.ShapeDtypeStruct((n, 512), BF16), jax.ShapeDtypeStruct((n, 128), F32),
                   jax.ShapeDtypeStruct((n, 256), BF16), jax.ShapeDtypeStruct((n, 128), F32),
                   jax.ShapeDtypeStruct((DSA_KV_HEADS, group * n, DSA_HD), BF16),
                   jax.ShapeDtypeStruct((IDX_HEADS * n, IDX_DIM), BF16),
                   jax.ShapeDtypeStruct((DSA_KV_HEADS, n, DSA_HD), BF16),
                   jax.ShapeDtypeStruct((n, IDX_DIM), BF16),
                   jax.ShapeDtypeStruct((n // DSA_TK, LANE, DSA_TK), BF16),
                   jax.ShapeDtypeStruct((8, n), F32)],
        compiler_params=_cparams(1),
        name="even_rope",
    )(pe, pe, pe, pe, pe, pe, pe, tq, tk, td, ti)


def _ret_kernel(gch_ref, q_ref, k_ref, v_ref, g_ref, dm_ref, qd_ref, kd_ref, gng_ref, gnb_ref,
                s0_ref, *rest):
    o_ref, s_ref, s_sc = rest[-3], rest[-2], rest[-1]
    c = pl.program_id(1)

    @pl.when(c == 0)
    def _():
        s_sc[...] = s0_ref[0]

    q = q_ref[...]
    k = k_ref[...]
    qx = (q * qd_ref[...]).astype(BF16)
    kw = (k * kd_ref[...]).astype(BF16)
    qb = q.astype(BF16)
    kb = k.astype(BF16)
    vb = v_ref[...].astype(BF16)
    gate = g_ref[...]
    for h in range(RET_HEADS):
        ks = slice(h * RET_DK, (h + 1) * RET_DK)
        vs = slice(h * RET_DV, (h + 1) * RET_DV)
        att = _dot_nt(qb[:, ks], kb[:, ks]) * dm_ref[h]
        s_old = s_sc[h]
        y = _dot(att.astype(BF16), vb[:, vs]) + _dot(qx[:, ks], s_old.astype(BF16))
        s_sc[h] = s_old * gch_ref[h] + _dot_tn(kw[:, ks], vb[:, vs])
        yn = _layer_norm(y, gng_ref[:, vs], gnb_ref[:, vs])
        o_ref[:, vs] = _silu(gate[:, vs]) * yn

    @pl.when(c == pl.num_programs(1) - 1)
    def _():
        s_ref[0] = s_sc[...]


def _retention(qa, ka, pe, s0, gn_g, gn_b, n_seq, t, row0, prev_out):
    n = qa.shape[0]
    lc = min(RET_LC, t)
    nc = t // lc
    blk0 = row0 // lc
    log_g = jnp.log(1.0 - 2.0 ** (-5.0 - jnp.arange(RET_HEADS, dtype=F32)))
    pos = jnp.arange(lc, dtype=F32)
    diff = pos[:, None] - pos[None, :]
    dmask = jnp.where(diff >= 0, jnp.exp(jnp.maximum(diff, 0.0)[None] * log_g[:, None, None]), 0.0)
    w_end = jnp.exp((lc - 1 - pos)[:, None] * log_g[None, :])
    xi = jnp.exp((pos + 1.0)[:, None] * log_g[None, :])
    kdec = jnp.repeat(w_end, RET_DK, axis=1)
    qdec = jnp.repeat(xi, RET_DK, axis=1)
    gch = jnp.exp(lc * log_g)

    def rows(wd, j):
        return pl.BlockSpec((lc, wd), lambda b, c: (blk0 + b * nc + c, j))

    def const(shape):
        nd = len(shape)
        return pl.BlockSpec(shape, lambda b, c: (0,) * nd)

    in_specs = [pl.BlockSpec(memory_space=pltpu.SMEM),
                rows(256, 0), rows(256, 0), rows(512, 1), rows(512, 2),
                const((RET_HEADS, lc, lc)), const((lc, 256)), const((lc, 256)),
                const((1, 512)), const((1, 512)),
                pl.BlockSpec((1, RET_HEADS, RET_DK, RET_DV), lambda b, c: (b, 0, 0, 0))]
    args = [gch, qa, ka, pe, pe, dmask, qdec, kdec,
            gn_g.reshape(1, 512), gn_b.reshape(1, 512), s0]
    aliases = {}
    if prev_out is not None:
        in_specs.append(pl.BlockSpec(memory_space=pl.ANY))
        args.append(prev_out)
        aliases = {len(args) - 1: 0}
    return pl.pallas_call(
        _ret_kernel,
        grid=(n_seq, nc),
        in_specs=in_specs,
        out_specs=[pl.BlockSpec((lc, 512), lambda b, c: (blk0 + b * nc + c, 0)),
                   pl.BlockSpec((1, RET_HEADS, RET_DK, RET_DV), lambda b, c: (b, 0, 0, 0))],
        out_shape=[jax.ShapeDtypeStruct((n, 512), F32),
                   jax.ShapeDtypeStruct((n_seq, RET_HEADS, RET_DK, RET_DV), F32)],
        scratch_shapes=[pltpu.VMEM((RET_HEADS, RET_DK, RET_DV), F32)],
        input_output_aliases=aliases,
        compiler_params=_cparams(2),
        name="retention",
    )(*args)


def _col_reduce(x, op):
    r, c = x.shape
    return op(op(x.reshape(r // 8, 8, c), axis=0), axis=0, keepdims=True)


def _dsa_kernel(nkb_ref, q_ref, iq_ref, iwt_ref, qlim_ref, k_ref, vt_ref, ik_ref, *rest,
                topk, tq_out):
    o_ref, key_sc, hi_sc, m_sc, l_sc, acc_sc, lga_sc, lgb_sc = rest[-8:]
    nkb = nkb_ref[pl.program_id(1)]
    tq = qlim_ref.shape[1]
    tk = key_sc.shape[1]
    group = DSA_HEADS // DSA_KV_HEADS
    qlim = qlim_ref[...]
    iwt = iwt_ref[...]
    iqs = iq_ref[...]
    idx_scale = (IDX_HEADS * IDX_DIM) ** -0.5
    krow = lax.broadcasted_iota(I32, (tk, tq), 0)

    def score_body(kb, carry):
        off = pl.multiple_of(kb * tk, tk)
        s_all = _dot_nt(ik_ref[pl.ds(off, tk), :], iqs)
        s = jnp.zeros((tk, tq), F32)
        for h in range(IDX_HEADS):
            s = s + iwt[h:h + 1, :] * jnp.maximum(s_all[:, h * tq:(h + 1) * tq], 0.0)
        s = s * idx_scale
        s = jnp.where(s == 0.0, 0.0, s)
        bits = pltpu.bitcast(s, I32)
        key = jnp.where(bits >= 0, bits, bits ^ jnp.int32(0x7FFFFFFF))
        adm = (off + krow) < qlim
        key_sc[kb] = jnp.where(adm, key, jnp.int32(INT_MIN))
        top = pltpu.bitcast(bits & jnp.int32(-65536), F32)
        hi_sc[kb] = jnp.where(adm, top, -jnp.inf).astype(BF16)
        return carry

    n_pair = (nkb + 1) // 2
    last = 2 * n_pair - 1

    def score_pair(j, carry):
        return score_body(2 * j + 1, score_body(2 * j, carry))

    lax.fori_loop(0, n_pair, score_pair, 0)

    def count(pred):
        def body(kb, acc):
            hit = jnp.where(pred(key_sc[kb]), 1.0, 0.0)
            return acc + jnp.sum(hit.reshape(tk // 64, 64, tq), axis=0)
        acc = lax.fori_loop(0, nkb, body, jnp.zeros((64, tq), F32))
        return jnp.sum(acc, axis=0, keepdims=True)

    def count_top(c):
        one, zero = jnp.ones((), BF16), jnp.zeros((), BF16)

        def body(kb, acc):
            hit = jnp.where(hi_sc[kb] >= c, one, zero).reshape(tk // 64, 64, tq)
            part = hit[0]
            for r in range(1, tk // 64):
                part = part + hit[r]
            return acc + part
        acc = lax.fori_loop(0, nkb, body, jnp.zeros((64, tq), BF16))
        return jnp.sum(acc.astype(F32), axis=0, keepdims=True)

    def bit_body_top(it, ans):
        cand = ans + (jnp.int32(1) << (31 - it))
        cbits = jnp.where(cand >= 0, cand, cand ^ jnp.int32(0x7FFFFFFF)) & jnp.int32(-65536)
        min_normal = jnp.int32(0x00800000)
        cbits = jnp.where(jnp.logical_and(cbits > 0, cbits < min_normal), min_normal, cbits)
        c = pltpu.bitcast(cbits, F32).astype(BF16)
        return jnp.where(count_top(c) >= topk, cand, ans)

    def bit_body(it, ans):
        cand = ans + (jnp.int32(1) << (31 - it))
        return jnp.where(count(lambda k: k >= cand) >= topk, cand, ans)

    t = lax.fori_loop(0, 16, bit_body_top, jnp.full((1, tq), INT_MIN, I32))
    t = lax.fori_loop(16, 32, bit_body, t)
    need = topk - count(lambda k: k > t)

    m_sc[...] = jnp.full(m_sc.shape, 0.1 * NEG_BIG, F32)
    l_sc[...] = jnp.zeros(l_sc.shape, F32)
    acc_sc[...] = jnp.zeros(acc_sc.shape, F32)
    r_i = lax.broadcasted_iota(I32, (tk, tk), 0)
    c_i = lax.broadcasted_iota(I32, (tk, tk), 1)
    lower = (c_i < r_i).astype(BF16)

    def logits_stage(kb, n_eq, dst):
        off = pl.multiple_of(kb * tk, tk)
        key = key_sc[kb]
        adm = (off + krow) < qlim
        eq = jnp.logical_and(key == t, adm)
        eqf = jnp.where(eq, 1.0, 0.0)
        pref = _dot(lower, eqf.astype(BF16))
        sel = jnp.logical_and(adm, jnp.logical_or(
            key > t, jnp.logical_and(eq, (n_eq + pref) < need)))
        bias = jnp.where(sel, 0.0, NEG_BIG)
        for n in range(DSA_KV_HEADS):
            lg_all = _dot_nt(k_ref[n, pl.ds(off, tk), :], q_ref[n])
            for g in range(group):
                ls = slice(g * tq, (g + 1) * tq)
                dst[n, :, ls] = lg_all[:, ls] + bias
        return n_eq + _col_reduce(eqf, jnp.sum)

    def softmax_stage(kb, src):
        vt = vt_ref[kb]
        for n in range(DSA_KV_HEADS):
            ps, alphas = [], []
            for g in range(group):
                ls = slice(g * tq, (g + 1) * tq)
                lg = src[n, :, ls]
                m_old = m_sc[n, :, ls]
                m_new = jnp.maximum(m_old, _col_reduce(lg, jnp.max))
                p = jnp.exp2(lg - m_new)
                alpha = jnp.exp2(m_old - m_new)
                l_sc[n, :, ls] = alpha * l_sc[n, :, ls] + _col_reduce(p, jnp.sum)
                m_sc[n, :, ls] = m_new
                ps.append(p.astype(BF16))
                alphas.append(alpha)
            p_all = jnp.concatenate(ps, axis=1)
            alpha_all = jnp.concatenate(alphas, axis=1)
            pv = _dot(vt[n * DSA_HD:(n + 1) * DSA_HD, :], p_all)
            acc_sc[n] = alpha_all * acc_sc[n] + pv

    def pair_body(j, n_eq):
        kb0 = 2 * j
        softmax_stage(kb0, lga_sc)
        n_eq = logits_stage(kb0 + 1, n_eq, lgb_sc)
        softmax_stage(kb0 + 1, lgb_sc)
        return logits_stage(jnp.minimum(kb0 + 2, last), n_eq, lga_sc)

    lax.fori_loop(0, n_pair, pair_body, logits_stage(0, jnp.zeros((1, tq), F32), lga_sc))
    pieces = []
    for n in range(DSA_KV_HEADS):
        o_n = acc_sc[n] / l_sc[n]
        for g in range(group):
            pieces.append(o_n[:, g * tq:(g + 1) * tq])
    o_ref[...] = jnp.concatenate(pieces, axis=0).T[:tq_out, :]


def _dsa(q_st, iq_st, iw_t, qlim, nkb, k_hm, v_t, ik_bf, n_seq, nq, s_len, tq_out, row0, n_out,
         topk, prev_out):
    tq = DSA_TQ
    group = DSA_HEADS // DSA_KV_HEADS
    blk0 = row0 // tq_out
    in_specs = [pl.BlockSpec((DSA_KV_HEADS, group * tq, DSA_HD), lambda b, i, s: (0, b * nq + i, 0)),
                pl.BlockSpec((IDX_HEADS * tq, IDX_DIM), lambda b, i, s: (b * nq + i, 0)),
                pl.BlockSpec((8, tq), lambda b, i, s: (0, b * nq + i)),
                pl.BlockSpec((1, tq), lambda b, i, s: (0, i)),
                pl.BlockSpec((DSA_KV_HEADS, s_len, DSA_HD), lambda b, i, s: (0, b, 0)),
                pl.BlockSpec((s_len // DSA_TK, LANE, DSA_TK), lambda b, i, s: (b, 0, 0)),
                pl.BlockSpec((s_len, IDX_DIM), lambda b, i, s: (b, 0))]
    args = [nkb, q_st, iq_st, iw_t, qlim, k_hm, v_t, ik_bf]
    aliases = {}
    if prev_out is not None:
        in_specs.append(pl.BlockSpec(memory_space=pl.ANY))
        args.append(prev_out)
        aliases = {len(args) - 1: 0}
    grid_spec = pltpu.PrefetchScalarGridSpec(
        num_scalar_prefetch=1,
        grid=(n_seq, nq),
        in_specs=in_specs,
        out_specs=pl.BlockSpec((tq_out, 512), lambda b, i, s: (blk0 + b * nq + i, 0)),
        scratch_shapes=[pltpu.VMEM((s_len // DSA_TK, DSA_TK, tq), I32),
                        pltpu.VMEM((s_len // DSA_TK, DSA_TK, tq), BF16),
                        pltpu.VMEM((DSA_KV_HEADS, 1, group * tq), F32),
                        pltpu.VMEM((DSA_KV_HEADS, 1, group * tq), F32),
                        pltpu.VMEM((DSA_KV_HEADS, DSA_HD, group * tq), F32),
                        pltpu.VMEM((DSA_KV_HEADS, DSA_TK, group * tq), F32),
                        pltpu.VMEM((DSA_KV_HEADS, DSA_TK, group * tq), F32)])
    return pl.pallas_call(
        functools.partial(_dsa_kernel, topk=topk, tq_out=tq_out),
        grid_spec=grid_spec,
        out_shape=jax.ShapeDtypeStruct((n_out, 512), F32),
        input_output_aliases=aliases,
        compiler_params=_cparams(2),
        name="dsa_attention",
    )(*args)


def _outproj_ln_kernel(ya_ref, yb_ref, w_ref, h_ref, g_ref, b_ref, o_ref, op_ref):
    half = ya_ref.shape[1]
    y = (_dot(ya_ref[...].astype(BF16), w_ref[:half, :])
         + _dot(yb_ref[...].astype(BF16), w_ref[half:, :]))
    out = _layer_norm(DN_ALPHA * h_ref[...] + y, g_ref[...], b_ref[...])
    o_ref[...] = out
    op_ref[...] = _pack_pairs(out)


def _outproj_ln(ya, yb, w, h, g, b, tm):
    n = h.shape[0]
    return pl.pallas_call(
        _outproj_ln_kernel,
        grid=(n // tm,),
        in_specs=[pl.BlockSpec((tm, 512), lambda i: (i, 0)),
                  pl.BlockSpec((tm, 512), lambda i: (i, 0)),
                  pl.BlockSpec((D_MODEL, D_MODEL), lambda i: (0, 0)),
                  pl.BlockSpec((tm, D_MODEL), lambda i: (i, 0)),
                  pl.BlockSpec((1, D_MODEL), lambda i: (0, 0)),
                  pl.BlockSpec((1, D_MODEL), lambda i: (0, 0))],
        out_specs=[pl.BlockSpec((tm, D_MODEL), lambda i: (i, 0)),
                   pl.BlockSpec((tm, D_MODEL // 2), lambda i: (i, 0))],
        out_shape=[jax.ShapeDtypeStruct((n, D_MODEL), F32),
                   jax.ShapeDtypeStruct((n, D_MODEL // 2), U32)],
        compiler_params=_cparams(1),
        name="out_proj_ln",
    )(ya, yb, w, h, g.reshape(1, -1), b.reshape(1, -1))


def _router_kernel(x_ref, w_ref, b_ref, idx_o, gate_o, rank_o, cnt_o, cnt_sc):
    i = pl.program_id(0)

    @pl.when(i == 0)
    def _():
        cnt_sc[...] = jnp.zeros(cnt_sc.shape, F32)

    tm = x_ref.shape[0]
    ne = w_ref.shape[0]
    x = x_ref[...]
    w = w_ref[...]
    x_hi = x.astype(BF16)
    x_lo = (x - x_hi.astype(F32)).astype(BF16)
    w_hi = w.astype(BF16)
    w_lo = (w - w_hi.astype(F32)).astype(BF16)
    logits = (_dot_nt(w_hi, x_hi) + _dot_nt(w_lo, x_hi) + _dot_nt(w_hi, x_lo)) + b_ref[...]
    erow = lax.broadcasted_iota(I32, (ne, tm), 0)
    vals, idxs = [], []
    onehot = jnp.zeros((ne, tm), F32)
    for _ in range(TOP_K):
        m = jnp.max(logits, axis=0, keepdims=True)
        ix = jnp.min(jnp.where(logits == m, erow, ne), axis=0, keepdims=True)
        hit = erow == ix
        onehot = jnp.where(hit, 1.0, onehot)
        logits = jnp.where(hit, -jnp.inf, logits)
        vals.append(m)
        idxs.append(ix)
    es = [jnp.exp(v - vals[0]) for v in vals]
    den = es[0] + es[1] + es[2] + es[3]
    r_i = lax.broadcasted_iota(I32, (tm, tm), 0)
    c_i = lax.broadcasted_iota(I32, (tm, tm), 1)
    upper = (r_i < c_i).astype(BF16)
    rank_dense = _dot(onehot.astype(BF16), upper) + cnt_sc[...]
    prow = lax.broadcasted_iota(I32, (8, tm), 0)
    idx_out = jnp.zeros((8, tm), I32)
    gate_out = jnp.zeros((8, tm), F32)
    rank_out = jnp.zeros((8, tm), F32)
    for k in range(TOP_K):
        rk = jnp.sum(jnp.where(erow == idxs[k], rank_dense, 0.0), axis=0, keepdims=True)
        idx_out = jnp.where(prow == k, idxs[k], idx_out)
        gate_out = jnp.where(prow == k, es[k] / den, gate_out)
        rank_out = jnp.where(prow == k, rk, rank_out)
    idx_o[...] = idx_out
    gate_o[...] = gate_out
    rank_o[...] = rank_out.astype(I32)
    cnt = cnt_sc[...] + jnp.sum(onehot, axis=1, keepdims=True)
    cnt_sc[...] = cnt
    cnt_o[...] = cnt


def _router(x, w_r, b_r, tm):
    n = x.shape[0]
    row = pl.BlockSpec((8, tm), lambda i: (0, i))
    return pl.pallas_call(
        _router_kernel,
        grid=(n // tm,),
        in_specs=[pl.BlockSpec((tm, D_MODEL), lambda i: (i, 0)),
                  pl.BlockSpec((N_EXPERTS, D_MODEL), lambda i: (0, 0)),
                  pl.BlockSpec((N_EXPERTS, 1), lambda i: (0, 0))],
        out_specs=[row, row, row, pl.BlockSpec((N_EXPERTS, 1), lambda i: (0, 0))],
        out_shape=[jax.ShapeDtypeStruct((8, n), I32), jax.ShapeDtypeStruct((8, n), F32),
                   jax.ShapeDtypeStruct((8, n), I32), jax.ShapeDtypeStruct((N_EXPERTS, 1), F32)],
        scratch_shapes=[pltpu.VMEM((N_EXPERTS, 1), F32)],
        compiler_params=_cparams(1),
        name="moe_router",
    )(x, w_r.T, b_r.reshape(N_EXPERTS, 1))


def _moe_kernel(be_ref, nu_ref, *rest, n_parts, blocks_per_part):
    x_refs = rest[:n_parts]
    wgu_ref, bgu_ref, wdn_ref, bdn_ref, o_ref, wgu_sc, wdn_sc = rest[n_parts:]
    i = pl.program_id(0)

    @pl.when(jnp.logical_or(i == 0, be_ref[i] != be_ref[jnp.maximum(i - 1, 0)]))
    def _():
        wgu_sc[...] = wgu_ref[0, 0].astype(BF16)
        wdn_sc[...] = wdn_ref[0, 0].astype(BF16)

    @pl.when(i < nu_ref[0])
    def _():
        part = i // blocks_per_part
        xw = x_refs[0][...]
        for c in range(1, n_parts):
            xw = jnp.where(part == c, x_refs[c][...], xw)
        xa, xb = _unpack_pairs(xw)
        x = jnp.concatenate([xa.astype(BF16), xb.astype(BF16)], axis=1)
        h = _dot(x, wgu_sc[...]) + bgu_ref[0, 0]
        g = jnp.minimum(h[:, :D_FF], SWIGLU_LIMIT)
        up = jnp.clip(h[:, D_FF:], -SWIGLU_LIMIT, SWIGLU_LIMIT)
        a = (up + 1.0) * g * jax.nn.sigmoid(SWIGLU_ALPHA * g)
        o_ref[...] = _pack_pairs(_dot(a.astype(BF16), wdn_sc[...]) + bdn_ref[0, 0])

    @pl.when(i >= nu_ref[0])
    def _():
        o_ref[...] = jnp.zeros(o_ref.shape, U32)


def _moe_experts(xs_parts, blk_e, n_used, layer, w_gu, b_gu, w_dn, b_dn):
    n_parts = len(xs_parts)
    tm = MOE_TM
    bpp = xs_parts[0].shape[0] // tm
    n_rows = n_parts * bpp * tm
    depth = w_gu.shape[0]

    def x_spec(c):
        return pl.BlockSpec((tm, D_MODEL // 2),
                            lambda i, be, nu: (jnp.clip(i - c * bpp, 0, bpp - 1), 0))
    grid_spec = pltpu.PrefetchScalarGridSpec(
        num_scalar_prefetch=2,
        grid=(n_rows // tm,),
        in_specs=[x_spec(c) for c in range(n_parts)] + [
                  pl.BlockSpec((1, 1, D_MODEL, 2 * D_FF), lambda i, be, nu: (layer, be[i], 0, 0)),
                  pl.BlockSpec((1, 1, 1, 2 * D_FF), lambda i, be, nu: (layer, be[i], 0, 0)),
                  pl.BlockSpec((1, 1, D_FF, D_MODEL), lambda i, be, nu: (layer, be[i], 0, 0)),
                  pl.BlockSpec((1, 1, 1, D_MODEL), lambda i, be, nu: (layer, be[i], 0, 0))],
        out_specs=pl.BlockSpec((tm, D_MODEL // 2), lambda i, be, nu: (i, 0)),
        scratch_shapes=[pltpu.VMEM((D_MODEL, 2 * D_FF), BF16), pltpu.VMEM((D_FF, D_MODEL), BF16)])
    return pl.pallas_call(
        functools.partial(_moe_kernel, n_parts=n_parts, blocks_per_part=bpp),
        grid_spec=grid_spec,
        out_shape=jax.ShapeDtypeStruct((n_rows, D_MODEL // 2), U32),
        compiler_params=_cparams(1),
        name="moe_experts",
    )(blk_e, n_used, *xs_parts, w_gu, b_gu.reshape(depth, N_EXPERTS, 1, -1), w_dn,
      b_dn.reshape(depth, N_EXPERTS, 1, -1))


def _combine_ln_kernel(h_ref, y0_ref, y1_ref, y2_ref, y3_ref, gate_ref, g_ref, b_ref, *o_refs,
                       n_first):
    gate = gate_ref[...]
    ya, yb = None, None
    for k, y_ref in enumerate((y0_ref, y1_ref, y2_ref, y3_ref)):
        a, b = _unpack_pairs(y_ref[...])
        gk = gate[:, k:k + 1]
        ya = gk * a if ya is None else ya + gk * a
        yb = gk * b if yb is None else yb + gk * b
    y = jnp.concatenate([ya, yb], axis=1)
    out = _layer_norm(DN_ALPHA * h_ref[...] + y, g_ref[...], b_ref[...])
    if len(o_refs) == 1:
        o_refs[0][...] = out
    else:
        @pl.when(pl.program_id(0) < n_first)
        def _():
            o_refs[0][...] = out

        @pl.when(pl.program_id(0) >= n_first)
        def _():
            o_refs[1][...] = out


def _combine_ln(h, ys, gate, g, b, tm, split_rows=None):
    n = h.shape[0]
    row = pl.BlockSpec((tm, D_MODEL), lambda i: (i, 0))
    half = pl.BlockSpec((tm, D_MODEL // 2), lambda i: (i, 0))
    vec = pl.BlockSpec((1, D_MODEL), lambda i: (0, 0))
    if split_rows is None:
        n_first = n // tm
        out_specs = row
        out_shape = jax.ShapeDtypeStruct((n, D_MODEL), F32)
    else:
        n_first = split_rows // tm
        out_specs = [pl.BlockSpec((tm, D_MODEL), lambda i: (jnp.minimum(i, n_first - 1), 0)),
                     pl.BlockSpec((tm, D_MODEL), lambda i: (jnp.maximum(i - n_first, 0), 0))]
        out_shape = [jax.ShapeDtypeStruct((split_rows, D_MODEL), F32),
                     jax.ShapeDtypeStruct((n - split_rows, D_MODEL), F32)]
    return pl.pallas_call(
        functools.partial(_combine_ln_kernel, n_first=n_first),
        grid=(n // tm,),
        in_specs=[row, half, half, half, half, pl.BlockSpec((tm, TOP_K), lambda i: (i, 0)), vec, vec],
        out_specs=out_specs,
        out_shape=out_shape,
        compiler_params=_cparams(1),
        name="moe_combine_ln",
    )(h, ys[0], ys[1], ys[2], ys[3], gate, g.reshape(1, -1), b.reshape(1, -1))


def _rows(x, idx):
    return x.at[idx].get(mode="promise_in_bounds")


def _moe_layer(h, h_packed, layer, w_r, b_r, w_gu, b_gu, w_dn, b_dn, ln_g, ln_b, tm, split_rows=None):
    n = h.shape[0]
    n_pair = n * TOP_K
    idx_t, gate_t, rank_t, cnt = _router(h, w_r, b_r, tm)
    top_i = idx_t[:TOP_K].T
    counts = cnt[:, 0].astype(I32)
    padded = (counts + MOE_TM - 1) // MOE_TM * MOE_TM
    pad_end = jnp.cumsum(padded)
    start = pad_end - padded
    first = jnp.cumsum(counts) - counts
    dest = _rows(start, top_i) + rank_t[:TOP_K].T
    n_blk = -(-(-(-n_pair // MOE_TM) + N_EXPERTS) // MOE_PARTS) * MOE_PARTS
    n_used = (pad_end[-1] // MOE_TM).astype(I32)
    blk_row = jnp.minimum(jnp.arange(n_blk, dtype=I32), n_used - 1) * MOE_TM
    blk_e = jnp.sum((pad_end[None, :] <= blk_row[:, None]).astype(I32), axis=1)
    blk_e = jnp.minimum(blk_e, N_EXPERTS - 1)
    order = jnp.argsort(top_i.reshape(-1), stable=True).astype(I32)
    row_in_e = (jnp.arange(n_blk, dtype=I32)[:, None] * MOE_TM - _rows(start, blk_e)[:, None]
                + jnp.arange(MOE_TM, dtype=I32)[None, :])
    pair = jnp.clip(_rows(first, blk_e)[:, None] + row_in_e, 0, n_pair - 1).reshape(-1)
    src = (_rows(order, pair) // TOP_K).reshape(MOE_PARTS, -1)
    xs_parts = [_rows(h_packed, src[c]) for c in range(MOE_PARTS)]
    ybuf = _moe_experts(xs_parts, blk_e, n_used.reshape(1), layer, w_gu, b_gu, w_dn, b_dn)
    ys = [_rows(ybuf, dest[:, k]) for k in range(TOP_K)]
    return _combine_ln(h, ys, gate_t[:TOP_K].T, ln_g, ln_b, tm, split_rows)


def _band_kernel(q_ref, *rest, nkb):
    k_refs = rest[:nkb]
    v_refs = rest[nkb:2 * nkb]
    bias_ref = rest[2 * nkb]
    o_ref = rest[-1]
    c = pl.program_id(1)
    tq = q_ref.shape[0]
    tkb = k_refs[0].shape[0]
    q_scale = BAND_HD ** -0.5 * math.log2(math.e)
    kbs = [r[...].astype(BF16) for r in k_refs]
    vbs = [r[...].astype(BF16) for r in v_refs]
    off = [jnp.where((c + j - (nkb - 1)) >= 0, 0.0, NEG_BIG) for j in range(nkb)]
    low = lax.broadcasted_iota(I32, (tq, LANE), 1) < BAND_HD
    for pair in range(BAND_HEADS // 2):
        ls = slice(pair * LANE, (pair + 1) * LANE)
        qp = q_ref[:, ls] * q_scale
        outs = []
        for par in range(2):
            h = 2 * pair + par
            qh = jnp.where(low == (par == 0), qp, 0.0).astype(BF16)
            lgs = [_dot_nt(qh, kbs[j][:, ls]) + (bias_ref[h, :, j * tkb:(j + 1) * tkb] + off[j])
                   for j in range(nkb)]
            mx = lgs[0]
            for j in range(1, nkb):
                mx = jnp.maximum(mx, lgs[j])
            m = mx.max(axis=1, keepdims=True)
            ps = [jnp.exp2(lg - m) for lg in lgs]
            sm = ps[0]
            for j in range(1, nkb):
                sm = sm + ps[j]
            den = sm.sum(axis=1, keepdims=True)
            acc = _dot(ps[0].astype(BF16), vbs[0][:, ls])
            for j in range(1, nkb):
                acc = acc + _dot(ps[j].astype(BF16), vbs[j][:, ls])
            outs.append(acc * (1.0 / den))
        o_ref[:, ls] = jnp.where(low, outs[0], outs[1])


def _band(q_arr, k_arr, v_arr, cols, bias, n_seq, t, tq, tkb, nkb, q_row0, kv_blocks_per_seq,
          n_out, prev_out):
    nq = t // tq
    qblk0 = q_row0 // tq
    qcol, kcol, vcol = cols

    def kv_spec(j, col):
        def ix(b, c):
            return (b * kv_blocks_per_seq + jnp.maximum(c + j - (nkb - 1), 0), col)
        return pl.BlockSpec((tkb, 512), ix)

    in_specs = ([pl.BlockSpec((tq, 512), lambda b, c: (qblk0 + b * nq + c, qcol))]
                + [kv_spec(j, kcol) for j in range(nkb)]
                + [kv_spec(j, vcol) for j in range(nkb)]
                + [pl.BlockSpec(bias.shape, lambda b, c: (0, 0, 0))])
    args = [q_arr] + [k_arr] * nkb + [v_arr] * nkb + [bias]
    aliases = {}
    if prev_out is not None:
        in_specs.append(pl.BlockSpec(memory_space=pl.ANY))
        args.append(prev_out)
        aliases = {len(args) - 1: 0}
    return pl.pallas_call(
        functools.partial(_band_kernel, nkb=nkb),
        grid=(n_seq, nq),
        in_specs=in_specs,
        out_specs=pl.BlockSpec((tq, 512), lambda b, c: (qblk0 + b * nq + c, 0)),
        out_shape=jax.ShapeDtypeStruct((n_out, 512), F32),
        input_output_aliases=aliases,
        compiler_params=_cparams(2),
        name="band_attention",
    )(*args)


def _band_bias(rel_bias, tq, n_keys, key0):
    n_off = tq + n_keys - 1
    d_max = tq - 1 - key0
    rel = np.clip(d_max - np.arange(n_off), -REL_CLIP, REL_CLIP) + REL_CLIP
    vals = jnp.concatenate([rel_bias[:, rel], jnp.zeros((rel_bias.shape[0], 1), F32)], axis=1)
    rot = jnp.tile(vals, (1, tq))[:, :tq * n_off].reshape(-1, tq, n_off)
    toep = rot[:, :, tq - 1:tq - 1 + n_keys]
    qp = np.arange(tq)[:, None]
    kp = key0 + np.arange(n_keys)[None, :]
    cs = (qp // CHUNK) * CHUNK
    band = np.logical_and(kp >= cs - BAND_PAST, kp < cs + CHUNK)
    return jnp.where(jnp.asarray(band)[None], toep * math.log2(math.e), NEG_BIG).astype(F32)


def _ssd_kernel(dsk_ref, z_ref, xbc_ref, dt_ref, cw_ref, cb_ref, dtb_ref, alog_ref, ng_ref,
                h0_ref, c0_ref, *rest):
    o_ref, h_ref, cl_ref, h_sc, xe_sc, y_sc = rest[-6:]
    c = pl.program_id(1)
    lc = z_ref.shape[0]

    @pl.when(c == 0)
    def _():
        h_sc[...] = h0_ref[0]
        xe_sc[0:8, :] = jnp.zeros((8, xe_sc.shape[1]), F32)
        xe_sc[8 - (SSD_CONV - 1):8, :] = c0_ref[0]

    xe_sc[8:8 + lc, :] = xbc_ref[...]
    conv = cb_ref[...] + cw_ref[SSD_CONV - 1:SSD_CONV, :] * xe_sc[8:8 + lc, :]
    for s in range(1, SSD_CONV):
        conv = conv + cw_ref[SSD_CONV - 1 - s:SSD_CONV - s, :] * xe_sc[8 - s:8 - s + lc, :]
    u = _silu(conv)
    gs = SSD_GROUPS * SSD_STATE
    xs = u[:, :SSD_INNER]
    bm = u[:, SSD_INNER:SSD_INNER + gs].astype(BF16)
    cm = u[:, SSD_INNER + gs:].astype(BF16)
    dx = dt_ref[...] + dtb_ref[...]
    dtv = jnp.maximum(dx, 0.0) + jnp.log1p(jnp.exp(-jnp.abs(dx)))
    a = dtv * (-jnp.exp(alog_ref[...]))
    r_i = lax.broadcasted_iota(I32, (lc, lc), 0)
    c_i = lax.broadcasted_iota(I32, (lc, lc), 1)
    causal = c_i <= r_i
    acum = _dot_f32(causal.astype(F32), a)
    acum_t = acum.T
    hpg = SSD_HEADS // SSD_GROUPS
    for g in range(SSD_GROUPS):
        ss = slice(g * SSD_STATE, (g + 1) * SSD_STATE)
        cb = _dot_nt(cm[:, ss], bm[:, ss])
        for jj in range(hpg):
            j = g * hpg + jj
            ps = slice(j * SSD_HD, (j + 1) * SSD_HD)
            col = acum[:, j:j + 1]
            row = acum_t[j:j + 1, :]
            lmat = jnp.exp(jnp.where(causal, col - row, -jnp.inf))
            x_j = xs[:, ps]
            xdt = x_j * dtv[:, j:j + 1]
            h_old = h_sc[j]
            y = _dot((cb * lmat).astype(BF16), xdt.astype(BF16))
            y = y + _dot_nt(cm[:, ss], h_old.astype(BF16)) * jnp.exp(col)
            y = y + dsk_ref[j] * x_j
            last = acum[lc - 1:lc, j:j + 1]
            st = _dot_tn((xdt * jnp.exp(last - col)).astype(BF16), bm[:, ss])
            h_sc[j] = h_old * jnp.exp(last) + st
            y_sc[:, ps] = y
    yd = y_sc[...] * _silu(z_ref[...])
    gw = SSD_INNER // SSD_GROUPS
    for g in range(SSD_GROUPS):
        ws = slice(g * gw, (g + 1) * gw)
        yg = yd[:, ws]
        ms = jnp.mean(yg * yg, axis=-1, keepdims=True)
        o_ref[:, ws] = yg * lax.rsqrt(ms + LN_EPS) * ng_ref[:, ws]
    xe_sc[0:8, :] = xe_sc[lc:lc + 8, :]

    @pl.when(c == pl.num_programs(1) - 1)
    def _():
        h_ref[0] = h_sc[...]
        cl_ref[0] = xe_sc[8 - (SSD_CONV - 1):8, :]


def _ssd(po, h0, c0, conv_w, conv_b, dt_bias, a_log, d_skip, norm_g, n_seq, t, row0, prev_out):
    n = po.shape[0]
    lc = min(SSD_LC, t)
    nc = t // lc
    blk0 = row0 // lc
    cdim = conv_w.shape[1]

    def rows(wd, j):
        return pl.BlockSpec((lc, wd), lambda b, c: (blk0 + b * nc + c, j))

    def const(shape):
        nd = len(shape)
        return pl.BlockSpec(shape, lambda b, c: (0,) * nd)

    pad8 = lambda v: jnp.zeros((1, LANE), F32).at[0, :SSD_HEADS].set(v)
    in_specs = [pl.BlockSpec(memory_space=pltpu.SMEM),
                rows(512, 3), rows(cdim, 2), rows(LANE, 24),
                const((SSD_CONV, cdim)), const((1, cdim)), const((1, LANE)), const((1, LANE)),
                const((1, SSD_INNER)),
                pl.BlockSpec((1, SSD_HEADS, SSD_HD, SSD_STATE), lambda b, c: (b, 0, 0, 0)),
                pl.BlockSpec((1, SSD_CONV - 1, cdim), lambda b, c: (b, 0, 0))]
    args = [d_skip, po, po, po, conv_w, conv_b.reshape(1, -1), pad8(dt_bias), pad8(a_log),
            norm_g.reshape(1, -1), h0, c0]
    aliases = {}
    if prev_out is not None:
        in_specs.append(pl.BlockSpec(memory_space=pl.ANY))
        args.append(prev_out)
        aliases = {len(args) - 1: 0}
    return pl.pallas_call(
        _ssd_kernel,
        grid=(n_seq, nc),
        in_specs=in_specs,
        out_specs=[pl.BlockSpec((lc, 512), lambda b, c: (blk0 + b * nc + c, 0)),
                   pl.BlockSpec((1, SSD_HEADS, SSD_HD, SSD_STATE), lambda b, c: (b, 0, 0, 0)),
                   pl.BlockSpec((1, SSD_CONV - 1, cdim), lambda b, c: (b, 0, 0))],
        out_shape=[jax.ShapeDtypeStruct((n, 512), F32),
                   jax.ShapeDtypeStruct((n_seq, SSD_HEADS, SSD_HD, SSD_STATE), F32),
                   jax.ShapeDtypeStruct((n_seq, SSD_CONV - 1, cdim), F32)],
        scratch_shapes=[pltpu.VMEM((SSD_HEADS, SSD_HD, SSD_STATE), F32),
                        pltpu.VMEM((lc + 8, cdim), F32),
                        pltpu.VMEM((lc, 512), F32)],
        input_output_aliases=aliases,
        compiler_params=_cparams(2),
        name="ssd_scan",
    )(*args)


def _pad_cols(w, width):
    return jnp.concatenate([w, jnp.zeros((w.shape[0], width - w.shape[1]), w.dtype)], axis=1)


def kernel(x_prompt, x_sample, state_ret, cache_dsa_k, cache_dsa_v, cache_dsa_kidx, cache_band_k, cache_band_v, state_ssm, state_conv, e_w_in, e_w_out, e_gn_g, e_gn_b, o_w_in, o_w_out, o_rel_bias, o_conv_w, o_conv_b, o_dt_bias, o_a_log, o_d_skip, o_norm_g, ln1_g, ln1_b, ln2_g, ln2_b, router_w, router_b, exp_w_gu, exp_b_gu, exp_w_dn, exp_b_dn):
    bp, tp, _ = x_prompt.shape
    bs, ts, _ = x_sample.shape
    past = cache_dsa_k.shape[2]
    n_p, n_s = bp * tp, bs * ts
    n = n_p + n_s
    tm = math.gcd(512, math.gcd(n_p, n_s))
    assert tp % tm == 0 and tm % ts == 0 and ts == CHUNK

    h = jnp.concatenate([x_prompt.reshape(n_p, D_MODEL), x_sample.reshape(n_s, D_MODEL)], axis=0)

    def blank():
        return jnp.zeros((n, 512), F32)

    pe = _proj(h, _pad_cols(e_w_in[0], EVEN_W).astype(BF16), tm)
    pos_p = jnp.arange(tp, dtype=I32)
    pos_s = past + jnp.arange(ts, dtype=I32)
    pos_tab = jnp.concatenate([pos_p, jnp.tile(pos_s, tm // ts)])
    tabs = (_rope_tables(pos_tab, RET_HEADS, RET_DK, RET_DK, RET_THETA),
            _rope_tables(pos_tab, RET_HEADS, RET_DK, RET_DK, RET_THETA, scale=RET_DK ** -0.5),
            _rope_tables(pos_tab, DSA_HEADS, DSA_HD, DSA_ROT, ROPE_THETA),
            _rope_tables(pos_tab, 1, IDX_DIM, DSA_ROT, ROPE_THETA, pad_to=LANE))
    (qa, ka, qb, kb, iq, ikw, q_st, iq_st, k_hm, ik_bf, v_t, iw_t) = _even_prep(
        pe, tabs, tm, n_p // tm, tp // tm)

    ya, ret_p = _retention(qa, ka, pe, jnp.zeros((bp, RET_HEADS, RET_DK, RET_DV), F32),
                           e_gn_g[0], e_gn_b[0], bp, tp, 0, blank())
    ya, ret_s = _retention(qa, ka, pe, state_ret[0], e_gn_g[0], e_gn_b[0], bs, ts, n_p, ya)

    topk_p = min(DSA_TOPK_MAX, tp // 4)
    qlim_p = (((pos_p // CHUNK) + 1) * CHUNK).reshape(1, tp)
    nq_p = tp // DSA_TQ
    nkb_p = ((jnp.arange(nq_p, dtype=I32) + 1) * DSA_TQ + DSA_TK - 1) // DSA_TK
    yb = _dsa(q_st, iq_st, iw_t, qlim_p, nkb_p, k_hm, v_t, ik_bf, bp, nq_p, tp, DSA_TQ, 0, n,
              topk_p, blank())

    s_len = past + ts
    s_pad = -(-s_len // (2 * DSA_TK)) * (2 * DSA_TK)
    topk_s = min(DSA_TOPK_MAX, s_len // 4)
    group = DSA_HEADS // DSA_KV_HEADS

    def cat_keys(cache, new, wd):
        zpad = jnp.zeros((bs, s_pad - s_len, wd), F32)
        return jnp.concatenate([cache, new.reshape(bs, ts, wd), zpad], axis=1)

    def pad_q(x):
        return jnp.concatenate([x, jnp.zeros((bs, DSA_TQ - ts) + x.shape[2:], x.dtype)], axis=1)

    ks = cat_keys(cache_dsa_k[0].reshape(bs, past, LANE), kb[n_p:], LANE)
    vs = cat_keys(cache_dsa_v[0].reshape(bs, past, LANE), pe[n_p:, 2176:2304], LANE)
    iks = cat_keys(cache_dsa_kidx[0], ikw[n_p:, :IDX_DIM], IDX_DIM)
    k_hm_s = ks.reshape(bs, s_pad, DSA_KV_HEADS, DSA_HD).transpose(2, 0, 1, 3).reshape(
        DSA_KV_HEADS, bs * s_pad, DSA_HD).astype(BF16)
    v_t_s = vs.reshape(bs, s_pad // DSA_TK, DSA_TK, LANE).transpose(0, 1, 3, 2).reshape(
        bs * (s_pad // DSA_TK), LANE, DSA_TK).astype(BF16)
    ik_s = iks.reshape(bs * s_pad, IDX_DIM).astype(BF16)
    q_s = pad_q(qb[n_p:].reshape(bs, ts, DSA_KV_HEADS, group, DSA_HD))
    q_st_s = q_s.transpose(2, 0, 3, 1, 4).reshape(DSA_KV_HEADS, bs * group * DSA_TQ, DSA_HD)
    iq_s = pad_q(iq[n_p:].reshape(bs, ts, IDX_HEADS, IDX_DIM))
    iq_st_s = iq_s.transpose(0, 2, 1, 3).reshape(bs * IDX_HEADS * DSA_TQ, IDX_DIM)
    iw_t_s = pad_q(ikw[n_p:, IDX_DIM:IDX_DIM + 8].reshape(bs, ts, 8)).reshape(bs * DSA_TQ, 8).T
    qlim_s = jnp.full((1, DSA_TQ), s_len, I32)
    nkb_s = jnp.full((1,), -(-s_len // DSA_TK), I32)
    yb = _dsa(q_st_s, iq_st_s, iw_t_s, qlim_s, nkb_s, k_hm_s, v_t_s, ik_s, bs, 1, s_pad, ts, n_p, n,
              topk_s, yb)

    h, h_packed = _outproj_ln(ya, yb, e_w_out[0].astype(BF16), h, ln1_g[0], ln1_b[0], tm)
    h = _moe_layer(h, h_packed, 0, router_w[0], router_b[0], exp_w_gu, exp_b_gu, exp_w_dn, exp_b_dn,
                   ln2_g[0], ln2_b[0], tm)

    po = _proj(h, _pad_cols(o_w_in[0], ODD_W).astype(BF16), tm)
    tq_p = min(BAND_TQ, tp)
    nkb_band = BAND_PAST // tq_p + 1
    bias_p = _band_bias(o_rel_bias[0], tq_p, nkb_band * tq_p, -(nkb_band - 1) * tq_p)
    yc = _band(po, po, po, (0, 1, 2), bias_p, bp, tp, tq_p, tq_p, nkb_band, 0, tp // tq_p, n,
               blank())
    band_len = cache_band_k.shape[2]
    kc_new = po[n_p:, 512:1024].reshape(bs, ts, 512)
    vc_new = po[n_p:, 1024:1536].reshape(bs, ts, 512)
    kcat = jnp.concatenate([cache_band_k[0].reshape(bs, band_len, 512), kc_new], axis=1)
    vcat = jnp.concatenate([cache_band_v[0].reshape(bs, band_len, 512), vc_new], axis=1)
    wlen = band_len + ts
    bias_s = _band_bias(o_rel_bias[0], ts, wlen, -band_len)
    yc = _band(po, kcat.reshape(bs * wlen, 512), vcat.reshape(bs * wlen, 512), (0, 0, 0), bias_s,
               bs, ts, ts, wlen, 1, n_p, 1, n, yc)

    ssd_w = (o_conv_w[0], o_conv_b[0], o_dt_bias[0], o_a_log[0], o_d_skip[0], o_norm_g[0])
    cdim = o_conv_w.shape[2]
    yd, ssm_p, conv_p = _ssd(po, jnp.zeros((bp, SSD_HEADS, SSD_HD, SSD_STATE), F32),
                             jnp.zeros((bp, SSD_CONV - 1, cdim), F32), *ssd_w, bp, tp, 0, blank())
    yd, ssm_s, conv_s = _ssd(po, state_ssm[0], state_conv[0], *ssd_w, bs, ts, n_p, yd)

    h, h_packed = _outproj_ln(yc, yd, o_w_out[0].astype(BF16), h, ln1_g[1], ln1_b[1], tm)
    h_p, h_s = _moe_layer(h, h_packed, 1, router_w[1], router_b[1], exp_w_gu, exp_b_gu, exp_w_dn, exp_b_dn,
                          ln2_g[1], ln2_b[1], tm, split_rows=n_p)

    keep = min(BAND_PAST, tp)
    kd = DSA_KV_HEADS * DSA_HD
    kc_p = po[:n_p, 512:1024].reshape(bp, tp, BAND_HEADS, BAND_HD)[:, -keep:]
    vc_p = po[:n_p, 1024:1536].reshape(bp, tp, BAND_HEADS, BAND_HD)[:, -keep:]
    return (h_p.reshape(bp, tp, D_MODEL), h_s.reshape(bs, ts, D_MODEL),
            ret_p[None],
            kb[:n_p].reshape(1, bp, tp, DSA_KV_HEADS, DSA_HD),
            pe[:n_p, 2176:2176 + kd].reshape(1, bp, tp, DSA_KV_HEADS, DSA_HD),
            ikw[:n_p, :IDX_DIM].reshape(1, bp, tp, IDX_DIM),
            kc_p[None], vc_p[None], ssm_p[None], conv_p[None],
            ret_s[None],
            kb[n_p:].reshape(1, bs, ts, DSA_KV_HEADS, DSA_HD),
            pe[n_p:, 2176:2176 + kd].reshape(1, bs, ts, DSA_KV_HEADS, DSA_HD),
            ikw[n_p:, :IDX_DIM].reshape(1, bs, ts, IDX_DIM),
            kc_new.reshape(1, bs, ts, BAND_HEADS, BAND_HD), vc_new.reshape(1, bs, ts, BAND_HEADS, BAND_HD),
            ssm_s[None], conv_s[None])
```

```python
import functools
import math

import jax
import jax.numpy as jnp
import numpy as np
from jax import lax
from jax.experimental import pallas as pl
from jax.experimental.pallas import tpu as pltpu

F32 = jnp.float32
BF16 = jnp.bfloat16
I32 = jnp.int32
U32 = jnp.uint32

D_MODEL = 1024
CHUNK = 64
RET_HEADS, RET_DK, RET_DV, RET_THETA = 8, 32, 64, 10000.0
DSA_HEADS, DSA_KV_HEADS, DSA_HD = 8, 2, 64
DSA_ROT = DSA_HD // 4
IDX_HEADS, IDX_DIM = 4, 64
DSA_TOPK_MAX = 256
ROPE_THETA = 500000.0
BAND_HEADS, BAND_HD, BAND_PREV = 8, 64, 8
BAND_PAST = BAND_PREV * CHUNK
REL_CLIP = 256
SSD_HEADS, SSD_HD, SSD_GROUPS, SSD_STATE, SSD_CONV = 8, 64, 2, 128, 4
SSD_INNER = SSD_HEADS * SSD_HD
N_EXPERTS, TOP_K, D_FF = 32, 4, 1024
SWIGLU_LIMIT, SWIGLU_ALPHA = 7.0, 1.702
DEPTH = 2
DN_ALPHA = (2 * DEPTH) ** 0.25
LN_EPS = 1e-5

LANE = 128
VMEM_LIMIT = 56 * 1024 * 1024
INT_MIN = -(2 ** 31)
NEG_BIG = -1e30

EVEN_IN = 2628
EVEN_W = 2688
ODD_IN = 3080
ODD_W = 3200

MOE_TM = 512
MOE_PARTS = 4
RET_LC = 256
SEQ_PAR = 1
SSD_LC = 256
DSA_TQ = 128
DSA_TK = 256
BAND_TQ = 256


def _cparams(n_axes):
    return pltpu.CompilerParams(dimension_semantics=("arbitrary",) * n_axes,
                                vmem_limit_bytes=VMEM_LIMIT)


def _dot(a, b):
    return jnp.dot(a, b, preferred_element_type=F32)


def _dot_nt(a, b):
    return lax.dot_general(a, b, (((1,), (1,)), ((), ())), preferred_element_type=F32)


def _dot_tn(a, b):
    return lax.dot_general(a, b, (((0,), (0,)), ((), ())), preferred_element_type=F32)


def _dot_f32(a, b):
    return jnp.dot(a, b, preferred_element_type=F32, precision=lax.Precision.HIGHEST)


def _layer_norm(x, g, b):
    mu = jnp.mean(x, axis=-1, keepdims=True)
    xc = x - mu
    var = jnp.mean(xc * xc, axis=-1, keepdims=True)
    return xc * lax.rsqrt(var + LN_EPS) * g + b


def _silu(x):
    return x * jax.nn.sigmoid(x)


def _pack_pairs(x):
    c = x.shape[1] // 2
    hi = pltpu.bitcast(x[:, :c].astype(jnp.bfloat16).astype(F32), U32)
    lo = pltpu.bitcast(x[:, c:].astype(jnp.bfloat16).astype(F32), U32)
    return hi | (lo >> 16)


def _unpack_pairs(w):
    return (pltpu.bitcast(w & jnp.uint32(0xFFFF0000), F32), pltpu.bitcast(w << 16, F32))


def _row_specs(parts, tm):
    width = parts[0].shape[1]
    if len(parts) == 1:
        return [pl.BlockSpec((tm, width), lambda i: (i, 0))], parts[0].shape[0] // tm
    n_first = parts[0].shape[0] // tm
    return [pl.BlockSpec((tm, width), lambda i: (jnp.minimum(i, n_first - 1), 0)),
            pl.BlockSpec((tm, width), lambda i: (jnp.maximum(i - n_first, 0), 0))], n_first


def _read_rows(refs, n_first):
    if len(refs) == 1:
        return refs[0][...]
    return jnp.where(pl.program_id(0) < n_first, refs[0][...], refs[1][...])


def _proj_kernel(*refs, n_parts, n_first):
    w_ref, o_ref = refs[n_parts:]
    o_ref[...] = _dot(_read_rows(refs[:n_parts], n_first).astype(BF16), w_ref[...])


def _proj(x_parts, w, tm):
    n = sum(p.shape[0] for p in x_parts)
    k, wd = w.shape
    x_specs, n_first = _row_specs(x_parts, tm)
    return pl.pallas_call(
        functools.partial(_proj_kernel, n_parts=len(x_parts), n_first=n_first),
        grid=(n // tm,),
        in_specs=x_specs + [pl.BlockSpec((k, wd), lambda i: (0, 0))],
        out_specs=pl.BlockSpec((tm, wd), lambda i: (i, 0)),
        out_shape=jax.ShapeDtypeStruct((n, wd), F32),
        compiler_params=_cparams(1),
        name="in_proj",
    )(*x_parts, w)


def _rope_tables(pos, n_heads, d, rot, theta, scale=1.0, pad_to=None):
    half = rot // 2
    inv = theta ** (-jnp.arange(half, dtype=F32) / half)
    ang = pos.astype(F32)[:, None] * inv[None, :]
    cos, sin = jnp.cos(ang), jnp.sin(ang)
    p = pos.shape[0]
    one = jnp.ones((p, d - rot), F32)
    zr = jnp.zeros((p, d - rot), F32)
    zh = jnp.zeros((p, half), F32)
    c = jnp.tile(jnp.concatenate([cos, cos, one], 1), (1, n_heads))
    a = jnp.tile(jnp.concatenate([-sin, zh, zr], 1), (1, n_heads))
    b = jnp.tile(jnp.concatenate([zh, sin, zr], 1), (1, n_heads))
    if pad_to is not None and pad_to > n_heads * d:
        extra = pad_to - n_heads * d
        c = jnp.concatenate([c, jnp.ones((p, extra), F32)], 1)
        a = jnp.concatenate([a, jnp.zeros((p, extra), F32)], 1)
        b = jnp.concatenate([b, jnp.zeros((p, extra), F32)], 1)
    return jnp.stack([c, a, b]) * scale


def _rope(x, tab_ref, half):
    w = x.shape[-1]
    return (x * tab_ref[0] + pltpu.roll(x, w - half, 1) * tab_ref[1]
            + pltpu.roll(x, half, 1) * tab_ref[2])


def _even_prep_kernel(qa_ref, ka_ref, qb_ref, kb_ref, iq_ref, ikw_ref, v_ref,
                      tq_ref, tk_ref, td_ref, ti_ref,
                      qa_o, ka_o, qb_o, kb_o, iq_o, ikw_o, qst_o, iqst_o, khm_o, ikb_o, vt_o, iwt_o):
    tm = qa_ref.shape[0]
    h = DSA_ROT // 2
    qa_o[...] = _rope(qa_ref[...], tq_ref, RET_DK // 2)
    ka_o[...] = _rope(ka_ref[...], tk_ref, RET_DK // 2)
    qb = (_rope(qb_ref[...], td_ref, h) * (DSA_HD ** -0.5 * math.log2(math.e))).astype(BF16)
    qb_o[...] = qb
    kb = kb_ref[...]
    kb = (kb * td_ref[0, :, :LANE] + pltpu.roll(kb, LANE - h, 1) * td_ref[1, :, :LANE]
          + pltpu.roll(kb, h, 1) * td_ref[2, :, :LANE])
    kb_o[...] = kb
    iq = iq_ref[...]
    w = iq.shape[-1]
    iq = (iq * td_ref[0, :, :w] + pltpu.roll(iq, w - h, 1) * td_ref[1, :, :w]
          + pltpu.roll(iq, h, 1) * td_ref[2, :, :w]).astype(BF16)
    iq_o[...] = iq
    ikw = _rope(ikw_ref[...], ti_ref, h)
    ikw_o[...] = ikw
    group = DSA_HEADS // DSA_KV_HEADS
    for jb in range(tm // DSA_TQ):
        rs = slice(jb * DSA_TQ, (jb + 1) * DSA_TQ)
        for hd in range(DSA_HEADS):
            n, g = divmod(hd, group)
            ro = (jb * group + g) * DSA_TQ
            qst_o[n, ro:ro + DSA_TQ, :] = qb[rs, hd * DSA_HD:(hd + 1) * DSA_HD]
        for hd in range(IDX_HEADS):
            ro = (jb * IDX_HEADS + hd) * DSA_TQ
            iqst_o[ro:ro + DSA_TQ, :] = iq[rs, hd * IDX_DIM:(hd + 1) * IDX_DIM]
    kbb = kb.astype(BF16)
    for n in range(DSA_KV_HEADS):
        khm_o[n] = kbb[:, n * DSA_HD:(n + 1) * DSA_HD]
    ikb_o[...] = ikw[:, :IDX_DIM].astype(BF16)
    v = v_ref[...]
    for j in range(tm // DSA_TK):
        vt_o[j] = v[j * DSA_TK:(j + 1) * DSA_TK, :].T.astype(BF16)
    iwt_o[...] = ikw.T[IDX_DIM:IDX_DIM + 8, :]


def _even_prep(pe, tabs, tm, n_prompt_blocks, tab_blocks):
    n = pe.shape[0]
    tq, tk, td, ti = tabs

    def tix(i):
        return (0, jnp.where(i < n_prompt_blocks, i % tab_blocks, tab_blocks), 0)

    def col(wd, j):
        return pl.BlockSpec((tm, wd), lambda i: (i, j))

    def tab(wd):
        return pl.BlockSpec((3, tm, wd), tix)

    def out(wd):
        return pl.BlockSpec((tm, wd), lambda i: (i, 0))

    group = DSA_HEADS // DSA_KV_HEADS
    return pl.pallas_call(
        _even_prep_kernel,
        grid=(n // tm,),
        in_specs=[col(256, 0), col(256, 1), col(512, 3), col(128, 16), col(256, 9), col(128, 20),
                  col(128, 17), tab(256), tab(256), tab(512), tab(128)],
        out_specs=[out(256), out(256), out(512), out(128), out(256), out(128),
                   pl.BlockSpec((DSA_KV_HEADS, group * tm, DSA_HD), lambda i: (0, i, 0)),
                   pl.BlockSpec((IDX_HEADS * tm, IDX_DIM), lambda i: (i, 0)),
                   pl.BlockSpec((DSA_KV_HEADS, tm, DSA_HD), lambda i: (0, i, 0)),
                   pl.BlockSpec((tm, IDX_DIM), lambda i: (i, 0)),
                   pl.BlockSpec((tm // DSA_TK, LANE, DSA_TK), lambda i: (i, 0, 0)),
                   pl.BlockSpec((8, tm), lambda i: (0, i))],
        out_shape=[jax.ShapeDtypeStruct((n, 256), F32), jax.ShapeDtypeStruct((n, 256), F32),
                   jax.ShapeDtypeStruct((n, 512), BF16), jax.ShapeDtypeStruct((n, 128), F32),
                   jax.ShapeDtypeStruct((n, 256), BF16), jax.ShapeDtypeStruct((n, 128), F32),
                   jax.ShapeDtypeStruct((DSA_KV_HEADS, group * n, DSA_HD), BF16),
                   jax.ShapeDtypeStruct((IDX_HEADS * n, IDX_DIM), BF16),
                   jax.ShapeDtypeStruct((DSA_KV_HEADS, n, DSA_HD), BF16),
                   jax.ShapeDtypeStruct((n, IDX_DIM), BF16),
                   jax.ShapeDtypeStruct((n // DSA_TK, LANE, DSA_TK), BF16),
                   jax.ShapeDtypeStruct((8, n), F32)],
        compiler_params=_cparams(1),
        name="even_rope",
    )(pe, pe, pe, pe, pe, pe, pe, tq, tk, td, ti)


def _ret_kernel(gch_ref, *refs, n_par):
    seq_refs = [refs[4 * s:4 * s + 4] for s in range(n_par)]
    dm_ref, qd_ref, kd_ref, gng_ref, gnb_ref, avg_ref, s0_ref, o_ref, s_ref, s_sc, y_sc = refs[4 * n_par:]
    c = pl.program_id(1)

    @pl.when(c == 0)
    def _():
        s_sc[...] = s0_ref[...]

    for s, (q_ref, k_ref, v_ref, g_ref) in enumerate(seq_refs):
        q = q_ref[...]
        k = k_ref[...]
        qx = (q * qd_ref[...]).astype(BF16)
        kw = (k * kd_ref[...]).astype(BF16)
        qb = q.astype(BF16)
        kb = k.astype(BF16)
        vb = v_ref[...].astype(BF16)
        gate = g_ref[...]
        for h in range(RET_HEADS):
            ks = slice(h * RET_DK, (h + 1) * RET_DK)
            vs = slice(h * RET_DV, (h + 1) * RET_DV)
            att = _dot_nt(qb[:, ks], kb[:, ks]) * dm_ref[h]
            s_old = s_sc[s, h]
            y_sc[:, vs] = _dot(att.astype(BF16), vb[:, vs]) + _dot(qx[:, ks], s_old.astype(BF16))
            s_sc[s, h] = s_old * gch_ref[h] + _dot_tn(kw[:, ks], vb[:, vs])
        y = y_sc[...]
        mu = _dot(y.astype(BF16), avg_ref[...])
        d = y - mu
        var = _dot((d * d).astype(BF16), avg_ref[...])
        yn = d * lax.rsqrt(var + LN_EPS) * gng_ref[...] + gnb_ref[...]
        o_ref[0, s] = _silu(gate) * yn

    @pl.when(c == pl.num_programs(1) - 1)
    def _():
        s_ref[...] = s_sc[...]


def _retention(qa, ka, pe, s0, gn_g, gn_b, n_seq, t, row0):
    n_par = SEQ_PAR
    lc = min(RET_LC, t)
    nc = t // lc
    blk0 = row0 // lc
    log_g = jnp.log(1.0 - 2.0 ** (-5.0 - jnp.arange(RET_HEADS, dtype=F32)))
    pos = jnp.arange(lc, dtype=F32)
    diff = pos[:, None] - pos[None, :]
    dmask = jnp.where(diff >= 0, jnp.exp(jnp.maximum(diff, 0.0)[None] * log_g[:, None, None]), 0.0)
    w_end = jnp.exp((lc - 1 - pos)[:, None] * log_g[None, :])
    xi = jnp.exp((pos + 1.0)[:, None] * log_g[None, :])
    kdec = jnp.repeat(w_end, RET_DK, axis=1)
    qdec = jnp.repeat(xi, RET_DK, axis=1)
    gch = jnp.exp(lc * log_g)
    head = np.arange(512) // RET_DV
    avg = jnp.asarray((head[:, None] == head[None, :]) / RET_DV, BF16)

    def rows(s, wd, j):
        return pl.BlockSpec((lc, wd), lambda b, c: (blk0 + (b * n_par + s) * nc + c, j))

    def const(shape):
        nd = len(shape)
        return pl.BlockSpec(shape, lambda b, c: (0,) * nd)

    seq_specs, seq_args = [], []
    for s in range(n_par):
        seq_specs += [rows(s, 256, 0), rows(s, 256, 0), rows(s, 512, 1), rows(s, 512, 2)]
        seq_args += [qa, ka, pe, pe]
    state = pl.BlockSpec((n_par, RET_HEADS, RET_DK, RET_DV), lambda b, c: (b, 0, 0, 0))
    y, s_last = pl.pallas_call(
        functools.partial(_ret_kernel, n_par=n_par),
        grid=(n_seq // n_par, nc),
        in_specs=[pl.BlockSpec(memory_space=pltpu.SMEM)] + seq_specs + [
            const((RET_HEADS, lc, lc)), const((lc, 256)), const((lc, 256)),
            const((1, 512)), const((1, 512)), const((512, 512)), state],
        out_specs=[pl.BlockSpec((1, n_par, lc, 512), lambda b, c: (b, 0, c, 0)), state],
        out_shape=[jax.ShapeDtypeStruct((n_seq // n_par, n_par, t, 512), F32),
                   jax.ShapeDtypeStruct((n_seq, RET_HEADS, RET_DK, RET_DV), F32)],
        scratch_shapes=[pltpu.VMEM((n_par, RET_HEADS, RET_DK, RET_DV), F32),
                        pltpu.VMEM((lc, 512), F32)],
        compiler_params=_cparams(2),
        name="retention",
    )(gch, *seq_args, dmask, qdec, kdec, gn_g.reshape(1, 512), gn_b.reshape(1, 512), avg, s0)
    return y.reshape(n_seq * t, 512), s_last


def _col_reduce(x, op):
    r, c = x.shape
    return op(op(x.reshape(r // 8, 8, c), axis=0), axis=0, keepdims=True)


def _dsa_kernel(nkb_ref, q_ref, iq_ref, iwt_ref, qlim_ref, k_ref, vt_ref, ik_ref, *rest,
                topk, tq_out):
    o_ref, key_sc, m_sc, l_sc, acc_sc, lga_sc, lgb_sc = rest[-7:]
    nkb = nkb_ref[pl.program_id(1)]
    tq = qlim_ref.shape[1]
    tk = key_sc.shape[1]
    group = DSA_HEADS // DSA_KV_HEADS
    qlim = qlim_ref[...]
    iwt = iwt_ref[...] * ((IDX_HEADS * IDX_DIM) ** -0.5)
    iqs = iq_ref[...]
    krow = lax.broadcasted_iota(I32, (tk, tq), 0)

    def score_body(kb, carry):
        off = pl.multiple_of(kb * tk, tk)
        s_all = _dot_nt(ik_ref[pl.ds(off, tk), :], iqs)
        s = jnp.zeros((tk, tq), F32)
        for h in range(IDX_HEADS):
            s = s + iwt[h:h + 1, :] * jnp.maximum(s_all[:, h * tq:(h + 1) * tq], 0.0)
        s = jnp.where(s == 0.0, 0.0, s)
        bits = pltpu.bitcast(s, I32)
        key = jnp.where(bits >= 0, bits, bits ^ jnp.int32(0x7FFFFFFF))
        adm = (off + krow) < qlim
        key_sc[kb] = jnp.where(adm, key, jnp.int32(INT_MIN))
        return carry

    n_pair = (nkb + 1) // 2
    last = 2 * n_pair - 1

    def score_pair(j, carry):
        return score_body(2 * j + 1, score_body(2 * j, carry))

    lax.fori_loop(0, n_pair, score_pair, 0)

    def count(pred):
        def body(kb, acc):
            hit = jnp.where(pred(key_sc[kb]), 1.0, 0.0)
            return acc + jnp.sum(hit.reshape(tk // 64, 64, tq), axis=0)
        acc = lax.fori_loop(0, nkb, body, jnp.zeros((64, tq), F32))
        return jnp.sum(acc, axis=0, keepdims=True)

    def bit_body(it, ans):
        cand = ans + (jnp.int32(1) << (31 - it))
        return jnp.where(count(lambda k: k >= cand) >= topk, cand, ans)

    t = lax.fori_loop(0, 32, bit_body, jnp.full((1, tq), INT_MIN, I32))
    need = topk - count(lambda k: k > t)

    m_sc[...] = jnp.full(m_sc.shape, 0.1 * NEG_BIG, F32)
    l_sc[...] = jnp.zeros(l_sc.shape, F32)
    acc_sc[...] = jnp.zeros(acc_sc.shape, F32)
    r_i = lax.broadcasted_iota(I32, (tk, tk), 0)
    c_i = lax.broadcasted_iota(I32, (tk, tk), 1)
    lower = (c_i < r_i).astype(BF16)

    def logits_stage(kb, n_eq, dst):
        off = pl.multiple_of(kb * tk, tk)
        key = key_sc[kb]
        adm = (off + krow) < qlim
        eq = jnp.logical_and(key == t, adm)
        eqf = jnp.where(eq, 1.0, 0.0)
        pref = _dot(lower, eqf.astype(BF16))
        sel = jnp.logical_and(adm, jnp.logical_or(
            key > t, jnp.logical_and(eq, (n_eq + pref) < need)))
        bias = jnp.where(sel, 0.0, NEG_BIG)
        for n in range(DSA_KV_HEADS):
            lg_all = _dot_nt(k_ref[n, pl.ds(off, tk), :], q_ref[n])
            for g in range(group):
                ls = slice(g * tq, (g + 1) * tq)
                dst[n, :, ls] = lg_all[:, ls] + bias
        return n_eq + _col_reduce(eqf, jnp.sum)

    def softmax_stage(kb, src):
        vt = vt_ref[kb]
        for n in range(DSA_KV_HEADS):
            ps, alphas = [], []
            for g in range(group):
                ls = slice(g * tq, (g + 1) * tq)
                lg = src[n, :, ls]
                m_old = m_sc[n, :, ls]
                m_new = jnp.maximum(m_old, _col_reduce(lg, jnp.max))
                p = jnp.exp2(lg - m_new)
                alpha = jnp.exp2(m_old - m_new)
                l_sc[n, :, ls] = alpha * l_sc[n, :, ls] + _col_reduce(p, jnp.sum)
                m_sc[n, :, ls] = m_new
                ps.append(p.astype(BF16))
                alphas.append(alpha)
            p_all = jnp.concatenate(ps, axis=1)
            alpha_all = jnp.concatenate(alphas, axis=1)
            pv = _dot(vt[n * DSA_HD:(n + 1) * DSA_HD, :], p_all)
            acc_sc[n] = alpha_all * acc_sc[n] + pv

    def pair_body(j, n_eq):
        kb0 = 2 * j
        softmax_stage(kb0, lga_sc)
        n_eq = logits_stage(kb0 + 1, n_eq, lgb_sc)
        softmax_stage(kb0 + 1, lgb_sc)
        return logits_stage(jnp.minimum(kb0 + 2, last), n_eq, lga_sc)

    lax.fori_loop(0, n_pair, pair_body, logits_stage(0, jnp.zeros((1, tq), F32), lga_sc))
    pieces = []
    for n in range(DSA_KV_HEADS):
        o_n = acc_sc[n] / l_sc[n]
        for g in range(group):
            pieces.append(o_n[:, g * tq:(g + 1) * tq])
    o_ref[...] = jnp.concatenate(pieces, axis=0).T[:tq_out, :]


def _dsa(q_st, iq_st, iw_t, qlim, nkb, k_hm, v_t, ik_bf, n_seq, nq, s_len, tq_out, row0, n_out,
         topk, prev_out):
    tq = DSA_TQ
    group = DSA_HEADS // DSA_KV_HEADS
    blk0 = row0 // tq_out
    in_specs = [pl.BlockSpec((DSA_KV_HEADS, group * tq, DSA_HD), lambda b, i, s: (0, b * nq + i, 0)),
                pl.BlockSpec((IDX_HEADS * tq, IDX_DIM), lambda b, i, s: (b * nq + i, 0)),
                pl.BlockSpec((8, tq), lambda b, i, s: (0, b * nq + i)),
                pl.BlockSpec((1, tq), lambda b, i, s: (0, i)),
                pl.BlockSpec((DSA_KV_HEADS, s_len, DSA_HD), lambda b, i, s: (0, b, 0)),
                pl.BlockSpec((s_len // DSA_TK, LANE, DSA_TK), lambda b, i, s: (b, 0, 0)),
                pl.BlockSpec((s_len, IDX_DIM), lambda b, i, s: (b, 0))]
    args = [nkb, q_st, iq_st, iw_t, qlim, k_hm, v_t, ik_bf]
    aliases = {}
    if prev_out is not None:
        in_specs.append(pl.BlockSpec(memory_space=pl.ANY))
        args.append(prev_out)
        aliases = {len(args) - 1: 0}
    grid_spec = pltpu.PrefetchScalarGridSpec(
        num_scalar_prefetch=1,
        grid=(n_seq, nq),
        in_specs=in_specs,
        out_specs=pl.BlockSpec((tq_out, 512), lambda b, i, s: (blk0 + b * nq + i, 0)),
        scratch_shapes=[pltpu.VMEM((s_len // DSA_TK, DSA_TK, tq), I32),
                        pltpu.VMEM((DSA_KV_HEADS, 1, group * tq), F32),
                        pltpu.VMEM((DSA_KV_HEADS, 1, group * tq), F32),
                        pltpu.VMEM((DSA_KV_HEADS, DSA_HD, group * tq), F32),
                        pltpu.VMEM((DSA_KV_HEADS, DSA_TK, group * tq), F32),
                        pltpu.VMEM((DSA_KV_HEADS, DSA_TK, group * tq), F32)])
    return pl.pallas_call(
        functools.partial(_dsa_kernel, topk=topk, tq_out=tq_out),
        grid_spec=grid_spec,
        out_shape=jax.ShapeDtypeStruct((n_out, 512), F32),
        input_output_aliases=aliases,
        compiler_params=_cparams(2),
        name="dsa_attention",
    )(*args)


def _outproj_ln_kernel(*refs, counts, n_first):
    na, nb, nh = counts
    ya = _read_rows(refs[:na], n_first)
    yb = _read_rows(refs[na:na + nb], n_first)
    w_ref = refs[na + nb]
    h = _read_rows(refs[na + nb + 1:na + nb + 1 + nh], n_first)
    g_ref, b_ref, o_ref, op_ref = refs[na + nb + 1 + nh:]
    half = ya.shape[1]
    y = _dot(ya.astype(BF16), w_ref[:half, :]) + _dot(yb.astype(BF16), w_ref[half:, :])
    out = _layer_norm(DN_ALPHA * h + y, g_ref[...], b_ref[...])
    o_ref[...] = out
    op_ref[...] = _pack_pairs(out)


def _outproj_ln(ya_parts, yb_parts, w, h_parts, g, b, tm, n_first):
    n = sum(p.shape[0] for p in h_parts)
    specs = []
    for parts in (ya_parts, yb_parts):
        sp, nf = _row_specs(parts, tm)
        assert len(parts) == 1 or nf == n_first
        specs += sp
    h_specs, nf = _row_specs(h_parts, tm)
    assert len(h_parts) == 1 or nf == n_first
    return pl.pallas_call(
        functools.partial(_outproj_ln_kernel, counts=(len(ya_parts), len(yb_parts), len(h_parts)),
                          n_first=n_first),
        grid=(n // tm,),
        in_specs=specs + [pl.BlockSpec((D_MODEL, D_MODEL), lambda i: (0, 0))] + h_specs + [
                  pl.BlockSpec((1, D_MODEL), lambda i: (0, 0)),
                  pl.BlockSpec((1, D_MODEL), lambda i: (0, 0))],
        out_specs=[pl.BlockSpec((tm, D_MODEL), lambda i: (i, 0)),
                   pl.BlockSpec((tm, D_MODEL // 2), lambda i: (i, 0))],
        out_shape=[jax.ShapeDtypeStruct((n, D_MODEL), F32),
                   jax.ShapeDtypeStruct((n, D_MODEL // 2), U32)],
        compiler_params=_cparams(1),
        name="out_proj_ln",
    )(*ya_parts, *yb_parts, w, *h_parts, g.reshape(1, -1), b.reshape(1, -1))


def _router_kernel(x_ref, w_ref, b_ref, idx_o, gate_o, rank_o, cnt_o, cnt_sc):
    i = pl.program_id(0)

    @pl.when(i == 0)
    def _():
        cnt_sc[...] = jnp.zeros(cnt_sc.shape, F32)

    tm = x_ref.shape[0]
    ne = w_ref.shape[0]
    x = x_ref[...]
    w = w_ref[...]
    x_hi = x.astype(BF16)
    x_lo = (x - x_hi.astype(F32)).astype(BF16)
    w_hi = w.astype(BF16)
    w_lo = (w - w_hi.astype(F32)).astype(BF16)
    logits = (_dot_nt(w_hi, x_hi) + _dot_nt(w_lo, x_hi) + _dot_nt(w_hi, x_lo)) + b_ref[...]
    erow = lax.broadcasted_iota(I32, (ne, tm), 0)
    vals, idxs = [], []
    onehot = jnp.zeros((ne, tm), F32)
    for _ in range(TOP_K):
        m = jnp.max(logits, axis=0, keepdims=True)
        ix = jnp.min(jnp.where(logits == m, erow, ne), axis=0, keepdims=True)
        hit = erow == ix
        onehot = jnp.where(hit, 1.0, onehot)
        logits = jnp.where(hit, -jnp.inf, logits)
        vals.append(m)
        idxs.append(ix)
    es = [jnp.exp(v - vals[0]) for v in vals]
    den = es[0] + es[1] + es[2] + es[3]
    r_i = lax.broadcasted_iota(I32, (tm, tm), 0)
    c_i = lax.broadcasted_iota(I32, (tm, tm), 1)
    upper = (r_i < c_i).astype(BF16)
    rank_dense = _dot(onehot.astype(BF16), upper) + cnt_sc[...]
    prow = lax.broadcasted_iota(I32, (8, tm), 0)
    idx_out = jnp.zeros((8, tm), I32)
    gate_out = jnp.zeros((8, tm), F32)
    rank_out = jnp.zeros((8, tm), F32)
    for k in range(TOP_K):
        rk = jnp.sum(jnp.where(erow == idxs[k], rank_dense, 0.0), axis=0, keepdims=True)
        idx_out = jnp.where(prow == k, idxs[k], idx_out)
        gate_out = jnp.where(prow == k, es[k] / den, gate_out)
        rank_out = jnp.where(prow == k, rk, rank_out)
    idx_o[...] = idx_out
    gate_o[...] = gate_out
    rank_o[...] = rank_out.astype(I32)
    cnt = cnt_sc[...] + jnp.sum(onehot, axis=1, keepdims=True)
    cnt_sc[...] = cnt
    cnt_o[...] = cnt


def _router(x, w_r, b_r, tm):
    n = x.shape[0]
    row = pl.BlockSpec((8, tm), lambda i: (0, i))
    return pl.pallas_call(
        _router_kernel,
        grid=(n // tm,),
        in_specs=[pl.BlockSpec((tm, D_MODEL), lambda i: (i, 0)),
                  pl.BlockSpec((N_EXPERTS, D_MODEL), lambda i: (0, 0)),
                  pl.BlockSpec((N_EXPERTS, 1), lambda i: (0, 0))],
        out_specs=[row, row, row, pl.BlockSpec((N_EXPERTS, 1), lambda i: (0, 0))],
        out_shape=[jax.ShapeDtypeStruct((8, n), I32), jax.ShapeDtypeStruct((8, n), F32),
                   jax.ShapeDtypeStruct((8, n), I32), jax.ShapeDtypeStruct((N_EXPERTS, 1), F32)],
        scratch_shapes=[pltpu.VMEM((N_EXPERTS, 1), F32)],
        compiler_params=_cparams(1),
        name="moe_router",
    )(x, w_r.T, b_r.reshape(N_EXPERTS, 1))


def _moe_kernel(be_ref, nu_ref, *rest, n_parts, blocks_per_part):
    x_refs = rest[:n_parts]
    wgu_ref, bgu_ref, wdn_ref, bdn_ref, o_ref, wgu_sc, wdn_sc = rest[n_parts:]
    i = pl.program_id(0)

    @pl.when(jnp.logical_or(i == 0, be_ref[i] != be_ref[jnp.maximum(i - 1, 0)]))
    def _():
        wgu_sc[...] = wgu_ref[0, 0].astype(BF16)
        wdn_sc[...] = wdn_ref[0, 0].astype(BF16)

    @pl.when(i < nu_ref[0])
    def _():
        part = i // blocks_per_part
        xw = x_refs[0][...]
        for c in range(1, n_parts):
            xw = jnp.where(part == c, x_refs[c][...], xw)
        xa, xb = _unpack_pairs(xw)
        x = jnp.concatenate([xa.astype(BF16), xb.astype(BF16)], axis=1)
        h = _dot(x, wgu_sc[...]) + bgu_ref[0, 0]
        g = jnp.minimum(h[:, :D_FF], SWIGLU_LIMIT)
        up = jnp.clip(h[:, D_FF:], -SWIGLU_LIMIT, SWIGLU_LIMIT)
        a = (up + 1.0) * g * jax.nn.sigmoid(SWIGLU_ALPHA * g)
        o_ref[...] = _pack_pairs(_dot(a.astype(BF16), wdn_sc[...]) + bdn_ref[0, 0])

    @pl.when(i >= nu_ref[0])
    def _():
        o_ref[...] = jnp.zeros(o_ref.shape, U32)


def _moe_experts(xs_parts, blk_e, n_used, layer, w_gu, b_gu, w_dn, b_dn):
    n_parts = len(xs_parts)
    tm = MOE_TM
    bpp = xs_parts[0].shape[0] // tm
    n_rows = n_parts * bpp * tm
    depth = w_gu.shape[0]

    def x_spec(c):
        return pl.BlockSpec((tm, D_MODEL // 2),
                            lambda i, be, nu: (jnp.clip(i - c * bpp, 0, bpp - 1), 0))
    grid_spec = pltpu.PrefetchScalarGridSpec(
        num_scalar_prefetch=2,
        grid=(n_rows // tm,),
        in_specs=[x_spec(c) for c in range(n_parts)] + [
                  pl.BlockSpec((1, 1, D_MODEL, 2 * D_FF), lambda i, be, nu: (layer, be[i], 0, 0)),
                  pl.BlockSpec((1, 1, 1, 2 * D_FF), lambda i, be, nu: (layer, be[i], 0, 0)),
                  pl.BlockSpec((1, 1, D_FF, D_MODEL), lambda i, be, nu: (layer, be[i], 0, 0)),
                  pl.BlockSpec((1, 1, 1, D_MODEL), lambda i, be, nu: (layer, be[i], 0, 0))],
        out_specs=pl.BlockSpec((tm, D_MODEL // 2), lambda i, be, nu: (i, 0)),
        scratch_shapes=[pltpu.VMEM((D_MODEL, 2 * D_FF), BF16), pltpu.VMEM((D_FF, D_MODEL), BF16)])
    return pl.pallas_call(
        functools.partial(_moe_kernel, n_parts=n_parts, blocks_per_part=bpp),
        grid_spec=grid_spec,
        out_shape=jax.ShapeDtypeStruct((n_rows, D_MODEL // 2), U32),
        compiler_params=_cparams(1),
        name="moe_experts",
    )(blk_e, n_used, *xs_parts, w_gu, b_gu.reshape(depth, N_EXPERTS, 1, -1), w_dn,
      b_dn.reshape(depth, N_EXPERTS, 1, -1))


def _combine_ln_kernel(h_ref, y0_ref, y1_ref, y2_ref, y3_ref, gate_ref, g_ref, b_ref, *o_refs,
                       n_first):
    gate = gate_ref[...]
    ya, yb = None, None
    for k, y_ref in enumerate((y0_ref, y1_ref, y2_ref, y3_ref)):
        a, b = _unpack_pairs(y_ref[...])
        gk = gate[:, k:k + 1]
        ya = gk * a if ya is None else ya + gk * a
        yb = gk * b if yb is None else yb + gk * b
    y = jnp.concatenate([ya, yb], axis=1)
    out = _layer_norm(DN_ALPHA * h_ref[...] + y, g_ref[...], b_ref[...])
    if len(o_refs) == 1:
        o_refs[0][...] = out
    else:
        @pl.when(pl.program_id(0) < n_first)
        def _():
            o_refs[0][...] = out

        @pl.when(pl.program_id(0) >= n_first)
        def _():
            o_refs[1][...] = out


def _combine_ln(h, ys, gate, g, b, tm, split_rows=None):
    n = h.shape[0]
    row = pl.BlockSpec((tm, D_MODEL), lambda i: (i, 0))
    half = pl.BlockSpec((tm, D_MODEL // 2), lambda i: (i, 0))
    vec = pl.BlockSpec((1, D_MODEL), lambda i: (0, 0))
    if split_rows is None:
        n_first = n // tm
        out_specs = row
        out_shape = jax.ShapeDtypeStruct((n, D_MODEL), F32)
    else:
        n_first = split_rows // tm
        out_specs = [pl.BlockSpec((tm, D_MODEL), lambda i: (jnp.minimum(i, n_first - 1), 0)),
                     pl.BlockSpec((tm, D_MODEL), lambda i: (jnp.maximum(i - n_first, 0), 0))]
        out_shape = [jax.ShapeDtypeStruct((split_rows, D_MODEL), F32),
                     jax.ShapeDtypeStruct((n - split_rows, D_MODEL), F32)]
    return pl.pallas_call(
        functools.partial(_combine_ln_kernel, n_first=n_first),
        grid=(n // tm,),
        in_specs=[row, half, half, half, half, pl.BlockSpec((tm, TOP_K), lambda i: (i, 0)), vec, vec],
        out_specs=out_specs,
        out_shape=out_shape,
        compiler_params=_cparams(1),
        name="moe_combine_ln",
    )(h, ys[0], ys[1], ys[2], ys[3], gate, g.reshape(1, -1), b.reshape(1, -1))


def _rows(x, idx):
    return x.at[idx].get(mode="promise_in_bounds")


def _moe_layer(h, h_packed, layer, w_r, b_r, w_gu, b_gu, w_dn, b_dn, ln_g, ln_b, tm, split_rows=None):
    n = h.shape[0]
    n_pair = n * TOP_K
    idx_t, gate_t, rank_t, cnt = _router(h, w_r, b_r, tm)
    top_i = idx_t[:TOP_K].T
    counts = cnt[:, 0].astype(I32)
    padded = (counts + MOE_TM - 1) // MOE_TM * MOE_TM
    pad_end = jnp.cumsum(padded)
    start = pad_end - padded
    first = jnp.cumsum(counts) - counts
    dest = _rows(start, top_i) + rank_t[:TOP_K].T
    n_blk = -(-(-(-n_pair // MOE_TM) + N_EXPERTS) // MOE_PARTS) * MOE_PARTS
    n_used = (pad_end[-1] // MOE_TM).astype(I32)
    blk_row = jnp.minimum(jnp.arange(n_blk, dtype=I32), n_used - 1) * MOE_TM
    blk_e = jnp.sum((pad_end[None, :] <= blk_row[:, None]).astype(I32), axis=1)
    blk_e = jnp.minimum(blk_e, N_EXPERTS - 1)
    order = jnp.argsort(top_i.reshape(-1), stable=True).astype(I32)
    row_in_e = (jnp.arange(n_blk, dtype=I32)[:, None] * MOE_TM - _rows(start, blk_e)[:, None]
                + jnp.arange(MOE_TM, dtype=I32)[None, :])
    pair = jnp.clip(_rows(first, blk_e)[:, None] + row_in_e, 0, n_pair - 1).reshape(-1)
    src = (_rows(order, pair) // TOP_K).reshape(MOE_PARTS, -1)
    xs_parts = [_rows(h_packed, src[c]) for c in range(MOE_PARTS)]
    ybuf = _moe_experts(xs_parts, blk_e, n_used.reshape(1), layer, w_gu, b_gu, w_dn, b_dn)
    ys = [_rows(ybuf, dest[:, k]) for k in range(TOP_K)]
    return _combine_ln(h, ys, gate_t[:TOP_K].T, ln_g, ln_b, tm, split_rows)


def _band_kernel(q_ref, *rest, nkb):
    k_refs = rest[:nkb]
    v_refs = rest[nkb:2 * nkb]
    bias_ref = rest[2 * nkb]
    o_ref = rest[-1]
    c = pl.program_id(1)
    tq = q_ref.shape[0]
    tkb = k_refs[0].shape[0]
    q_scale = BAND_HD ** -0.5 * math.log2(math.e)
    kbs = [r[...].astype(BF16) for r in k_refs]
    vbs = [r[...].astype(BF16) for r in v_refs]
    off = [jnp.where((c + j - (nkb - 1)) >= 0, 0.0, NEG_BIG) for j in range(nkb)]
    low = lax.broadcasted_iota(I32, (tq, LANE), 1) < BAND_HD
    for pair in range(BAND_HEADS // 2):
        ls = slice(pair * LANE, (pair + 1) * LANE)
        qp = q_ref[:, ls] * q_scale
        outs = []
        for par in range(2):
            h = 2 * pair + par
            qh = jnp.where(low == (par == 0), qp, 0.0).astype(BF16)
            lgs = [_dot_nt(qh, kbs[j][:, ls]) + (bias_ref[h, :, j * tkb:(j + 1) * tkb] + off[j])
                   for j in range(nkb)]
            mx = lgs[0]
            for j in range(1, nkb):
                mx = jnp.maximum(mx, lgs[j])
            m = mx.max(axis=1, keepdims=True)
            ps = [jnp.exp2(lg - m) for lg in lgs]
            sm = ps[0]
            for j in range(1, nkb):
                sm = sm + ps[j]
            den = sm.sum(axis=1, keepdims=True)
            acc = _dot(ps[0].astype(BF16), vbs[0][:, ls])
            for j in range(1, nkb):
                acc = acc + _dot(ps[j].astype(BF16), vbs[j][:, ls])
            outs.append(acc * (1.0 / den))
        o_ref[:, ls] = jnp.where(low, outs[0], outs[1])


def _band(q_arr, k_arr, v_arr, cols, bias, n_seq, t, tq, tkb, nkb, q_row0, kv_blocks_per_seq,
          n_out, prev_out):
    nq = t // tq
    qblk0 = q_row0 // tq
    qcol, kcol, vcol = cols

    def kv_spec(j, col):
        def ix(b, c):
            return (b * kv_blocks_per_seq + jnp.maximum(c + j - (nkb - 1), 0), col)
        return pl.BlockSpec((tkb, 512), ix)

    in_specs = ([pl.BlockSpec((tq, 512), lambda b, c: (qblk0 + b * nq + c, qcol))]
                + [kv_spec(j, kcol) for j in range(nkb)]
                + [kv_spec(j, vcol) for j in range(nkb)]
                + [pl.BlockSpec(bias.shape, lambda b, c: (0, 0, 0))])
    args = [q_arr] + [k_arr] * nkb + [v_arr] * nkb + [bias]
    aliases = {}
    if prev_out is not None:
        in_specs.append(pl.BlockSpec(memory_space=pl.ANY))
        args.append(prev_out)
        aliases = {len(args) - 1: 0}
    return pl.pallas_call(
        functools.partial(_band_kernel, nkb=nkb),
        grid=(n_seq, nq),
        in_specs=in_specs,
        out_specs=pl.BlockSpec((tq, 512), lambda b, c: (qblk0 + b * nq + c, 0)),
        out_shape=jax.ShapeDtypeStruct((n_out, 512), F32),
        input_output_aliases=aliases,
        compiler_params=_cparams(2),
        name="band_attention",
    )(*args)


def _band_bias(rel_bias, tq, n_keys, key0):
    n_off = tq + n_keys - 1
    d_max = tq - 1 - key0
    rel = np.clip(d_max - np.arange(n_off), -REL_CLIP, REL_CLIP) + REL_CLIP
    vals = jnp.concatenate([rel_bias[:, rel], jnp.zeros((rel_bias.shape[0], 1), F32)], axis=1)
    rot = jnp.tile(vals, (1, tq))[:, :tq * n_off].reshape(-1, tq, n_off)
    toep = rot[:, :, tq - 1:tq - 1 + n_keys]
    qp = np.arange(tq)[:, None]
    kp = key0 + np.arange(n_keys)[None, :]
    cs = (qp // CHUNK) * CHUNK
    band = np.logical_and(kp >= cs - BAND_PAST, kp < cs + CHUNK)
    return jnp.where(jnp.asarray(band)[None], toep * math.log2(math.e), NEG_BIG).astype(F32)


def _ssd_kernel(dsk_ref, *refs, n_par):
    seq_refs = [refs[3 * s:3 * s + 3] for s in range(n_par)]
    (cw_ref, cb_ref, dtb_ref, alog_ref, ng_ref, h0_ref, c0_ref,
     o_ref, h_ref, cl_ref, h_sc, xe_sc, y_sc) = refs[3 * n_par:]
    c = pl.program_id(1)
    lc = seq_refs[0][0].shape[0]
    cdim = xe_sc.shape[2]

    @pl.when(c == 0)
    def _():
        h_sc[...] = h0_ref[...]
        for s in range(n_par):
            xe_sc[s, 0:8, :] = jnp.zeros((8, cdim), F32)
            xe_sc[s, 8 - (SSD_CONV - 1):8, :] = c0_ref[s]

    r_i = lax.broadcasted_iota(I32, (lc, lc), 0)
    c_i = lax.broadcasted_iota(I32, (lc, lc), 1)
    causal = c_i <= r_i
    tri = causal.astype(F32)
    gs = SSD_GROUPS * SSD_STATE
    hpg = SSD_HEADS // SSD_GROUPS
    gw = SSD_INNER // SSD_GROUPS
    for s, (z_ref, xbc_ref, dt_ref) in enumerate(seq_refs):
        xe_sc[s, 8:8 + lc, :] = xbc_ref[...]
        conv = cb_ref[...] + cw_ref[SSD_CONV - 1:SSD_CONV, :] * xe_sc[s, 8:8 + lc, :]
        for sh in range(1, SSD_CONV):
            conv = conv + cw_ref[SSD_CONV - 1 - sh:SSD_CONV - sh, :] * xe_sc[s, 8 - sh:8 - sh + lc, :]
        u = _silu(conv)
        xs = u[:, :SSD_INNER]
        bm = u[:, SSD_INNER:SSD_INNER + gs].astype(BF16)
        cm = u[:, SSD_INNER + gs:].astype(BF16)
        dx = dt_ref[...] + dtb_ref[...]
        dtv = jnp.maximum(dx, 0.0) + jnp.log1p(jnp.exp(-jnp.abs(dx)))
        a = dtv * (-jnp.exp(alog_ref[...]))
        acum = _dot_f32(tri, a)
        acum_t = acum.T
        for g in range(SSD_GROUPS):
            ss = slice(g * SSD_STATE, (g + 1) * SSD_STATE)
            cb = _dot_nt(cm[:, ss], bm[:, ss])
            for jj in range(hpg):
                j = g * hpg + jj
                ps = slice(j * SSD_HD, (j + 1) * SSD_HD)
                col = acum[:, j:j + 1]
                row = acum_t[j:j + 1, :]
                lmat = jnp.exp(jnp.where(causal, col - row, -jnp.inf))
                x_j = xs[:, ps]
                xdt = x_j * dtv[:, j:j + 1]
                h_old = h_sc[s, j]
                y = _dot((cb * lmat).astype(BF16), xdt.astype(BF16))
                y = y + _dot_nt(cm[:, ss], h_old.astype(BF16)) * jnp.exp(col)
                y = y + dsk_ref[j] * x_j
                last = acum[lc - 1:lc, j:j + 1]
                st = _dot_tn((xdt * jnp.exp(last - col)).astype(BF16), bm[:, ss])
                h_sc[s, j] = h_old * jnp.exp(last) + st
                y_sc[s, :, ps] = y
        yd = y_sc[s] * _silu(z_ref[...])
        for g in range(SSD_GROUPS):
            ws = slice(g * gw, (g + 1) * gw)
            yg = yd[:, ws]
            ms = jnp.mean(yg * yg, axis=-1, keepdims=True)
            o_ref[0, s, :, ws] = yg * lax.rsqrt(ms + LN_EPS) * ng_ref[:, ws]
        xe_sc[s, 0:8, :] = xe_sc[s, lc:lc + 8, :]

    @pl.when(c == pl.num_programs(1) - 1)
    def _():
        h_ref[...] = h_sc[...]
        for s in range(n_par):
            cl_ref[s] = xe_sc[s, 8 - (SSD_CONV - 1):8, :]


def _ssd(po, h0, c0, conv_w, conv_b, dt_bias, a_log, d_skip, norm_g, n_seq, t, row0):
    n_par = SEQ_PAR
    lc = min(SSD_LC, t)
    nc = t // lc
    blk0 = row0 // lc
    cdim = conv_w.shape[1]

    def rows(s, wd, j):
        return pl.BlockSpec((lc, wd), lambda b, c: (blk0 + (b * n_par + s) * nc + c, j))

    def const(shape):
        nd = len(shape)
        return pl.BlockSpec(shape, lambda b, c: (0,) * nd)

    pad8 = lambda v: jnp.zeros((1, LANE), F32).at[0, :SSD_HEADS].set(v)
    seq_specs, seq_args = [], []
    for s in range(n_par):
        seq_specs += [rows(s, 512, 3), rows(s, cdim, 2), rows(s, LANE, 24)]
        seq_args += [po, po, po]
    h_spec = pl.BlockSpec((n_par, SSD_HEADS, SSD_HD, SSD_STATE), lambda b, c: (b, 0, 0, 0))
    c_spec = pl.BlockSpec((n_par, SSD_CONV - 1, cdim), lambda b, c: (b, 0, 0))
    y, h_last, c_last = pl.pallas_call(
        functools.partial(_ssd_kernel, n_par=n_par),
        grid=(n_seq // n_par, nc),
        in_specs=[pl.BlockSpec(memory_space=pltpu.SMEM)] + seq_specs + [
            const((SSD_CONV, cdim)), const((1, cdim)), const((1, LANE)), const((1, LANE)),
            const((1, SSD_INNER)), h_spec, c_spec],
        out_specs=[pl.BlockSpec((1, n_par, lc, 512), lambda b, c: (b, 0, c, 0)), h_spec, c_spec],
        out_shape=[jax.ShapeDtypeStruct((n_seq // n_par, n_par, t, 512), F32),
                   jax.ShapeDtypeStruct((n_seq, SSD_HEADS, SSD_HD, SSD_STATE), F32),
                   jax.ShapeDtypeStruct((n_seq, SSD_CONV - 1, cdim), F32)],
        scratch_shapes=[pltpu.VMEM((n_par, SSD_HEADS, SSD_HD, SSD_STATE), F32),
                        pltpu.VMEM((n_par, lc + 8, cdim), F32),
                        pltpu.VMEM((n_par, lc, 512), F32)],
        compiler_params=_cparams(2),
        name="ssd_scan",
    )(d_skip, *seq_args, conv_w, conv_b.reshape(1, -1), pad8(dt_bias), pad8(a_log),
      norm_g.reshape(1, -1), h0, c0)
    return y.reshape(n_seq * t, 512), h_last, c_last


def _pad_cols(w, width):
    return jnp.concatenate([w, jnp.zeros((w.shape[0], width - w.shape[1]), w.dtype)], axis=1)


def kernel(x_prompt, x_sample, state_ret, cache_dsa_k, cache_dsa_v, cache_dsa_kidx, cache_band_k, cache_band_v, state_ssm, state_conv, e_w_in, e_w_out, e_gn_g, e_gn_b, o_w_in, o_w_out, o_rel_bias, o_conv_w, o_conv_b, o_dt_bias, o_a_log, o_d_skip, o_norm_g, ln1_g, ln1_b, ln2_g, ln2_b, router_w, router_b, exp_w_gu, exp_b_gu, exp_w_dn, exp_b_dn):
    bp, tp, _ = x_prompt.shape
    bs, ts, _ = x_sample.shape
    past = cache_dsa_k.shape[2]
    n_p, n_s = bp * tp, bs * ts
    n = n_p + n_s
    tm = math.gcd(512, math.gcd(n_p, n_s))
    assert tp % tm == 0 and tm % ts == 0 and ts == CHUNK

    x_parts = (x_prompt.reshape(n_p, D_MODEL), x_sample.reshape(n_s, D_MODEL))

    def blank():
        return jnp.zeros((n, 512), F32)

    pe = _proj(x_parts, _pad_cols(e_w_in[0], EVEN_W).astype(BF16), tm)
    pos_p = jnp.arange(tp, dtype=I32)
    pos_s = past + jnp.arange(ts, dtype=I32)
    pos_tab = jnp.concatenate([pos_p, jnp.tile(pos_s, tm // ts)])
    tabs = (_rope_tables(pos_tab, RET_HEADS, RET_DK, RET_DK, RET_THETA),
            _rope_tables(pos_tab, RET_HEADS, RET_DK, RET_DK, RET_THETA, scale=RET_DK ** -0.5),
            _rope_tables(pos_tab, DSA_HEADS, DSA_HD, DSA_ROT, ROPE_THETA),
            _rope_tables(pos_tab, 1, IDX_DIM, DSA_ROT, ROPE_THETA, pad_to=LANE))
    (qa, ka, qb, kb, iq, ikw, q_st, iq_st, k_hm, ik_bf, v_t, iw_t) = _even_prep(
        pe, tabs, tm, n_p // tm, tp // tm)

    ya_p, ret_p = _retention(qa, ka, pe, jnp.zeros((bp, RET_HEADS, RET_DK, RET_DV), F32),
                             e_gn_g[0], e_gn_b[0], bp, tp, 0)
    ya_s, ret_s = _retention(qa, ka, pe, state_ret[0], e_gn_g[0], e_gn_b[0], bs, ts, n_p)

    topk_p = min(DSA_TOPK_MAX, tp // 4)
    qlim_p = (((pos_p // CHUNK) + 1) * CHUNK).reshape(1, tp)
    nq_p = tp // DSA_TQ
    nkb_p = ((jnp.arange(nq_p, dtype=I32) + 1) * DSA_TQ + DSA_TK - 1) // DSA_TK
    yb = _dsa(q_st, iq_st, iw_t, qlim_p, nkb_p, k_hm, v_t, ik_bf, bp, nq_p, tp, DSA_TQ, 0, n,
              topk_p, blank())

    s_len = past + ts
    s_pad = -(-s_len // (2 * DSA_TK)) * (2 * DSA_TK)
    topk_s = min(DSA_TOPK_MAX, s_len // 4)
    group = DSA_HEADS // DSA_KV_HEADS

    def cat_keys(cache, new, wd):
        zpad = jnp.zeros((bs, s_pad - s_len, wd), F32)
        return jnp.concatenate([cache, new.reshape(bs, ts, wd), zpad], axis=1)

    def pad_q(x):
        return jnp.concatenate([x, jnp.zeros((bs, DSA_TQ - ts) + x.shape[2:], x.dtype)], axis=1)

    ks = cat_keys(cache_dsa_k[0].reshape(bs, past, LANE), kb[n_p:], LANE)
    vs = cat_keys(cache_dsa_v[0].reshape(bs, past, LANE), pe[n_p:, 2176:2304], LANE)
    iks = cat_keys(cache_dsa_kidx[0], ikw[n_p:, :IDX_DIM], IDX_DIM)
    k_hm_s = ks.reshape(bs, s_pad, DSA_KV_HEADS, DSA_HD).transpose(2, 0, 1, 3).reshape(
        DSA_KV_HEADS, bs * s_pad, DSA_HD).astype(BF16)
    v_t_s = vs.reshape(bs, s_pad // DSA_TK, DSA_TK, LANE).transpose(0, 1, 3, 2).reshape(
        bs * (s_pad // DSA_TK), LANE, DSA_TK).astype(BF16)
    ik_s = iks.reshape(bs * s_pad, IDX_DIM).astype(BF16)
    q_s = pad_q(qb[n_p:].reshape(bs, ts, DSA_KV_HEADS, group, DSA_HD))
    q_st_s = q_s.transpose(2, 0, 3, 1, 4).reshape(DSA_KV_HEADS, bs * group * DSA_TQ, DSA_HD)
    iq_s = pad_q(iq[n_p:].reshape(bs, ts, IDX_HEADS, IDX_DIM))
    iq_st_s = iq_s.transpose(0, 2, 1, 3).reshape(bs * IDX_HEADS * DSA_TQ, IDX_DIM)
    iw_t_s = pad_q(ikw[n_p:, IDX_DIM:IDX_DIM + 8].reshape(bs, ts, 8)).reshape(bs * DSA_TQ, 8).T
    qlim_s = jnp.full((1, DSA_TQ), s_len, I32)
    nkb_s = jnp.full((1,), -(-s_len // DSA_TK), I32)
    yb = _dsa(q_st_s, iq_st_s, iw_t_s, qlim_s, nkb_s, k_hm_s, v_t_s, ik_s, bs, 1, s_pad, ts, n_p, n,
              topk_s, yb)

    h, h_packed = _outproj_ln((ya_p, ya_s), (yb,), e_w_out[0].astype(BF16), x_parts, ln1_g[0], ln1_b[0],
                              tm, n_p // tm)
    h = _moe_layer(h, h_packed, 0, router_w[0], router_b[0], exp_w_gu, exp_b_gu, exp_w_dn, exp_b_dn,
                   ln2_g[0], ln2_b[0], tm)

    po = _proj((h,), _pad_cols(o_w_in[0], ODD_W).astype(BF16), tm)
    tq_p = min(BAND_TQ, tp)
    nkb_band = BAND_PAST // tq_p + 1
    bias_p = _band_bias(o_rel_bias[0], tq_p, nkb_band * tq_p, -(nkb_band - 1) * tq_p)
    yc = _band(po, po, po, (0, 1, 2), bias_p, bp, tp, tq_p, tq_p, nkb_band, 0, tp // tq_p, n,
               blank())
    band_len = cache_band_k.shape[2]
    kc_new = po[n_p:, 512:1024].reshape(bs, ts, 512)
    vc_new = po[n_p:, 1024:1536].reshape(bs, ts, 512)
    kcat = jnp.concatenate([cache_band_k[0].reshape(bs, band_len, 512), kc_new], axis=1)
    vcat = jnp.concatenate([cache_band_v[0].reshape(bs, band_len, 512), vc_new], axis=1)
    wlen = band_len + ts
    bias_s = _band_bias(o_rel_bias[0], ts, wlen, -band_len)
    yc = _band(po, kcat.reshape(bs * wlen, 512), vcat.reshape(bs * wlen, 512), (0, 0, 0), bias_s,
               bs, ts, ts, wlen, 1, n_p, 1, n, yc)

    ssd_w = (o_conv_w[0], o_conv_b[0], o_dt_bias[0], o_a_log[0], o_d_skip[0], o_norm_g[0])
    cdim = o_conv_w.shape[2]
    yd_p, ssm_p, conv_p = _ssd(po, jnp.zeros((bp, SSD_HEADS, SSD_HD, SSD_STATE), F32),
                               jnp.zeros((bp, SSD_CONV - 1, cdim), F32), *ssd_w, bp, tp, 0)
    yd_s, ssm_s, conv_s = _ssd(po, state_ssm[0], state_conv[0], *ssd_w, bs, ts, n_p)

    h, h_packed = _outproj_ln((yc,), (yd_p, yd_s), o_w_out[0].astype(BF16), (h,), ln1_g[1], ln1_b[1],
                              tm, n_p // tm)
    h_p, h_s = _moe_layer(h, h_packed, 1, router_w[1], router_b[1], exp_w_gu, exp_b_gu, exp_w_dn, exp_b_dn,
                          ln2_g[1], ln2_b[1], tm, split_rows=n_p)

    keep = min(BAND_PAST, tp)
    kd = DSA_KV_HEADS * DSA_HD
    kc_p = po[:n_p, 512:1024].reshape(bp, tp, BAND_HEADS, BAND_HD)[:, -keep:]
    vc_p = po[:n_p, 1024:1536].reshape(bp, tp, BAND_HEADS, BAND_HD)[:, -keep:]
    return (h_p.reshape(bp, tp, D_MODEL), h_s.reshape(bs, ts, D_MODEL),
            ret_p[None],
            kb[:n_p].reshape(1, bp, tp, DSA_KV_HEADS, DSA_HD),
            pe[:n_p, 2176:2176 + kd].reshape(1, bp, tp, DSA_KV_HEADS, DSA_HD),
            ikw[:n_p, :IDX_DIM].reshape(1, bp, tp, IDX_DIM),
            kc_p[None], vc_p[None], ssm_p[None], conv_p[None],
            ret_s[None],
            kb[n_p:].reshape(1, bs, ts, DSA_KV_HEADS, DSA_HD),
            pe[n_p:, 2176:2176 + kd].reshape(1, bs, ts, DSA_KV_HEADS, DSA_HD),
            ikw[n_p:, :IDX_DIM].reshape(1, bs, ts, IDX_DIM),
            kc_new.reshape(1, bs, ts, BAND_HEADS, BAND_HD), vc_new.reshape(1, bs, ts, BAND_HEADS, BAND_HD),
            ssm_s[None], conv_s[None])
```

```python
import functools
import math

import jax
import jax.numpy as jnp
import numpy as np
from jax import lax
from jax.experimental import pallas as pl
from jax.experimental.pallas import tpu as pltpu

F32 = jnp.float32
BF16 = jnp.bfloat16
I32 = jnp.int32
U32 = jnp.uint32

D_MODEL = 1024
CHUNK = 64
RET_HEADS, RET_DK, RET_DV, RET_THETA = 8, 32, 64, 10000.0
DSA_HEADS, DSA_KV_HEADS, DSA_HD = 8, 2, 64
DSA_ROT = DSA_HD // 4
IDX_HEADS, IDX_DIM = 4, 64
DSA_TOPK_MAX = 256
ROPE_THETA = 500000.0
BAND_HEADS, BAND_HD, BAND_PREV = 8, 64, 8
BAND_PAST = BAND_PREV * CHUNK
REL_CLIP = 256
SSD_HEADS, SSD_HD, SSD_GROUPS, SSD_STATE, SSD_CONV = 8, 64, 2, 128, 4
SSD_INNER = SSD_HEADS * SSD_HD
N_EXPERTS, TOP_K, D_FF = 32, 4, 1024
SWIGLU_LIMIT, SWIGLU_ALPHA = 7.0, 1.702
DEPTH = 2
DN_ALPHA = (2 * DEPTH) ** 0.25
LN_EPS = 1e-5

LANE = 128
VMEM_LIMIT = 56 * 1024 * 1024
INT_MIN = -(2 ** 31)
NEG_BIG = -1e30

EVEN_IN = 2628
EVEN_W = 2688
ODD_IN = 3080
ODD_W = 3200

MOE_TM = 512
MOE_PARTS = 4
RET_LC = 256
SEQ_PAR = 1
SSD_LC = 256
DSA_TQ = 128
DSA_TK = 256
DSA_VT_ROWS = 80
BAND_TQ = 256


def _cparams(n_axes):
    return pltpu.CompilerParams(dimension_semantics=("arbitrary",) * n_axes,
                                vmem_limit_bytes=VMEM_LIMIT)


def _dot(a, b):
    return jnp.dot(a, b, preferred_element_type=F32)


def _dot_nt(a, b):
    return lax.dot_general(a, b, (((1,), (1,)), ((), ())), preferred_element_type=F32)


def _dot_tn(a, b):
    return lax.dot_general(a, b, (((0,), (0,)), ((), ())), preferred_element_type=F32)


def _dot_f32(a, b):
    return jnp.dot(a, b, preferred_element_type=F32, precision=lax.Precision.HIGHEST)


def _layer_norm(x, g, b):
    mu = jnp.mean(x, axis=-1, keepdims=True)
    xc = x - mu
    var = jnp.mean(xc * xc, axis=-1, keepdims=True)
    return xc * lax.rsqrt(var + LN_EPS) * g + b


def _silu(x):
    return x * jax.nn.sigmoid(x)


def _pack_pairs(x):
    c = x.shape[1] // 2
    hi = pltpu.bitcast(x[:, :c].astype(jnp.bfloat16).astype(F32), U32)
    lo = pltpu.bitcast(x[:, c:].astype(jnp.bfloat16).astype(F32), U32)
    return hi | (lo >> 16)


def _unpack_pairs(w):
    return (pltpu.bitcast(w & jnp.uint32(0xFFFF0000), F32), pltpu.bitcast(w << 16, F32))


def _row_specs(parts, tm):
    width = parts[0].shape[1]
    if len(parts) == 1:
        return [pl.BlockSpec((tm, width), lambda i: (i, 0))], parts[0].shape[0] // tm
    n_first = parts[0].shape[0] // tm
    return [pl.BlockSpec((tm, width), lambda i: (jnp.minimum(i, n_first - 1), 0)),
            pl.BlockSpec((tm, width), lambda i: (jnp.maximum(i - n_first, 0), 0))], n_first


def _read_rows(refs, n_first):
    if len(refs) == 1:
        return refs[0][...]
    return jnp.where(pl.program_id(0) < n_first, refs[0][...], refs[1][...])


def _proj_kernel(*refs, n_parts, n_first):
    w_ref, o_ref = refs[n_parts:]
    o_ref[...] = _dot(_read_rows(refs[:n_parts], n_first).astype(BF16), w_ref[...])


def _proj(x_parts, w, tm):
    n = sum(p.shape[0] for p in x_parts)
    k, wd = w.shape
    x_specs, n_first = _row_specs(x_parts, tm)
    return pl.pallas_call(
        functools.partial(_proj_kernel, n_parts=len(x_parts), n_first=n_first),
        grid=(n // tm,),
        in_specs=x_specs + [pl.BlockSpec((k, wd), lambda i: (0, 0))],
        out_specs=pl.BlockSpec((tm, wd), lambda i: (i, 0)),
        out_shape=jax.ShapeDtypeStruct((n, wd), F32),
        compiler_params=_cparams(1),
        name="in_proj",
    )(*x_parts, w)


def _rope_tables(pos, n_heads, d, rot, theta, scale=1.0, pad_to=None):
    half = rot // 2
    inv = theta ** (-jnp.arange(half, dtype=F32) / half)
    ang = pos.astype(F32)[:, None] * inv[None, :]
    cos, sin = jnp.cos(ang), jnp.sin(ang)
    p = pos.shape[0]
    one = jnp.ones((p, d - rot), F32)
    zr = jnp.zeros((p, d - rot), F32)
    zh = jnp.zeros((p, half), F32)
    c = jnp.tile(jnp.concatenate([cos, cos, one], 1), (1, n_heads))
    a = jnp.tile(jnp.concatenate([-sin, zh, zr], 1), (1, n_heads))
    b = jnp.tile(jnp.concatenate([zh, sin, zr], 1), (1, n_heads))
    if pad_to is not None and pad_to > n_heads * d:
        extra = pad_to - n_heads * d
        c = jnp.concatenate([c, jnp.ones((p, extra), F32)], 1)
        a = jnp.concatenate([a, jnp.zeros((p, extra), F32)], 1)
        b = jnp.concatenate([b, jnp.zeros((p, extra), F32)], 1)
    return jnp.stack([c, a, b]) * scale


def _rope(x, tab_ref, half):
    w = x.shape[-1]
    return (x * tab_ref[0] + pltpu.roll(x, w - half, 1) * tab_ref[1]
            + pltpu.roll(x, half, 1) * tab_ref[2])


def _even_prep_kernel(qa_ref, ka_ref, qb_ref, kb_ref, iq_ref, ikw_ref, v_ref,
                      tq_ref, tk_ref, td_ref, ti_ref,
                      qa_o, ka_o, qb_o, kb_o, iq_o, ikw_o, qst_o, iqst_o, khm_o, ikb_o, vt_o, iwt_o,
                      k5p_o, v5p_o, xp_o, k5s_o, v5s_o, xs_o, *, n_prompt_blocks):
    tm = qa_ref.shape[0]
    h = DSA_ROT // 2
    qa_o[...] = _rope(qa_ref[...], tq_ref, RET_DK // 2)
    ka_o[...] = _rope(ka_ref[...], tk_ref, RET_DK // 2)
    qb = (_rope(qb_ref[...], td_ref, h) * (DSA_HD ** -0.5 * math.log2(math.e))).astype(BF16)
    qb_o[...] = qb
    kb = kb_ref[...]
    kb = (kb * td_ref[0, :, :LANE] + pltpu.roll(kb, LANE - h, 1) * td_ref[1, :, :LANE]
          + pltpu.roll(kb, h, 1) * td_ref[2, :, :LANE])
    kb_o[...] = kb
    iq = iq_ref[...]
    w = iq.shape[-1]
    iq = (iq * td_ref[0, :, :w] + pltpu.roll(iq, w - h, 1) * td_ref[1, :, :w]
          + pltpu.roll(iq, h, 1) * td_ref[2, :, :w]).astype(BF16)
    iq_o[...] = iq
    ikw = _rope(ikw_ref[...], ti_ref, h)
    ikw_o[...] = ikw
    group = DSA_HEADS // DSA_KV_HEADS
    for jb in range(tm // DSA_TQ):
        rs = slice(jb * DSA_TQ, (jb + 1) * DSA_TQ)
        for hd in range(DSA_HEADS):
            n, g = divmod(hd, group)
            ro = (jb * group + g) * DSA_TQ
            qst_o[n, ro:ro + DSA_TQ, :] = qb[rs, hd * DSA_HD:(hd + 1) * DSA_HD]
        for hd in range(IDX_HEADS):
            ro = (jb * IDX_HEADS + hd) * DSA_TQ
            iqst_o[ro:ro + DSA_TQ, :] = iq[rs, hd * IDX_DIM:(hd + 1) * IDX_DIM]
    kbb = kb.astype(BF16)
    for n in range(DSA_KV_HEADS):
        khm_o[n] = kbb[:, n * DSA_HD:(n + 1) * DSA_HD]
    ikb_o[...] = ikw[:, :IDX_DIM].astype(BF16)
    v = v_ref[...]
    ones = jnp.ones((DSA_VT_ROWS - DSA_HD, DSA_TK), F32)
    for j in range(tm // DSA_TK):
        vt = v[j * DSA_TK:(j + 1) * DSA_TK, :].T
        vt_o[j] = jnp.concatenate([vt[:DSA_HD], ones, vt[DSA_HD:], ones], axis=0).astype(BF16)
    iwt_o[...] = ikw.T[IDX_DIM:IDX_DIM + 8, :]

    def cache_rows(k_o, v_o, x_o):
        for n in range(DSA_KV_HEADS):
            k_o[:, n, :] = kb[:, n * DSA_HD:(n + 1) * DSA_HD]
            v_o[:, n, :] = v[:, n * DSA_HD:(n + 1) * DSA_HD]
        x_o[...] = ikw[:, :IDX_DIM]

    @pl.when(pl.program_id(0) < n_prompt_blocks)
    def _():
        cache_rows(k5p_o, v5p_o, xp_o)

    @pl.when(pl.program_id(0) >= n_prompt_blocks)
    def _():
        cache_rows(k5s_o, v5s_o, xs_o)


def _even_prep(pe, tabs, tm, n_prompt_blocks, tab_blocks):
    n = pe.shape[0]
    tq, tk, td, ti = tabs

    def tix(i):
        return (0, jnp.where(i < n_prompt_blocks, i % tab_blocks, tab_blocks), 0)

    def col(wd, j):
        return pl.BlockSpec((tm, wd), lambda i: (i, j))

    def tab(wd):
        return pl.BlockSpec((3, tm, wd), tix)

    def out(wd):
        return pl.BlockSpec((tm, wd), lambda i: (i, 0))

    group = DSA_HEADS // DSA_KV_HEADS
    n_prompt = n_prompt_blocks * tm
    cache_p = pl.BlockSpec((tm, DSA_KV_HEADS, DSA_HD),
                           lambda i: (jnp.minimum(i, n_prompt_blocks - 1), 0, 0))
    cache_s = pl.BlockSpec((tm, DSA_KV_HEADS, DSA_HD),
                           lambda i: (jnp.maximum(i - n_prompt_blocks, 0), 0, 0))
    kidx_p = pl.BlockSpec((tm, IDX_DIM), lambda i: (jnp.minimum(i, n_prompt_blocks - 1), 0))
    kidx_s = pl.BlockSpec((tm, IDX_DIM), lambda i: (jnp.maximum(i - n_prompt_blocks, 0), 0))
    return pl.pallas_call(
        functools.partial(_even_prep_kernel, n_prompt_blocks=n_prompt_blocks),
        grid=(n // tm,),
        in_specs=[col(256, 0), col(256, 1), col(512, 3), col(128, 16), col(256, 9), col(128, 20),
                  col(128, 17), tab(256), tab(256), tab(512), tab(128)],
        out_specs=[out(256), out(256), out(512), out(128), out(256), out(128),
                   pl.BlockSpec((DSA_KV_HEADS, group * tm, DSA_HD), lambda i: (0, i, 0)),
                   pl.BlockSpec((IDX_HEADS * tm, IDX_DIM), lambda i: (i, 0)),
                   pl.BlockSpec((DSA_KV_HEADS, tm, DSA_HD), lambda i: (0, i, 0)),
                   pl.BlockSpec((tm, IDX_DIM), lambda i: (i, 0)),
                   pl.BlockSpec((tm // DSA_TK, 2 * DSA_VT_ROWS, DSA_TK), lambda i: (i, 0, 0)),
                   pl.BlockSpec((8, tm), lambda i: (0, i)),
                   cache_p, cache_p, kidx_p, cache_s, cache_s, kidx_s],
        out_shape=[jax.ShapeDtypeStruct((n, 256), F32), jax.ShapeDtypeStruct((n, 256), F32),
                   jax.ShapeDtypeStruct((n, 512), BF16), jax.ShapeDtypeStruct((n, 128), F32),
                   jax.ShapeDtypeStruct((n, 256), BF16), jax.ShapeDtypeStruct((n, 128), F32),
                   jax.ShapeDtypeStruct((DSA_KV_HEADS, group * n, DSA_HD), BF16),
                   jax.ShapeDtypeStruct((IDX_HEADS * n, IDX_DIM), BF16),
                   jax.ShapeDtypeStruct((DSA_KV_HEADS, n, DSA_HD), BF16),
                   jax.ShapeDtypeStruct((n, IDX_DIM), BF16),
                   jax.ShapeDtypeStruct((n // DSA_TK, 2 * DSA_VT_ROWS, DSA_TK), BF16),
                   jax.ShapeDtypeStruct((8, n), F32),
                   jax.ShapeDtypeStruct((n_prompt, DSA_KV_HEADS, DSA_HD), F32),
                   jax.ShapeDtypeStruct((n_prompt, DSA_KV_HEADS, DSA_HD), F32),
                   jax.ShapeDtypeStruct((n_prompt, IDX_DIM), F32),
                   jax.ShapeDtypeStruct((n - n_prompt, DSA_KV_HEADS, DSA_HD), F32),
                   jax.ShapeDtypeStruct((n - n_prompt, DSA_KV_HEADS, DSA_HD), F32),
                   jax.ShapeDtypeStruct((n - n_prompt, IDX_DIM), F32)],
        compiler_params=_cparams(1),
        name="even_rope",
    )(pe, pe, pe, pe, pe, pe, pe, tq, tk, td, ti)


def _ret_kernel(gch_ref, *refs, n_par):
    seq_refs = [refs[4 * s:4 * s + 4] for s in range(n_par)]
    dm_ref, qd_ref, kd_ref, gng_ref, gnb_ref, avg_ref, s0_ref, o_ref, s_ref, s_sc, y_sc = refs[4 * n_par:]
    c = pl.program_id(1)

    @pl.when(c == 0)
    def _():
        s_sc[...] = s0_ref[...]

    for s, (q_ref, k_ref, v_ref, g_ref) in enumerate(seq_refs):
        q = q_ref[...]
        k = k_ref[...]
        qx = (q * qd_ref[...]).astype(BF16)
        kw = (k * kd_ref[...]).astype(BF16)
        qb = q.astype(BF16)
        kb = k.astype(BF16)
        vb = v_ref[...].astype(BF16)
        gate = g_ref[...]
        for h in range(RET_HEADS):
            ks = slice(h * RET_DK, (h + 1) * RET_DK)
            vs = slice(h * RET_DV, (h + 1) * RET_DV)
            att = _dot_nt(qb[:, ks], kb[:, ks]) * dm_ref[h]
            s_old = s_sc[s, h]
            y_sc[:, vs] = _dot(att.astype(BF16), vb[:, vs]) + _dot(qx[:, ks], s_old.astype(BF16))
            s_sc[s, h] = s_old * gch_ref[h] + _dot_tn(kw[:, ks], vb[:, vs])
        y = y_sc[...]
        mu = _dot(y.astype(BF16), avg_ref[...])
        d = y - mu
        var = _dot((d * d).astype(BF16), avg_ref[...])
        yn = d * lax.rsqrt(var + LN_EPS) * gng_ref[...] + gnb_ref[...]
        o_ref[0, s] = _silu(gate) * yn

    @pl.when(c == pl.num_programs(1) - 1)
    def _():
        s_ref[...] = s_sc[...]


def _retention(qa, ka, pe, s0, gn_g, gn_b, n_seq, t, row0):
    n_par = SEQ_PAR
    lc = min(RET_LC, t)
    nc = t // lc
    blk0 = row0 // lc
    log_g = jnp.log(1.0 - 2.0 ** (-5.0 - jnp.arange(RET_HEADS, dtype=F32)))
    pos = jnp.arange(lc, dtype=F32)
    diff = pos[:, None] - pos[None, :]
    dmask = jnp.where(diff >= 0, jnp.exp(jnp.maximum(diff, 0.0)[None] * log_g[:, None, None]), 0.0)
    w_end = jnp.exp((lc - 1 - pos)[:, None] * log_g[None, :])
    xi = jnp.exp((pos + 1.0)[:, None] * log_g[None, :])
    kdec = jnp.repeat(w_end, RET_DK, axis=1)
    qdec = jnp.repeat(xi, RET_DK, axis=1)
    gch = jnp.exp(lc * log_g)
    head = np.arange(512) // RET_DV
    avg = jnp.asarray((head[:, None] == head[None, :]) / RET_DV, BF16)

    def rows(s, wd, j):
        return pl.BlockSpec((lc, wd), lambda b, c: (blk0 + (b * n_par + s) * nc + c, j))

    def const(shape):
        nd = len(shape)
        return pl.BlockSpec(shape, lambda b, c: (0,) * nd)

    seq_specs, seq_args = [], []
    for s in range(n_par):
        seq_specs += [rows(s, 256, 0), rows(s, 256, 0), rows(s, 512, 1), rows(s, 512, 2)]
        seq_args += [qa, ka, pe, pe]
    state = pl.BlockSpec((n_par, RET_HEADS, RET_DK, RET_DV), lambda b, c: (b, 0, 0, 0))
    y, s_last = pl.pallas_call(
        functools.partial(_ret_kernel, n_par=n_par),
        grid=(n_seq // n_par, nc),
        in_specs=[pl.BlockSpec(memory_space=pltpu.SMEM)] + seq_specs + [
            const((RET_HEADS, lc, lc)), const((lc, 256)), const((lc, 256)),
            const((1, 512)), const((1, 512)), const((512, 512)), state],
        out_specs=[pl.BlockSpec((1, n_par, lc, 512), lambda b, c: (b, 0, c, 0)), state],
        out_shape=[jax.ShapeDtypeStruct((n_seq // n_par, n_par, t, 512), F32),
                   jax.ShapeDtypeStruct((n_seq, RET_HEADS, RET_DK, RET_DV), F32)],
        scratch_shapes=[pltpu.VMEM((n_par, RET_HEADS, RET_DK, RET_DV), F32),
                        pltpu.VMEM((lc, 512), F32)],
        compiler_params=_cparams(2),
        name="retention",
    )(gch, *seq_args, dmask, qdec, kdec, gn_g.reshape(1, 512), gn_b.reshape(1, 512), avg, s0)
    return y.reshape(n_seq * t, 512), s_last


def _col_reduce(x, op):
    r, c = x.shape
    return op(op(x.reshape(r // 8, 8, c), axis=0), axis=0, keepdims=True)


def _dsa_kernel(nkb_ref, q_ref, iq_ref, iwt_ref, qlim_ref, k_ref, vt_ref, ik_ref, *rest,
                topk, tq_out):
    o_ref, key_sc, m_sc, l_sc, acc_sc, lga_sc, lgb_sc = rest
    nkb = nkb_ref[pl.program_id(1)]
    tq = qlim_ref.shape[1]
    tk = key_sc.shape[1]
    group = DSA_HEADS // DSA_KV_HEADS
    qlim = qlim_ref[...]
    iwt = iwt_ref[...] * ((IDX_HEADS * IDX_DIM) ** -0.5)
    iqs = iq_ref[...]
    krow = lax.broadcasted_iota(I32, (tk, tq), 0)

    def score_body(kb, carry):
        off = pl.multiple_of(kb * tk, tk)
        s_all = _dot_nt(ik_ref[pl.ds(off, tk), :], iqs)
        s = jnp.zeros((tk, tq), F32)
        for h in range(IDX_HEADS):
            s = s + iwt[h:h + 1, :] * jnp.maximum(s_all[:, h * tq:(h + 1) * tq], 0.0)
        s = jnp.where(s == 0.0, 0.0, s)
        bits = pltpu.bitcast(s, I32)
        key = jnp.where(bits >= 0, bits, bits ^ jnp.int32(0x7FFFFFFF))
        adm = (off + krow) < qlim
        key_sc[kb] = jnp.where(adm, key, jnp.int32(INT_MIN))
        return carry

    n_pair = (nkb + 1) // 2
    last = 2 * n_pair - 1

    def score_pair(j, carry):
        return score_body(2 * j + 1, score_body(2 * j, carry))

    lax.fori_loop(0, n_pair, score_pair, 0)

    def count(pred):
        def body(kb, acc):
            hit = jnp.where(pred(key_sc[kb]), 1.0, 0.0)
            return acc + jnp.sum(hit.reshape(tk // 64, 64, tq), axis=0)
        acc = lax.fori_loop(0, nkb, body, jnp.zeros((64, tq), F32))
        return jnp.sum(acc, axis=0, keepdims=True)

    def bit_body(it, ans):
        cand = ans + (jnp.int32(1) << (31 - it))
        return jnp.where(count(lambda k: k >= cand) >= topk, cand, ans)

    t = lax.fori_loop(0, 32, bit_body, jnp.full((1, tq), INT_MIN, I32))
    need = topk - count(lambda k: k > t)

    m_sc[...] = jnp.full(m_sc.shape, 0.1 * NEG_BIG, F32)
    l_sc[...] = jnp.zeros(l_sc.shape, F32)
    acc_sc[...] = jnp.zeros(acc_sc.shape, F32)
    r_i = lax.broadcasted_iota(I32, (tk, tk), 0)
    c_i = lax.broadcasted_iota(I32, (tk, tk), 1)
    lower = (c_i < r_i).astype(BF16)

    def logits_stage(kb, n_eq, dst):
        off = pl.multiple_of(kb * tk, tk)
        key = key_sc[kb]
        adm = (off + krow) < qlim
        eq = jnp.logical_and(key == t, adm)
        eqf = jnp.where(eq, 1.0, 0.0)
        pref = _dot(lower, eqf.astype(BF16))
        sel = jnp.logical_and(adm, jnp.logical_or(
            key > t, jnp.logical_and(eq, (n_eq + pref) < need)))
        bias = jnp.where(sel, 0.0, NEG_BIG)
        for n in range(DSA_KV_HEADS):
            lg_all = _dot_nt(k_ref[n, pl.ds(off, tk), :], q_ref[n])
            for g in range(group):
                ls = slice(g * tq, (g + 1) * tq)
                dst[n, :, ls] = lg_all[:, ls] + bias
        return n_eq + _col_reduce(eqf, jnp.sum)

    def softmax_stage(kb, src):
        vt = vt_ref[kb]
        for n in range(DSA_KV_HEADS):
            ps, alphas = [], []
            for g in range(group):
                ls = slice(g * tq, (g + 1) * tq)
                lg = src[n, :, ls]
                m_old = m_sc[n, :, ls]
                m_new = jnp.maximum(m_old, _col_reduce(lg, jnp.max))
                m_sc[n, :, ls] = m_new
                ps.append(jnp.exp2(lg - m_new).astype(BF16))
                alphas.append(jnp.exp2(m_old - m_new))
            p_all = jnp.concatenate(ps, axis=1)
            alpha_all = jnp.concatenate(alphas, axis=1)
            pv = _dot(vt[n * DSA_VT_ROWS:(n + 1) * DSA_VT_ROWS, :], p_all)
            acc_sc[n] = alpha_all * acc_sc[n] + pv[:DSA_HD]
            l_sc[n] = alpha_all * l_sc[n] + pv[DSA_HD:DSA_HD + 1]

    def pair_body(j, n_eq):
        kb0 = 2 * j
        softmax_stage(kb0, lga_sc)
        n_eq = logits_stage(kb0 + 1, n_eq, lgb_sc)
        softmax_stage(kb0 + 1, lgb_sc)
        return logits_stage(jnp.minimum(kb0 + 2, last), n_eq, lga_sc)

    lax.fori_loop(0, n_pair, pair_body, logits_stage(0, jnp.zeros((1, tq), F32), lga_sc))
    pieces = []
    for n in range(DSA_KV_HEADS):
        o_n = acc_sc[n] / l_sc[n]
        for g in range(group):
            pieces.append(o_n[:, g * tq:(g + 1) * tq])
    o_ref[...] = jnp.concatenate(pieces, axis=0).T[:tq_out, :]


def _dsa(q_st, iq_st, iw_t, qlim, nkb, k_hm, v_t, ik_bf, n_seq, nq, s_len, tq_out, topk):
    tq = DSA_TQ
    group = DSA_HEADS // DSA_KV_HEADS
    in_specs = [pl.BlockSpec((DSA_KV_HEADS, group * tq, DSA_HD), lambda b, i, s: (0, b * nq + i, 0)),
                pl.BlockSpec((IDX_HEADS * tq, IDX_DIM), lambda b, i, s: (b * nq + i, 0)),
                pl.BlockSpec((8, tq), lambda b, i, s: (0, b * nq + i)),
                pl.BlockSpec((1, tq), lambda b, i, s: (0, i)),
                pl.BlockSpec((DSA_KV_HEADS, s_len, DSA_HD), lambda b, i, s: (0, b, 0)),
                pl.BlockSpec((s_len // DSA_TK, 2 * DSA_VT_ROWS, DSA_TK), lambda b, i, s: (b, 0, 0)),
                pl.BlockSpec((s_len, IDX_DIM), lambda b, i, s: (b, 0))]
    grid_spec = pltpu.PrefetchScalarGridSpec(
        num_scalar_prefetch=1,
        grid=(n_seq, nq),
        in_specs=in_specs,
        out_specs=pl.BlockSpec((tq_out, 512), lambda b, i, s: (b * nq + i, 0)),
        scratch_shapes=[pltpu.VMEM((s_len // DSA_TK, DSA_TK, tq), I32),
                        pltpu.VMEM((DSA_KV_HEADS, 1, group * tq), F32),
                        pltpu.VMEM((DSA_KV_HEADS, 1, group * tq), F32),
                        pltpu.VMEM((DSA_KV_HEADS, DSA_HD, group * tq), F32),
                        pltpu.VMEM((DSA_KV_HEADS, DSA_TK, group * tq), F32),
                        pltpu.VMEM((DSA_KV_HEADS, DSA_TK, group * tq), F32)])
    return pl.pallas_call(
        functools.partial(_dsa_kernel, topk=topk, tq_out=tq_out),
        grid_spec=grid_spec,
        out_shape=jax.ShapeDtypeStruct((n_seq * nq * tq_out, 512), F32),
        compiler_params=_cparams(2),
        name="dsa_attention",
    )(nkb, q_st, iq_st, iw_t, qlim, k_hm, v_t, ik_bf)


def _outproj_ln_kernel(*refs, counts, n_first):
    na, nb, nh = counts
    ya = _read_rows(refs[:na], n_first)
    yb = _read_rows(refs[na:na + nb], n_first)
    w_ref = refs[na + nb]
    h = _read_rows(refs[na + nb + 1:na + nb + 1 + nh], n_first)
    g_ref, b_ref, o_ref, op_ref = refs[na + nb + 1 + nh:]
    half = ya.shape[1]
    y = _dot(ya.astype(BF16), w_ref[:half, :]) + _dot(yb.astype(BF16), w_ref[half:, :])
    out = _layer_norm(DN_ALPHA * h + y, g_ref[...], b_ref[...])
    o_ref[...] = out
    op_ref[...] = _pack_pairs(out)


def _outproj_ln(ya_parts, yb_parts, w, h_parts, g, b, tm, n_first):
    n = sum(p.shape[0] for p in h_parts)
    specs = []
    for parts in (ya_parts, yb_parts):
        sp, nf = _row_specs(parts, tm)
        assert len(parts) == 1 or nf == n_first
        specs += sp
    h_specs, nf = _row_specs(h_parts, tm)
    assert len(h_parts) == 1 or nf == n_first
    return pl.pallas_call(
        functools.partial(_outproj_ln_kernel, counts=(len(ya_parts), len(yb_parts), len(h_parts)),
                          n_first=n_first),
        grid=(n // tm,),
        in_specs=specs + [pl.BlockSpec((D_MODEL, D_MODEL), lambda i: (0, 0))] + h_specs + [
                  pl.BlockSpec((1, D_MODEL), lambda i: (0, 0)),
                  pl.BlockSpec((1, D_MODEL), lambda i: (0, 0))],
        out_specs=[pl.BlockSpec((tm, D_MODEL), lambda i: (i, 0)),
                   pl.BlockSpec((tm, D_MODEL // 2), lambda i: (i, 0))],
        out_shape=[jax.ShapeDtypeStruct((n, D_MODEL), F32),
                   jax.ShapeDtypeStruct((n, D_MODEL // 2), U32)],
        compiler_params=_cparams(1),
        name="out_proj_ln",
    )(*ya_parts, *yb_parts, w, *h_parts, g.reshape(1, -1), b.reshape(1, -1))


def _router_kernel(x_ref, w_ref, b_ref, idx_o, gate_o, rank_o, cnt_o, cnt_sc):
    i = pl.program_id(0)

    @pl.when(i == 0)
    def _():
        cnt_sc[...] = jnp.zeros(cnt_sc.shape, F32)

    tm = x_ref.shape[0]
    ne = w_ref.shape[0]
    x = x_ref[...]
    w = w_ref[...]
    x_hi = x.astype(BF16)
    x_lo = (x - x_hi.astype(F32)).astype(BF16)
    w_hi = w.astype(BF16)
    w_lo = (w - w_hi.astype(F32)).astype(BF16)
    logits = (_dot_nt(w_hi, x_hi) + _dot_nt(w_lo, x_hi) + _dot_nt(w_hi, x_lo)) + b_ref[...]
    erow = lax.broadcasted_iota(I32, (ne, tm), 0)
    vals, idxs = [], []
    onehot = jnp.zeros((ne, tm), F32)
    for _ in range(TOP_K):
        m = jnp.max(logits, axis=0, keepdims=True)
        ix = jnp.min(jnp.where(logits == m, erow, ne), axis=0, keepdims=True)
        hit = erow == ix
        onehot = jnp.where(hit, 1.0, onehot)
        logits = jnp.where(hit, -jnp.inf, logits)
        vals.append(m)
        idxs.append(ix)
    es = [jnp.exp(v - vals[0]) for v in vals]
    den = es[0] + es[1] + es[2] + es[3]
    r_i = lax.broadcasted_iota(I32, (tm, tm), 0)
    c_i = lax.broadcasted_iota(I32, (tm, tm), 1)
    upper = (r_i < c_i).astype(BF16)
    rank_dense = _dot(onehot.astype(BF16), upper) + cnt_sc[...]
    prow = lax.broadcasted_iota(I32, (8, tm), 0)
    idx_out = jnp.zeros((8, tm), I32)
    gate_out = jnp.zeros((8, tm), F32)
    rank_out = jnp.zeros((8, tm), F32)
    for k in range(TOP_K):
        rk = jnp.sum(jnp.where(erow == idxs[k], rank_dense, 0.0), axis=0, keepdims=True)
        idx_out = jnp.where(prow == k, idxs[k], idx_out)
        gate_out = jnp.where(prow == k, es[k] / den, gate_out)
        rank_out = jnp.where(prow == k, rk, rank_out)
    idx_o[...] = idx_out
    gate_o[...] = gate_out
    rank_o[...] = rank_out.astype(I32)
    cnt = cnt_sc[...] + jnp.sum(onehot, axis=1, keepdims=True)
    cnt_sc[...] = cnt
    cnt_o[...] = cnt


def _router(x, w_r, b_r, tm):
    n = x.shape[0]
    row = pl.BlockSpec((8, tm), lambda i: (0, i))
    return pl.pallas_call(
        _router_kernel,
        grid=(n // tm,),
        in_specs=[pl.BlockSpec((tm, D_MODEL), lambda i: (i, 0)),
                  pl.BlockSpec((N_EXPERTS, D_MODEL), lambda i: (0, 0)),
                  pl.BlockSpec((N_EXPERTS, 1), lambda i: (0, 0))],
        out_specs=[row, row, row, pl.BlockSpec((N_EXPERTS, 1), lambda i: (0, 0))],
        out_shape=[jax.ShapeDtypeStruct((8, n), I32), jax.ShapeDtypeStruct((8, n), F32),
                   jax.ShapeDtypeStruct((8, n), I32), jax.ShapeDtypeStruct((N_EXPERTS, 1), F32)],
        scratch_shapes=[pltpu.VMEM((N_EXPERTS, 1), F32)],
        compiler_params=_cparams(1),
        name="moe_router",
    )(x, w_r.T, b_r.reshape(N_EXPERTS, 1))


def _moe_kernel(be_ref, nu_ref, *rest, n_parts, blocks_per_part):
    x_refs = rest[:n_parts]
    wgu_ref, bgu_ref, wdn_ref, bdn_ref, o_ref, wgu_sc, wdn_sc = rest[n_parts:]
    i = pl.program_id(0)

    @pl.when(jnp.logical_or(i == 0, be_ref[i] != be_ref[jnp.maximum(i - 1, 0)]))
    def _():
        wgu_sc[...] = wgu_ref[0, 0].astype(BF16)
        wdn_sc[...] = wdn_ref[0, 0].astype(BF16)

    @pl.when(i < nu_ref[0])
    def _():
        part = i // blocks_per_part
        xw = x_refs[0][...]
        for c in range(1, n_parts):
            xw = jnp.where(part == c, x_refs[c][...], xw)
        xa, xb = _unpack_pairs(xw)
        x = jnp.concatenate([xa.astype(BF16), xb.astype(BF16)], axis=1)
        h = _dot(x, wgu_sc[...]) + bgu_ref[0, 0]
        g = jnp.minimum(h[:, :D_FF], SWIGLU_LIMIT)
        up = jnp.clip(h[:, D_FF:], -SWIGLU_LIMIT, SWIGLU_LIMIT)
        a = (up + 1.0) * g * jax.nn.sigmoid(SWIGLU_ALPHA * g)
        o_ref[...] = _pack_pairs(_dot(a.astype(BF16), wdn_sc[...]) + bdn_ref[0, 0])

    @pl.when(i >= nu_ref[0])
    def _():
        o_ref[...] = jnp.zeros(o_ref.shape, U32)


def _moe_experts(xs_parts, blk_e, n_used, layer, w_gu, b_gu, w_dn, b_dn):
    n_parts = len(xs_parts)
    tm = MOE_TM
    bpp = xs_parts[0].shape[0] // tm
    n_rows = n_parts * bpp * tm
    depth = w_gu.shape[0]

    def x_spec(c):
        return pl.BlockSpec((tm, D_MODEL // 2),
                            lambda i, be, nu: (jnp.clip(i - c * bpp, 0, bpp - 1), 0))
    grid_spec = pltpu.PrefetchScalarGridSpec(
        num_scalar_prefetch=2,
        grid=(n_rows // tm,),
        in_specs=[x_spec(c) for c in range(n_parts)] + [
                  pl.BlockSpec((1, 1, D_MODEL, 2 * D_FF), lambda i, be, nu: (layer, be[i], 0, 0)),
                  pl.BlockSpec((1, 1, 1, 2 * D_FF), lambda i, be, nu: (layer, be[i], 0, 0)),
                  pl.BlockSpec((1, 1, D_FF, D_MODEL), lambda i, be, nu: (layer, be[i], 0, 0)),
                  pl.BlockSpec((1, 1, 1, D_MODEL), lambda i, be, nu: (layer, be[i], 0, 0))],
        out_specs=pl.BlockSpec((tm, D_MODEL // 2), lambda i, be, nu: (i, 0)),
        scratch_shapes=[pltpu.VMEM((D_MODEL, 2 * D_FF), BF16), pltpu.VMEM((D_FF, D_MODEL), BF16)])
    return pl.pallas_call(
        functools.partial(_moe_kernel, n_parts=n_parts, blocks_per_part=bpp),
        grid_spec=grid_spec,
        out_shape=jax.ShapeDtypeStruct((n_rows, D_MODEL // 2), U32),
        compiler_params=_cparams(1),
        name="moe_experts",
    )(blk_e, n_used, *xs_parts, w_gu, b_gu.reshape(depth, N_EXPERTS, 1, -1), w_dn,
      b_dn.reshape(depth, N_EXPERTS, 1, -1))


def _combine_ln_kernel(h_ref, y0_ref, y1_ref, y2_ref, y3_ref, gate_ref, g_ref, b_ref, *o_refs,
                       n_first):
    gate = gate_ref[...]
    ya, yb = None, None
    for k, y_ref in enumerate((y0_ref, y1_ref, y2_ref, y3_ref)):
        a, b = _unpack_pairs(y_ref[...])
        gk = gate[:, k:k + 1]
        ya = gk * a if ya is None else ya + gk * a
        yb = gk * b if yb is None else yb + gk * b
    y = jnp.concatenate([ya, yb], axis=1)
    out = _layer_norm(DN_ALPHA * h_ref[...] + y, g_ref[...], b_ref[...])
    if len(o_refs) == 1:
        o_refs[0][...] = out
    else:
        @pl.when(pl.program_id(0) < n_first)
        def _():
            o_refs[0][...] = out

        @pl.when(pl.program_id(0) >= n_first)
        def _():
            o_refs[1][...] = out


def _combine_ln(h, ys, gate, g, b, tm, split_rows=None):
    n = h.shape[0]
    row = pl.BlockSpec((tm, D_MODEL), lambda i: (i, 0))
    half = pl.BlockSpec((tm, D_MODEL // 2), lambda i: (i, 0))
    vec = pl.BlockSpec((1, D_MODEL), lambda i: (0, 0))
    if split_rows is None:
        n_first = n // tm
        out_specs = row
        out_shape = jax.ShapeDtypeStruct((n, D_MODEL), F32)
    else:
        n_first = split_rows // tm
        out_specs = [pl.BlockSpec((tm, D_MODEL), lambda i: (jnp.minimum(i, n_first - 1), 0)),
                     pl.BlockSpec((tm, D_MODEL), lambda i: (jnp.maximum(i - n_first, 0), 0))]
        out_shape = [jax.ShapeDtypeStruct((split_rows, D_MODEL), F32),
                     jax.ShapeDtypeStruct((n - split_rows, D_MODEL), F32)]
    return pl.pallas_call(
        functools.partial(_combine_ln_kernel, n_first=n_first),
        grid=(n // tm,),
        in_specs=[row, half, half, half, half, pl.BlockSpec((tm, TOP_K), lambda i: (i, 0)), vec, vec],
        out_specs=out_specs,
        out_shape=out_shape,
        compiler_params=_cparams(1),
        name="moe_combine_ln",
    )(h, ys[0], ys[1], ys[2], ys[3], gate, g.reshape(1, -1), b.reshape(1, -1))


def _rows(x, idx):
    return x.at[idx].get(mode="promise_in_bounds")


def _moe_layer(h, h_packed, layer, w_r, b_r, w_gu, b_gu, w_dn, b_dn, ln_g, ln_b, tm, split_rows=None):
    n = h.shape[0]
    n_pair = n * TOP_K
    idx_t, gate_t, rank_t, cnt = _router(h, w_r, b_r, tm)
    top_i = idx_t[:TOP_K].T
    counts = cnt[:, 0].astype(I32)
    padded = (counts + MOE_TM - 1) // MOE_TM * MOE_TM
    pad_end = jnp.cumsum(padded)
    start = pad_end - padded
    first = jnp.cumsum(counts) - counts
    dest = _rows(start, top_i) + rank_t[:TOP_K].T
    n_blk = -(-(-(-n_pair // MOE_TM) + N_EXPERTS) // MOE_PARTS) * MOE_PARTS
    n_used = (pad_end[-1] // MOE_TM).astype(I32)
    blk_row = jnp.minimum(jnp.arange(n_blk, dtype=I32), n_used - 1) * MOE_TM
    blk_e = jnp.sum((pad_end[None, :] <= blk_row[:, None]).astype(I32), axis=1)
    blk_e = jnp.minimum(blk_e, N_EXPERTS - 1)
    order = jnp.argsort(top_i.reshape(-1), stable=True).astype(I32)
    row_in_e = (jnp.arange(n_blk, dtype=I32)[:, None] * MOE_TM - _rows(start, blk_e)[:, None]
                + jnp.arange(MOE_TM, dtype=I32)[None, :])
    pair = jnp.clip(_rows(first, blk_e)[:, None] + row_in_e, 0, n_pair - 1).reshape(-1)
    src = (_rows(order, pair) // TOP_K).reshape(MOE_PARTS, -1)
    xs_parts = [_rows(h_packed, src[c]) for c in range(MOE_PARTS)]
    ybuf = _moe_experts(xs_parts, blk_e, n_used.reshape(1), layer, w_gu, b_gu, w_dn, b_dn)
    ys = [_rows(ybuf, dest[:, k]) for k in range(TOP_K)]
    return _combine_ln(h, ys, gate_t[:TOP_K].T, ln_g, ln_b, tm, split_rows)


def _band_kernel(q_ref, *rest, nkb):
    k_refs = rest[:nkb]
    v_refs = rest[nkb:2 * nkb]
    bias_ref = rest[2 * nkb]
    o_ref = rest[-1]
    c = pl.program_id(1)
    tq = q_ref.shape[0]
    tkb = k_refs[0].shape[0]
    q_scale = BAND_HD ** -0.5 * math.log2(math.e)
    kbs = [r[...].astype(BF16) for r in k_refs]
    vbs = [r[...].astype(BF16) for r in v_refs]
    off = [jnp.where((c + j - (nkb - 1)) >= 0, 0.0, NEG_BIG) for j in range(nkb)]
    low = lax.broadcasted_iota(I32, (tq, LANE), 1) < BAND_HD
    for pair in range(BAND_HEADS // 2):
        ls = slice(pair * LANE, (pair + 1) * LANE)
        qp = q_ref[:, ls] * q_scale
        outs = []
        for par in range(2):
            h = 2 * pair + par
            qh = jnp.where(low == (par == 0), qp, 0.0).astype(BF16)
            lgs = [_dot_nt(qh, kbs[j][:, ls]) + (bias_ref[h, :, j * tkb:(j + 1) * tkb] + off[j])
                   for j in range(nkb)]
            mx = lgs[0]
            for j in range(1, nkb):
                mx = jnp.maximum(mx, lgs[j])
            m = mx.max(axis=1, keepdims=True)
            ps = [jnp.exp2(lg - m) for lg in lgs]
            sm = ps[0]
            for j in range(1, nkb):
                sm = sm + ps[j]
            den = sm.sum(axis=1, keepdims=True)
            acc = _dot(ps[0].astype(BF16), vbs[0][:, ls])
            for j in range(1, nkb):
                acc = acc + _dot(ps[j].astype(BF16), vbs[j][:, ls])
            outs.append(acc * (1.0 / den))
        o_ref[:, ls] = jnp.where(low, outs[0], outs[1])


def _band(q_arr, k_arr, v_arr, cols, bias, n_seq, t, tq, tkb, nkb, q_row0, kv_blocks_per_seq):
    nq = t // tq
    qblk0 = q_row0 // tq
    qcol, kcol, vcol = cols

    def kv_spec(j, col):
        def ix(b, c):
            return (b * kv_blocks_per_seq + jnp.maximum(c + j - (nkb - 1), 0), col)
        return pl.BlockSpec((tkb, 512), ix)

    in_specs = ([pl.BlockSpec((tq, 512), lambda b, c: (qblk0 + b * nq + c, qcol))]
                + [kv_spec(j, kcol) for j in range(nkb)]
                + [kv_spec(j, vcol) for j in range(nkb)]
                + [pl.BlockSpec(bias.shape, lambda b, c: (0, 0, 0))])
    return pl.pallas_call(
        functools.partial(_band_kernel, nkb=nkb),
        grid=(n_seq, nq),
        in_specs=in_specs,
        out_specs=pl.BlockSpec((tq, 512), lambda b, c: (b * nq + c, 0)),
        out_shape=jax.ShapeDtypeStruct((n_seq * t, 512), F32),
        compiler_params=_cparams(2),
        name="band_attention",
    )(q_arr, *([k_arr] * nkb), *([v_arr] * nkb), bias)


def _head_rows_kernel(src_ref, k_ref, v_ref, ko_ref, vo_ref):
    for h in range(BAND_HEADS):
        hs = slice(h * BAND_HD, (h + 1) * BAND_HD)
        ko_ref[:, h, :] = k_ref[:, hs]
        vo_ref[:, h, :] = v_ref[:, hs]


def _head_rows(po, src_blocks, tmb):
    n_blocks = src_blocks.shape[0]
    out = jax.ShapeDtypeStruct((n_blocks * tmb, BAND_HEADS, BAND_HD), F32)
    grid_spec = pltpu.PrefetchScalarGridSpec(
        num_scalar_prefetch=1,
        grid=(n_blocks,),
        in_specs=[pl.BlockSpec((tmb, 512), lambda i, src: (src[i], 1)),
                  pl.BlockSpec((tmb, 512), lambda i, src: (src[i], 2))],
        out_specs=[pl.BlockSpec((tmb, BAND_HEADS, BAND_HD), lambda i, src: (i, 0, 0))] * 2)
    return pl.pallas_call(_head_rows_kernel, grid_spec=grid_spec, out_shape=[out, out],
                          compiler_params=_cparams(1), name="band_cache_rows")(src_blocks, po, po)


def _band_bias(rel_bias, tq, n_keys, key0):
    n_off = tq + n_keys - 1
    d_max = tq - 1 - key0
    rel = np.clip(d_max - np.arange(n_off), -REL_CLIP, REL_CLIP) + REL_CLIP
    vals = jnp.concatenate([rel_bias[:, rel], jnp.zeros((rel_bias.shape[0], 1), F32)], axis=1)
    rot = jnp.tile(vals, (1, tq))[:, :tq * n_off].reshape(-1, tq, n_off)
    toep = rot[:, :, tq - 1:tq - 1 + n_keys]
    qp = np.arange(tq)[:, None]
    kp = key0 + np.arange(n_keys)[None, :]
    cs = (qp // CHUNK) * CHUNK
    band = np.logical_and(kp >= cs - BAND_PAST, kp < cs + CHUNK)
    return jnp.where(jnp.asarray(band)[None], toep * math.log2(math.e), NEG_BIG).astype(F32)


def _ssd_kernel(dsk_ref, *refs, n_par):
    seq_refs = [refs[3 * s:3 * s + 3] for s in range(n_par)]
    (cw_ref, cb_ref, dtb_ref, alog_ref, ng_ref, h0_ref, c0_ref,
     o_ref, h_ref, cl_ref, h_sc, xe_sc, y_sc) = refs[3 * n_par:]
    c = pl.program_id(1)
    lc = seq_refs[0][0].shape[0]
    cdim = xe_sc.shape[2]

    @pl.when(c == 0)
    def _():
        h_sc[...] = h0_ref[...]
        for s in range(n_par):
            xe_sc[s, 0:8, :] = jnp.zeros((8, cdim), F32)
            xe_sc[s, 8 - (SSD_CONV - 1):8, :] = c0_ref[s]

    r_i = lax.broadcasted_iota(I32, (lc, lc), 0)
    c_i = lax.broadcasted_iota(I32, (lc, lc), 1)
    causal = c_i <= r_i
    tri = causal.astype(F32)
    gs = SSD_GROUPS * SSD_STATE
    hpg = SSD_HEADS // SSD_GROUPS
    gw = SSD_INNER // SSD_GROUPS
    for s, (z_ref, xbc_ref, dt_ref) in enumerate(seq_refs):
        xe_sc[s, 8:8 + lc, :] = xbc_ref[...]
        conv = cb_ref[...] + cw_ref[SSD_CONV - 1:SSD_CONV, :] * xe_sc[s, 8:8 + lc, :]
        for sh in range(1, SSD_CONV):
            conv = conv + cw_ref[SSD_CONV - 1 - sh:SSD_CONV - sh, :] * xe_sc[s, 8 - sh:8 - sh + lc, :]
        u = _silu(conv)
        xs = u[:, :SSD_INNER]
        bm = u[:, SSD_INNER:SSD_INNER + gs].astype(BF16)
        cm = u[:, SSD_INNER + gs:].astype(BF16)
        dx = dt_ref[...] + dtb_ref[...]
        dtv = jnp.maximum(dx, 0.0) + jnp.log1p(jnp.exp(-jnp.abs(dx)))
        a = dtv * (-jnp.exp(alog_ref[...]))
        acum = _dot_f32(tri, a)
        acum_t = acum.T
        for g in range(SSD_GROUPS):
            ss = slice(g * SSD_STATE, (g + 1) * SSD_STATE)
            cb = _dot_nt(cm[:, ss], bm[:, ss])
            for jj in range(hpg):
                j = g * hpg + jj
                ps = slice(j * SSD_HD, (j + 1) * SSD_HD)
                col = acum[:, j:j + 1]
                row = acum_t[j:j + 1, :]
                lmat = jnp.exp(jnp.where(causal, col - row, -jnp.inf))
                x_j = xs[:, ps]
                xdt = x_j * dtv[:, j:j + 1]
                h_old = h_sc[s, j]
                y = _dot((cb * lmat).astype(BF16), xdt.astype(BF16))
                y = y + _dot_nt(cm[:, ss], h_old.astype(BF16)) * jnp.exp(col)
                y = y + dsk_ref[j] * x_j
                last = acum[lc - 1:lc, j:j + 1]
                st = _dot_tn((xdt * jnp.exp(last - col)).astype(BF16), bm[:, ss])
                h_sc[s, j] = h_old * jnp.exp(last) + st
                y_sc[s, :, ps] = y
        yd = y_sc[s] * _silu(z_ref[...])
        for g in range(SSD_GROUPS):
            ws = slice(g * gw, (g + 1) * gw)
            yg = yd[:, ws]
            ms = jnp.mean(yg * yg, axis=-1, keepdims=True)
            o_ref[0, s, :, ws] = yg * lax.rsqrt(ms + LN_EPS) * ng_ref[:, ws]
        xe_sc[s, 0:8, :] = xe_sc[s, lc:lc + 8, :]

    @pl.when(c == pl.num_programs(1) - 1)
    def _():
        h_ref[...] = h_sc[...]
        for s in range(n_par):
            cl_ref[s] = xe_sc[s, 8 - (SSD_CONV - 1):8, :]


def _ssd(po, h0, c0, conv_w, conv_b, dt_bias, a_log, d_skip, norm_g, n_seq, t, row0):
    n_par = SEQ_PAR
    lc = min(SSD_LC, t)
    nc = t // lc
    blk0 = row0 // lc
    cdim = conv_w.shape[1]

    def rows(s, wd, j):
        return pl.BlockSpec((lc, wd), lambda b, c: (blk0 + (b * n_par + s) * nc + c, j))

    def const(shape):
        nd = len(shape)
        return pl.BlockSpec(shape, lambda b, c: (0,) * nd)

    pad8 = lambda v: jnp.zeros((1, LANE), F32).at[0, :SSD_HEADS].set(v)
    seq_specs, seq_args = [], []
    for s in range(n_par):
        seq_specs += [rows(s, 512, 3), rows(s, cdim, 2), rows(s, LANE, 24)]
        seq_args += [po, po, po]
    h_spec = pl.BlockSpec((n_par, SSD_HEADS, SSD_HD, SSD_STATE), lambda b, c: (b, 0, 0, 0))
    c_spec = pl.BlockSpec((n_par, SSD_CONV - 1, cdim), lambda b, c: (b, 0, 0))
    y, h_last, c_last = pl.pallas_call(
        functools.partial(_ssd_kernel, n_par=n_par),
        grid=(n_seq // n_par, nc),
        in_specs=[pl.BlockSpec(memory_space=pltpu.SMEM)] + seq_specs + [
            const((SSD_CONV, cdim)), const((1, cdim)), const((1, LANE)), const((1, LANE)),
            const((1, SSD_INNER)), h_spec, c_spec],
        out_specs=[pl.BlockSpec((1, n_par, lc, 512), lambda b, c: (b, 0, c, 0)), h_spec, c_spec],
        out_shape=[jax.ShapeDtypeStruct((n_seq // n_par, n_par, t, 512), F32),
                   jax.ShapeDtypeStruct((n_seq, SSD_HEADS, SSD_HD, SSD_STATE), F32),
                   jax.ShapeDtypeStruct((n_seq, SSD_CONV - 1, cdim), F32)],
        scratch_shapes=[pltpu.VMEM((n_par, SSD_HEADS, SSD_HD, SSD_STATE), F32),
                        pltpu.VMEM((n_par, lc + 8, cdim), F32),
                        pltpu.VMEM((n_par, lc, 512), F32)],
        compiler_params=_cparams(2),
        name="ssd_scan",
    )(d_skip, *seq_args, conv_w, conv_b.reshape(1, -1), pad8(dt_bias), pad8(a_log),
      norm_g.reshape(1, -1), h0, c0)
    return y.reshape(n_seq * t, 512), h_last, c_last


def _pad_cols(w, width):
    return jnp.concatenate([w, jnp.zeros((w.shape[0], width - w.shape[1]), w.dtype)], axis=1)


def kernel(x_prompt, x_sample, state_ret, cache_dsa_k, cache_dsa_v, cache_dsa_kidx, cache_band_k, cache_band_v, state_ssm, state_conv, e_w_in, e_w_out, e_gn_g, e_gn_b, o_w_in, o_w_out, o_rel_bias, o_conv_w, o_conv_b, o_dt_bias, o_a_log, o_d_skip, o_norm_g, ln1_g, ln1_b, ln2_g, ln2_b, router_w, router_b, exp_w_gu, exp_b_gu, exp_w_dn, exp_b_dn):
    bp, tp, _ = x_prompt.shape
    bs, ts, _ = x_sample.shape
    past = cache_dsa_k.shape[2]
    n_p, n_s = bp * tp, bs * ts
    n = n_p + n_s
    tm = math.gcd(512, math.gcd(n_p, n_s))
    assert tp % tm == 0 and tm % ts == 0 and ts == CHUNK

    x_parts = (x_prompt.reshape(n_p, D_MODEL), x_sample.reshape(n_s, D_MODEL))

    pe = _proj(x_parts, _pad_cols(e_w_in[0], EVEN_W).astype(BF16), tm)
    pos_p = jnp.arange(tp, dtype=I32)
    pos_s = past + jnp.arange(ts, dtype=I32)
    pos_tab = jnp.concatenate([pos_p, jnp.tile(pos_s, tm // ts)])
    tabs = (_rope_tables(pos_tab, RET_HEADS, RET_DK, RET_DK, RET_THETA),
            _rope_tables(pos_tab, RET_HEADS, RET_DK, RET_DK, RET_THETA, scale=RET_DK ** -0.5),
            _rope_tables(pos_tab, DSA_HEADS, DSA_HD, DSA_ROT, ROPE_THETA),
            _rope_tables(pos_tab, 1, IDX_DIM, DSA_ROT, ROPE_THETA, pad_to=LANE))
    (qa, ka, qb, kb, iq, ikw, q_st, iq_st, k_hm, ik_bf, v_t, iw_t,
     k5_p, v5_p, kidx_p, k5_s, v5_s, kidx_s) = _even_prep(pe, tabs, tm, n_p // tm, tp // tm)

    ya_p, ret_p = _retention(qa, ka, pe, jnp.zeros((bp, RET_HEADS, RET_DK, RET_DV), F32),
                             e_gn_g[0], e_gn_b[0], bp, tp, 0)
    ya_s, ret_s = _retention(qa, ka, pe, state_ret[0], e_gn_g[0], e_gn_b[0], bs, ts, n_p)

    topk_p = min(DSA_TOPK_MAX, tp // 4)
    qlim_p = (((pos_p // CHUNK) + 1) * CHUNK).reshape(1, tp)
    nq_p = tp // DSA_TQ
    nkb_p = ((jnp.arange(nq_p, dtype=I32) + 1) * DSA_TQ + DSA_TK - 1) // DSA_TK
    yb_p = _dsa(q_st, iq_st, iw_t, qlim_p, nkb_p, k_hm, v_t, ik_bf, bp, nq_p, tp, DSA_TQ, topk_p)

    s_len = past + ts
    s_pad = -(-s_len // (2 * DSA_TK)) * (2 * DSA_TK)
    topk_s = min(DSA_TOPK_MAX, s_len // 4)
    group = DSA_HEADS // DSA_KV_HEADS

    def cat_keys(cache, new, wd):
        zpad = jnp.zeros((bs, s_pad - s_len, wd), F32)
        return jnp.concatenate([cache, new.reshape(bs, ts, wd), zpad], axis=1)

    def pad_q(x):
        return jnp.concatenate([x, jnp.zeros((bs, DSA_TQ - ts) + x.shape[2:], x.dtype)], axis=1)

    ks = cat_keys(cache_dsa_k[0].reshape(bs, past, LANE), kb[n_p:], LANE)
    vs = cat_keys(cache_dsa_v[0].reshape(bs, past, LANE), pe[n_p:, 2176:2304], LANE)
    iks = cat_keys(cache_dsa_kidx[0], ikw[n_p:, :IDX_DIM], IDX_DIM)
    k_hm_s = ks.reshape(bs, s_pad, DSA_KV_HEADS, DSA_HD).transpose(2, 0, 1, 3).reshape(
        DSA_KV_HEADS, bs * s_pad, DSA_HD).astype(BF16)
    v_t_s = vs.reshape(bs * (s_pad // DSA_TK), DSA_TK, LANE).transpose(0, 2, 1)
    ones_s = jnp.ones((v_t_s.shape[0], DSA_VT_ROWS - DSA_HD, DSA_TK), F32)
    v_t_s = jnp.concatenate([v_t_s[:, :DSA_HD], ones_s, v_t_s[:, DSA_HD:], ones_s], axis=1).astype(BF16)
    ik_s = iks.reshape(bs * s_pad, IDX_DIM).astype(BF16)
    q_s = pad_q(qb[n_p:].reshape(bs, ts, DSA_KV_HEADS, group, DSA_HD))
    q_st_s = q_s.transpose(2, 0, 3, 1, 4).reshape(DSA_KV_HEADS, bs * group * DSA_TQ, DSA_HD)
    iq_s = pad_q(iq[n_p:].reshape(bs, ts, IDX_HEADS, IDX_DIM))
    iq_st_s = iq_s.transpose(0, 2, 1, 3).reshape(bs * IDX_HEADS * DSA_TQ, IDX_DIM)
    iw_t_s = pad_q(ikw[n_p:, IDX_DIM:IDX_DIM + 8].reshape(bs, ts, 8)).reshape(bs * DSA_TQ, 8).T
    qlim_s = jnp.full((1, DSA_TQ), s_len, I32)
    nkb_s = jnp.full((1,), -(-s_len // DSA_TK), I32)
    yb_s = _dsa(q_st_s, iq_st_s, iw_t_s, qlim_s, nkb_s, k_hm_s, v_t_s, ik_s, bs, 1, s_pad, ts, topk_s)

    h, h_packed = _outproj_ln((ya_p, ya_s), (yb_p, yb_s), e_w_out[0].astype(BF16), x_parts, ln1_g[0], ln1_b[0],
                              tm, n_p // tm)
    h = _moe_layer(h, h_packed, 0, router_w[0], router_b[0], exp_w_gu, exp_b_gu, exp_w_dn, exp_b_dn,
                   ln2_g[0], ln2_b[0], tm)

    po = _proj((h,), _pad_cols(o_w_in[0], ODD_W).astype(BF16), tm)
    tq_p = min(BAND_TQ, tp)
    nkb_band = BAND_PAST // tq_p + 1
    bias_p = _band_bias(o_rel_bias[0], tq_p, nkb_band * tq_p, -(nkb_band - 1) * tq_p)
    yc_p = _band(po, po, po, (0, 1, 2), bias_p, bp, tp, tq_p, tq_p, nkb_band, 0, tp // tq_p)
    band_len = cache_band_k.shape[2]
    kc_new = po[n_p:, 512:1024].reshape(bs, ts, 512)
    vc_new = po[n_p:, 1024:1536].reshape(bs, ts, 512)
    kcat = jnp.concatenate([cache_band_k[0].reshape(bs, band_len, 512), kc_new], axis=1)
    vcat = jnp.concatenate([cache_band_v[0].reshape(bs, band_len, 512), vc_new], axis=1)
    wlen = band_len + ts
    bias_s = _band_bias(o_rel_bias[0], ts, wlen, -band_len)
    yc_s = _band(po, kcat.reshape(bs * wlen, 512), vcat.reshape(bs * wlen, 512), (0, 0, 0), bias_s,
                 bs, ts, ts, wlen, 1, n_p, 1)

    ssd_w = (o_conv_w[0], o_conv_b[0], o_dt_bias[0], o_a_log[0], o_d_skip[0], o_norm_g[0])
    cdim = o_conv_w.shape[2]
    yd_p, ssm_p, conv_p = _ssd(po, jnp.zeros((bp, SSD_HEADS, SSD_HD, SSD_STATE), F32),
                               jnp.zeros((bp, SSD_CONV - 1, cdim), F32), *ssd_w, bp, tp, 0)
    yd_s, ssm_s, conv_s = _ssd(po, state_ssm[0], state_conv[0], *ssd_w, bs, ts, n_p)

    h, h_packed = _outproj_ln((yc_p, yc_s), (yd_p, yd_s), o_w_out[0].astype(BF16), (h,), ln1_g[1], ln1_b[1],
                              tm, n_p // tm)
    h_p, h_s = _moe_layer(h, h_packed, 1, router_w[1], router_b[1], exp_w_gu, exp_b_gu, exp_w_dn, exp_b_dn,
                          ln2_g[1], ln2_b[1], tm, split_rows=n_p)

    keep = min(BAND_PAST, tp)
    tmb = math.gcd(keep, 256)
    kept = (jnp.arange(bp, dtype=I32)[:, None] * (tp // tmb) + (tp - keep) // tmb
            + jnp.arange(keep // tmb, dtype=I32)[None, :]).reshape(-1)
    kc_p, vc_p = _head_rows(po, kept, tmb)
    kc_s, vc_s = _head_rows(po, n_p // ts + jnp.arange(bs, dtype=I32), ts)
    return (h_p.reshape(bp, tp, D_MODEL), h_s.reshape(bs, ts, D_MODEL),
            ret_p[None],
            k5_p.reshape(1, bp, tp, DSA_KV_HEADS, DSA_HD),
            v5_p.reshape(1, bp, tp, DSA_KV_HEADS, DSA_HD),
            kidx_p.reshape(1, bp, tp, IDX_DIM),
            kc_p.reshape(1, bp, keep, BAND_HEADS, BAND_HD), vc_p.reshape(1, bp, keep, BAND_HEADS, BAND_HD),
            ssm_p[None], conv_p[None],
            ret_s[None],
            k5_s.reshape(1, bs, ts, DSA_KV_HEADS, DSA_HD),
            v5_s.reshape(1, bs, ts, DSA_KV_HEADS, DSA_HD),
            kidx_s.reshape(1, bs, ts, IDX_DIM),
            kc_s.reshape(1, bs, ts, BAND_HEADS, BAND_HD), vc_s.reshape(1, bs, ts, BAND_HEADS, BAND_HD),
            ssm_s[None], conv_s[None])
```

```python
import functools
import math

import jax
import jax.numpy as jnp
import numpy as np
from jax import lax
from jax.experimental import pallas as pl
from jax.experimental.pallas import tpu as pltpu

F32 = jnp.float32
BF16 = jnp.bfloat16
I32 = jnp.int32
U32 = jnp.uint32

D_MODEL = 1024
CHUNK = 64
RET_HEADS, RET_DK, RET_DV, RET_THETA = 8, 32, 64, 10000.0
DSA_HEADS, DSA_KV_HEADS, DSA_HD = 8, 2, 64
DSA_ROT = DSA_HD // 4
IDX_HEADS, IDX_DIM = 4, 64
DSA_TOPK_MAX = 256
ROPE_THETA = 500000.0
BAND_HEADS, BAND_HD, BAND_PREV = 8, 64, 8
BAND_PAST = BAND_PREV * CHUNK
REL_CLIP = 256
SSD_HEADS, SSD_HD, SSD_GROUPS, SSD_STATE, SSD_CONV = 8, 64, 2, 128, 4
SSD_INNER = SSD_HEADS * SSD_HD
N_EXPERTS, TOP_K, D_FF = 32, 4, 1024
SWIGLU_LIMIT, SWIGLU_ALPHA = 7.0, 1.702
DEPTH = 2
DN_ALPHA = (2 * DEPTH) ** 0.25
LN_EPS = 1e-5

LANE = 128
VMEM_LIMIT = 56 * 1024 * 1024
INT_MIN = -(2 ** 31)
NEG_BIG = -1e30

EVEN_IN = 2628
EVEN_W = 2688
ODD_IN = 3080
ODD_W = 3200

MOE_TM = 512
MOE_PARTS = 4
RET_LC = 256
SEQ_PAR = 1
SSD_LC = 256
DSA_TQ = 128
DSA_TK = 256
DSA_VT_ROWS = 80
BAND_TQ = 256


def _cparams(n_axes):
    return pltpu.CompilerParams(dimension_semantics=("arbitrary",) * n_axes,
                                vmem_limit_bytes=VMEM_LIMIT)


def _dot(a, b):
    return jnp.dot(a, b, preferred_element_type=F32)


def _dot_nt(a, b):
    return lax.dot_general(a, b, (((1,), (1,)), ((), ())), preferred_element_type=F32)


def _dot_tn(a, b):
    return lax.dot_general(a, b, (((0,), (0,)), ((), ())), preferred_element_type=F32)


def _dot_f32(a, b):
    return jnp.dot(a, b, preferred_element_type=F32, precision=lax.Precision.HIGHEST)


def _layer_norm(x, g, b):
    mu = jnp.mean(x, axis=-1, keepdims=True)
    xc = x - mu
    var = jnp.mean(xc * xc, axis=-1, keepdims=True)
    return xc * lax.rsqrt(var + LN_EPS) * g + b


def _silu(x):
    return x * jax.nn.sigmoid(x)


def _pack_pairs(x):
    c = x.shape[1] // 2
    hi = pltpu.bitcast(x[:, :c].astype(jnp.bfloat16).astype(F32), U32)
    lo = pltpu.bitcast(x[:, c:].astype(jnp.bfloat16).astype(F32), U32)
    return hi | (lo >> 16)


def _unpack_pairs(w):
    return (pltpu.bitcast(w & jnp.uint32(0xFFFF0000), F32), pltpu.bitcast(w << 16, F32))


def _row_specs(parts, tm):
    width = parts[0].shape[1]
    if len(parts) == 1:
        return [pl.BlockSpec((tm, width), lambda i: (i, 0))], parts[0].shape[0] // tm
    n_first = parts[0].shape[0] // tm
    return [pl.BlockSpec((tm, width), lambda i: (jnp.minimum(i, n_first - 1), 0)),
            pl.BlockSpec((tm, width), lambda i: (jnp.maximum(i - n_first, 0), 0))], n_first


def _read_rows(refs, n_first):
    if len(refs) == 1:
        return refs[0][...]
    return jnp.where(pl.program_id(0) < n_first, refs[0][...], refs[1][...])


def _proj_kernel(*refs, n_parts, n_first):
    w_ref, o_ref = refs[n_parts:]
    o_ref[...] = _dot(_read_rows(refs[:n_parts], n_first).astype(BF16), w_ref[...])


def _proj(x_parts, w, tm):
    n = sum(p.shape[0] for p in x_parts)
    k, wd = w.shape
    x_specs, n_first = _row_specs(x_parts, tm)
    return pl.pallas_call(
        functools.partial(_proj_kernel, n_parts=len(x_parts), n_first=n_first),
        grid=(n // tm,),
        in_specs=x_specs + [pl.BlockSpec((k, wd), lambda i: (0, 0))],
        out_specs=pl.BlockSpec((tm, wd), lambda i: (i, 0)),
        out_shape=jax.ShapeDtypeStruct((n, wd), F32),
        compiler_params=_cparams(1),
        name="in_proj",
    )(*x_parts, w)


def _rope_tables(pos, n_heads, d, rot, theta, scale=1.0, pad_to=None):
    half = rot // 2
    inv = theta ** (-jnp.arange(half, dtype=F32) / half)
    ang = pos.astype(F32)[:, None] * inv[None, :]
    cos, sin = jnp.cos(ang), jnp.sin(ang)
    p = pos.shape[0]
    one = jnp.ones((p, d - rot), F32)
    zr = jnp.zeros((p, d - rot), F32)
    zh = jnp.zeros((p, half), F32)
    c = jnp.tile(jnp.concatenate([cos, cos, one], 1), (1, n_heads))
    a = jnp.tile(jnp.concatenate([-sin, zh, zr], 1), (1, n_heads))
    b = jnp.tile(jnp.concatenate([zh, sin, zr], 1), (1, n_heads))
    if pad_to is not None and pad_to > n_heads * d:
        extra = pad_to - n_heads * d
        c = jnp.concatenate([c, jnp.ones((p, extra), F32)], 1)
        a = jnp.concatenate([a, jnp.zeros((p, extra), F32)], 1)
        b = jnp.concatenate([b, jnp.zeros((p, extra), F32)], 1)
    return jnp.stack([c, a, b]) * scale


def _rope(x, tab_ref, half):
    w = x.shape[-1]
    return (x * tab_ref[0] + pltpu.roll(x, w - half, 1) * tab_ref[1]
            + pltpu.roll(x, half, 1) * tab_ref[2])


def _even_prep_kernel(qa_ref, ka_ref, qb_ref, kb_ref, iq_ref, ikw_ref, v_ref,
                      tq_ref, tk_ref, td_ref, ti_ref,
                      qa_o, ka_o, qb_o, kb_o, iq_o, ikw_o, qst_o, iqst_o, khm_o, ikb_o, vt_o, iwt_o,
                      k5p_o, v5p_o, xp_o, k5s_o, v5s_o, xs_o, *, n_prompt_blocks):
    tm = qa_ref.shape[0]
    h = DSA_ROT // 2
    qa_o[...] = _rope(qa_ref[...], tq_ref, RET_DK // 2)
    ka_o[...] = _rope(ka_ref[...], tk_ref, RET_DK // 2)
    qb = (_rope(qb_ref[...], td_ref, h) * (DSA_HD ** -0.5 * math.log2(math.e))).astype(BF16)
    qb_o[...] = qb
    kb = kb_ref[...]
    kb = (kb * td_ref[0, :, :LANE] + pltpu.roll(kb, LANE - h, 1) * td_ref[1, :, :LANE]
          + pltpu.roll(kb, h, 1) * td_ref[2, :, :LANE])
    kb_o[...] = kb
    iq = iq_ref[...]
    w = iq.shape[-1]
    iq = (iq * td_ref[0, :, :w] + pltpu.roll(iq, w - h, 1) * td_ref[1, :, :w]
          + pltpu.roll(iq, h, 1) * td_ref[2, :, :w]).astype(BF16)
    iq_o[...] = iq
    ikw = _rope(ikw_ref[...], ti_ref, h)
    ikw_o[...] = ikw
    group = DSA_HEADS // DSA_KV_HEADS
    for jb in range(tm // DSA_TQ):
        rs = slice(jb * DSA_TQ, (jb + 1) * DSA_TQ)
        for hd in range(DSA_HEADS):
            n, g = divmod(hd, group)
            ro = (jb * group + g) * DSA_TQ
            qst_o[n, ro:ro + DSA_TQ, :] = qb[rs, hd * DSA_HD:(hd + 1) * DSA_HD]
        for hd in range(IDX_HEADS):
            ro = (jb * IDX_HEADS + hd) * DSA_TQ
            iqst_o[ro:ro + DSA_TQ, :] = iq[rs, hd * IDX_DIM:(hd + 1) * IDX_DIM]
    kbb = kb.astype(BF16)
    for n in range(DSA_KV_HEADS):
        khm_o[n] = kbb[:, n * DSA_HD:(n + 1) * DSA_HD]
    ikb_o[...] = ikw[:, :IDX_DIM].astype(BF16)
    v = v_ref[...]
    ones = jnp.ones((DSA_VT_ROWS - DSA_HD, DSA_TK), F32)
    for j in range(tm // DSA_TK):
        vt = v[j * DSA_TK:(j + 1) * DSA_TK, :].T
        vt_o[j] = jnp.concatenate([vt[:DSA_HD], ones, vt[DSA_HD:], ones], axis=0).astype(BF16)
    iwt_o[...] = ikw.T[IDX_DIM:IDX_DIM + 8, :]

    def cache_rows(k_o, v_o, x_o):
        for n in range(DSA_KV_HEADS):
            k_o[:, n, :] = kb[:, n * DSA_HD:(n + 1) * DSA_HD]
            v_o[:, n, :] = v[:, n * DSA_HD:(n + 1) * DSA_HD]
        x_o[...] = ikw[:, :IDX_DIM]

    @pl.when(pl.program_id(0) < n_prompt_blocks)
    def _():
        cache_rows(k5p_o, v5p_o, xp_o)

    @pl.when(pl.program_id(0) >= n_prompt_blocks)
    def _():
        cache_rows(k5s_o, v5s_o, xs_o)


def _even_prep(pe, tabs, tm, n_prompt_blocks, tab_blocks):
    n = pe.shape[0]
    tq, tk, td, ti = tabs

    def tix(i):
        return (0, jnp.where(i < n_prompt_blocks, i % tab_blocks, tab_blocks), 0)

    def col(wd, j):
        return pl.BlockSpec((tm, wd), lambda i: (i, j))

    def tab(wd):
        return pl.BlockSpec((3, tm, wd), tix)

    def out(wd):
        return pl.BlockSpec((tm, wd), lambda i: (i, 0))

    group = DSA_HEADS // DSA_KV_HEADS
    n_prompt = n_prompt_blocks * tm
    cache_p = pl.BlockSpec((tm, DSA_KV_HEADS, DSA_HD),
                           lambda i: (jnp.minimum(i, n_prompt_blocks - 1), 0, 0))
    cache_s = pl.BlockSpec((tm, DSA_KV_HEADS, DSA_HD),
                           lambda i: (jnp.maximum(i - n_prompt_blocks, 0), 0, 0))
    kidx_p = pl.BlockSpec((tm, IDX_DIM), lambda i: (jnp.minimum(i, n_prompt_blocks - 1), 0))
    kidx_s = pl.BlockSpec((tm, IDX_DIM), lambda i: (jnp.maximum(i - n_prompt_blocks, 0), 0))
    return pl.pallas_call(
        functools.partial(_even_prep_kernel, n_prompt_blocks=n_prompt_blocks),
        grid=(n // tm,),
        in_specs=[col(256, 0), col(256, 1), col(512, 3), col(128, 16), col(256, 9), col(128, 20),
                  col(128, 17), tab(256), tab(256), tab(512), tab(128)],
        out_specs=[out(256), out(256), out(512), out(128), out(256), out(128),
                   pl.BlockSpec((DSA_KV_HEADS, group * tm, DSA_HD), lambda i: (0, i, 0)),
                   pl.BlockSpec((IDX_HEADS * tm, IDX_DIM), lambda i: (i, 0)),
                   pl.BlockSpec((DSA_KV_HEADS, tm, DSA_HD), lambda i: (0, i, 0)),
                   pl.BlockSpec((tm, IDX_DIM), lambda i: (i, 0)),
                   pl.BlockSpec((tm // DSA_TK, 2 * DSA_VT_ROWS, DSA_TK), lambda i: (i, 0, 0)),
                   pl.BlockSpec((8, tm), lambda i: (0, i)),
                   cache_p, cache_p, kidx_p, cache_s, cache_s, kidx_s],
        out_shape=[jax.ShapeDtypeStruct((n, 256), F32), jax.ShapeDtypeStruct((n, 256), F32),
                   jax.ShapeDtypeStruct((n, 512), BF16), jax.ShapeDtypeStruct((n, 128), F32),
                   jax.ShapeDtypeStruct((n, 256), BF16), jax.ShapeDtypeStruct((n, 128), F32),
                   jax.ShapeDtypeStruct((DSA_KV_HEADS, group * n, DSA_HD), BF16),
                   jax.ShapeDtypeStruct((IDX_HEADS * n, IDX_DIM), BF16),
                   jax.ShapeDtypeStruct((DSA_KV_HEADS, n, DSA_HD), BF16),
                   jax.ShapeDtypeStruct((n, IDX_DIM), BF16),
                   jax.ShapeDtypeStruct((n // DSA_TK, 2 * DSA_VT_ROWS, DSA_TK), BF16),
                   jax.ShapeDtypeStruct((8, n), F32),
                   jax.ShapeDtypeStruct((n_prompt, DSA_KV_HEADS, DSA_HD), F32),
                   jax.ShapeDtypeStruct((n_prompt, DSA_KV_HEADS, DSA_HD), F32),
                   jax.ShapeDtypeStruct((n_prompt, IDX_DIM), F32),
                   jax.ShapeDtypeStruct((n - n_prompt, DSA_KV_HEADS, DSA_HD), F32),
                   jax.ShapeDtypeStruct((n - n_prompt, DSA_KV_HEADS, DSA_HD), F32),
                   jax.ShapeDtypeStruct((n - n_prompt, IDX_DIM), F32)],
        compiler_params=_cparams(1),
        name="even_rope",
    )(pe, pe, pe, pe, pe, pe, pe, tq, tk, td, ti)


def _ret_kernel(gch_ref, *refs, n_par):
    seq_refs = [refs[4 * s:4 * s + 4] for s in range(n_par)]
    dm_ref, qd_ref, kd_ref, gng_ref, gnb_ref, avg_ref, s0_ref, o_ref, s_ref, s_sc, y_sc = refs[4 * n_par:]
    c = pl.program_id(1)

    @pl.when(c == 0)
    def _():
        s_sc[...] = s0_ref[...]

    for s, (q_ref, k_ref, v_ref, g_ref) in enumerate(seq_refs):
        q = q_ref[...]
        k = k_ref[...]
        qx = (q * qd_ref[...]).astype(BF16)
        kw = (k * kd_ref[...]).astype(BF16)
        qb = q.astype(BF16)
        kb = k.astype(BF16)
        vb = v_ref[...].astype(BF16)
        gate = g_ref[...]
        for h in range(RET_HEADS):
            ks = slice(h * RET_DK, (h + 1) * RET_DK)
            vs = slice(h * RET_DV, (h + 1) * RET_DV)
            att = _dot_nt(qb[:, ks], kb[:, ks]) * dm_ref[h]
            s_old = s_sc[s, h]
            y_sc[:, vs] = _dot(att.astype(BF16), vb[:, vs]) + _dot(qx[:, ks], s_old.astype(BF16))
            s_sc[s, h] = s_old * gch_ref[h] + _dot_tn(kw[:, ks], vb[:, vs])
        y = y_sc[...]
        mu = _dot(y.astype(BF16), avg_ref[...])
        d = y - mu
        var = _dot((d * d).astype(BF16), avg_ref[...])
        yn = d * lax.rsqrt(var + LN_EPS) * gng_ref[...] + gnb_ref[...]
        o_ref[0, s] = _silu(gate) * yn

    @pl.when(c == pl.num_programs(1) - 1)
    def _():
        s_ref[...] = s_sc[...]


def _retention(qa, ka, pe, s0, gn_g, gn_b, n_seq, t, row0):
    n_par = SEQ_PAR
    lc = min(RET_LC, t)
    nc = t // lc
    blk0 = row0 // lc
    log_g = jnp.log(1.0 - 2.0 ** (-5.0 - jnp.arange(RET_HEADS, dtype=F32)))
    pos = jnp.arange(lc, dtype=F32)
    diff = pos[:, None] - pos[None, :]
    dmask = jnp.where(diff >= 0, jnp.exp(jnp.maximum(diff, 0.0)[None] * log_g[:, None, None]), 0.0)
    w_end = jnp.exp((lc - 1 - pos)[:, None] * log_g[None, :])
    xi = jnp.exp((pos + 1.0)[:, None] * log_g[None, :])
    kdec = jnp.repeat(w_end, RET_DK, axis=1)
    qdec = jnp.repeat(xi, RET_DK, axis=1)
    gch = jnp.exp(lc * log_g)
    head = np.arange(512) // RET_DV
    avg = jnp.asarray((head[:, None] == head[None, :]) / RET_DV, BF16)

    def rows(s, wd, j):
        return pl.BlockSpec((lc, wd), lambda b, c: (blk0 + (b * n_par + s) * nc + c, j))

    def const(shape):
        nd = len(shape)
        return pl.BlockSpec(shape, lambda b, c: (0,) * nd)

    seq_specs, seq_args = [], []
    for s in range(n_par):
        seq_specs += [rows(s, 256, 0), rows(s, 256, 0), rows(s, 512, 1), rows(s, 512, 2)]
        seq_args += [qa, ka, pe, pe]
    state = pl.BlockSpec((n_par, RET_HEADS, RET_DK, RET_DV), lambda b, c: (b, 0, 0, 0))
    y, s_last = pl.pallas_call(
        functools.partial(_ret_kernel, n_par=n_par),
        grid=(n_seq // n_par, nc),
        in_specs=[pl.BlockSpec(memory_space=pltpu.SMEM)] + seq_specs + [
            const((RET_HEADS, lc, lc)), const((lc, 256)), const((lc, 256)),
            const((1, 512)), const((1, 512)), const((512, 512)), state],
        out_specs=[pl.BlockSpec((1, n_par, lc, 512), lambda b, c: (b, 0, c, 0)), state],
        out_shape=[jax.ShapeDtypeStruct((n_seq // n_par, n_par, t, 512), F32),
                   jax.ShapeDtypeStruct((n_seq, RET_HEADS, RET_DK, RET_DV), F32)],
        scratch_shapes=[pltpu.VMEM((n_par, RET_HEADS, RET_DK, RET_DV), F32),
                        pltpu.VMEM((lc, 512), F32)],
        compiler_params=_cparams(2),
        name="retention",
    )(gch, *seq_args, dmask, qdec, kdec, gn_g.reshape(1, 512), gn_b.reshape(1, 512), avg, s0)
    return y.reshape(n_seq * t, 512), s_last


def _col_reduce(x, op):
    r, c = x.shape
    return op(op(x.reshape(r // 8, 8, c), axis=0), axis=0, keepdims=True)


def _dsa_kernel(nkb_ref, q_ref, iq_ref, iwt_ref, qlim_ref, k_ref, vt_ref, ik_ref, *rest,
                topk, tq_out):
    o_ref, key_sc, m_sc, l_sc, acc_sc, lga_sc, lgb_sc = rest
    nkb = nkb_ref[pl.program_id(1)]
    tq = qlim_ref.shape[1]
    tk = key_sc.shape[1]
    group = DSA_HEADS // DSA_KV_HEADS
    qlim = qlim_ref[...]
    iwt = iwt_ref[...] * ((IDX_HEADS * IDX_DIM) ** -0.5)
    iqs = iq_ref[...]
    krow = lax.broadcasted_iota(I32, (tk, tq), 0)

    def score_body(kb, carry):
        off = pl.multiple_of(kb * tk, tk)
        s_all = _dot_nt(ik_ref[pl.ds(off, tk), :], iqs)
        s = jnp.zeros((tk, tq), F32)
        for h in range(IDX_HEADS):
            s = s + iwt[h:h + 1, :] * jnp.maximum(s_all[:, h * tq:(h + 1) * tq], 0.0)
        s = jnp.where(s == 0.0, 0.0, s)
        bits = pltpu.bitcast(s, I32)
        key = jnp.where(bits >= 0, bits, bits ^ jnp.int32(0x7FFFFFFF))
        adm = (off + krow) < qlim
        key_sc[kb] = jnp.where(adm, key, jnp.int32(INT_MIN))
        return carry

    n_pair = (nkb + 1) // 2
    last = 2 * n_pair - 1

    def score_pair(j, carry):
        return score_body(2 * j + 1, score_body(2 * j, carry))

    lax.fori_loop(0, n_pair, score_pair, 0)

    def count(pred):
        def body(kb, acc):
            hit = jnp.where(pred(key_sc[kb]), 1.0, 0.0)
            return acc + jnp.sum(hit.reshape(tk // 64, 64, tq), axis=0)
        acc = lax.fori_loop(0, nkb, body, jnp.zeros((64, tq), F32))
        return jnp.sum(acc, axis=0, keepdims=True)

    def bit_body(it, ans):
        cand = ans + (jnp.int32(1) << (31 - it))
        return jnp.where(count(lambda k: k >= cand) >= topk, cand, ans)

    t = lax.fori_loop(0, 32, bit_body, jnp.full((1, tq), INT_MIN, I32))
    need = topk - count(lambda k: k > t)

    m_sc[...] = jnp.full(m_sc.shape, 0.1 * NEG_BIG, F32)
    l_sc[...] = jnp.zeros(l_sc.shape, F32)
    acc_sc[...] = jnp.zeros(acc_sc.shape, F32)
    r_i = lax.broadcasted_iota(I32, (tk, tk), 0)
    c_i = lax.broadcasted_iota(I32, (tk, tk), 1)
    lower = (c_i < r_i).astype(BF16)

    def logits_stage(kb, n_eq, dst):
        off = pl.multiple_of(kb * tk, tk)
        key = key_sc[kb]
        adm = (off + krow) < qlim
        eq = jnp.logical_and(key == t, adm)
        eqf = jnp.where(eq, 1.0, 0.0)
        pref = _dot(lower, eqf.astype(BF16))
        sel = jnp.logical_and(adm, jnp.logical_or(
            key > t, jnp.logical_and(eq, (n_eq + pref) < need)))
        bias = jnp.where(sel, 0.0, NEG_BIG)
        for n in range(DSA_KV_HEADS):
            lg_all = _dot_nt(k_ref[n, pl.ds(off, tk), :], q_ref[n])
            for g in range(group):
                ls = slice(g * tq, (g + 1) * tq)
                dst[n, :, ls] = lg_all[:, ls] + bias
        return n_eq + _col_reduce(eqf, jnp.sum)

    def softmax_stage(kb, src):
        vt = vt_ref[kb]
        for n in range(DSA_KV_HEADS):
            ps, alphas = [], []
            for g in range(group):
                ls = slice(g * tq, (g + 1) * tq)
                lg = src[n, :, ls]
                m_old = m_sc[n, :, ls]
                m_new = jnp.maximum(m_old, _col_reduce(lg, jnp.max))
                m_sc[n, :, ls] = m_new
                ps.append(jnp.exp2(lg - m_new).astype(BF16))
                alphas.append(jnp.exp2(m_old - m_new))
            p_all = jnp.concatenate(ps, axis=1)
            alpha_all = jnp.concatenate(alphas, axis=1)
            pv = _dot(vt[n * DSA_VT_ROWS:(n + 1) * DSA_VT_ROWS, :], p_all)
            acc_sc[n] = alpha_all * acc_sc[n] + pv[:DSA_HD]
            l_sc[n] = alpha_all * l_sc[n] + pv[DSA_HD:DSA_HD + 1]

    def pair_body(j, n_eq):
        kb0 = 2 * j
        softmax_stage(kb0, lga_sc)
        n_eq = logits_stage(kb0 + 1, n_eq, lgb_sc)
        softmax_stage(kb0 + 1, lgb_sc)
        return logits_stage(jnp.minimum(kb0 + 2, last), n_eq, lga_sc)

    lax.fori_loop(0, n_pair, pair_body, logits_stage(0, jnp.zeros((1, tq), F32), lga_sc))
    pieces = []
    for n in range(DSA_KV_HEADS):
        o_n = acc_sc[n] / l_sc[n]
        for g in range(group):
            pieces.append(o_n[:, g * tq:(g + 1) * tq])
    o_ref[...] = jnp.concatenate(pieces, axis=0).T[:tq_out, :]


def _dsa(q_st, iq_st, iw_t, qlim, nkb, k_hm, v_t, ik_bf, n_seq, nq, s_len, tq_out, topk):
    tq = DSA_TQ
    group = DSA_HEADS // DSA_KV_HEADS
    in_specs = [pl.BlockSpec((DSA_KV_HEADS, group * tq, DSA_HD), lambda b, i, s: (0, b * nq + i, 0)),
                pl.BlockSpec((IDX_HEADS * tq, IDX_DIM), lambda b, i, s: (b * nq + i, 0)),
                pl.BlockSpec((8, tq), lambda b, i, s: (0, b * nq + i)),
                pl.BlockSpec((1, tq), lambda b, i, s: (0, i)),
                pl.BlockSpec((DSA_KV_HEADS, s_len, DSA_HD), lambda b, i, s: (0, b, 0)),
                pl.BlockSpec((s_len // DSA_TK, 2 * DSA_VT_ROWS, DSA_TK), lambda b, i, s: (b, 0, 0)),
                pl.BlockSpec((s_len, IDX_DIM), lambda b, i, s: (b, 0))]
    grid_spec = pltpu.PrefetchScalarGridSpec(
        num_scalar_prefetch=1,
        grid=(n_seq, nq),
        in_specs=in_specs,
        out_specs=pl.BlockSpec((tq_out, 512), lambda b, i, s: (b * nq + i, 0)),
        scratch_shapes=[pltpu.VMEM((s_len // DSA_TK, DSA_TK, tq), I32),
                        pltpu.VMEM((DSA_KV_HEADS, 1, group * tq), F32),
                        pltpu.VMEM((DSA_KV_HEADS, 1, group * tq), F32),
                        pltpu.VMEM((DSA_KV_HEADS, DSA_HD, group * tq), F32),
                        pltpu.VMEM((DSA_KV_HEADS, DSA_TK, group * tq), F32),
                        pltpu.VMEM((DSA_KV_HEADS, DSA_TK, group * tq), F32)])
    return pl.pallas_call(
        functools.partial(_dsa_kernel, topk=topk, tq_out=tq_out),
        grid_spec=grid_spec,
        out_shape=jax.ShapeDtypeStruct((n_seq * nq * tq_out, 512), F32),
        compiler_params=_cparams(2),
        name="dsa_attention",
    )(nkb, q_st, iq_st, iw_t, qlim, k_hm, v_t, ik_bf)


def _outproj_ln_kernel(*refs, counts, n_first):
    na, nb, nh = counts
    ya = _read_rows(refs[:na], n_first)
    yb = _read_rows(refs[na:na + nb], n_first)
    w_ref = refs[na + nb]
    h = _read_rows(refs[na + nb + 1:na + nb + 1 + nh], n_first)
    g_ref, b_ref, o_ref, op_ref = refs[na + nb + 1 + nh:]
    half = ya.shape[1]
    y = _dot(ya.astype(BF16), w_ref[:half, :]) + _dot(yb.astype(BF16), w_ref[half:, :])
    out = _layer_norm(DN_ALPHA * h + y, g_ref[...], b_ref[...])
    o_ref[...] = out
    op_ref[...] = _pack_pairs(out)


def _outproj_ln(ya_parts, yb_parts, w, h_parts, g, b, tm, n_first):
    n = sum(p.shape[0] for p in h_parts)
    specs = []
    for parts in (ya_parts, yb_parts):
        sp, nf = _row_specs(parts, tm)
        assert len(parts) == 1 or nf == n_first
        specs += sp
    h_specs, nf = _row_specs(h_parts, tm)
    assert len(h_parts) == 1 or nf == n_first
    return pl.pallas_call(
        functools.partial(_outproj_ln_kernel, counts=(len(ya_parts), len(yb_parts), len(h_parts)),
                          n_first=n_first),
        grid=(n // tm,),
        in_specs=specs + [pl.BlockSpec((D_MODEL, D_MODEL), lambda i: (0, 0))] + h_specs + [
                  pl.BlockSpec((1, D_MODEL), lambda i: (0, 0)),
                  pl.BlockSpec((1, D_MODEL), lambda i: (0, 0))],
        out_specs=[pl.BlockSpec((tm, D_MODEL), lambda i: (i, 0)),
                   pl.BlockSpec((tm, D_MODEL // 2), lambda i: (i, 0))],
        out_shape=[jax.ShapeDtypeStruct((n, D_MODEL), F32),
                   jax.ShapeDtypeStruct((n, D_MODEL // 2), U32)],
        compiler_params=_cparams(1),
        name="out_proj_ln",
    )(*ya_parts, *yb_parts, w, *h_parts, g.reshape(1, -1), b.reshape(1, -1))


def _router_kernel(x_ref, w_ref, b_ref, idx_o, gate_o, rank_o, cnt_o, cnt_sc):
    i = pl.program_id(0)

    @pl.when(i == 0)
    def _():
        cnt_sc[...] = jnp.zeros(cnt_sc.shape, F32)

    tm = x_ref.shape[0]
    ne = w_ref.shape[0]
    x = x_ref[...]
    w = w_ref[...]
    x_hi = x.astype(BF16)
    x_lo = (x - x_hi.astype(F32)).astype(BF16)
    w_hi = w.astype(BF16)
    w_lo = (w - w_hi.astype(F32)).astype(BF16)
    logits = (_dot_nt(w_hi, x_hi) + _dot_nt(w_lo, x_hi) + _dot_nt(w_hi, x_lo)) + b_ref[...]
    erow = lax.broadcasted_iota(I32, (ne, tm), 0)
    vals, idxs = [], []
    onehot = jnp.zeros((ne, tm), F32)
    for _ in range(TOP_K):
        m = jnp.max(logits, axis=0, keepdims=True)
        ix = jnp.min(jnp.where(logits == m, erow, ne), axis=0, keepdims=True)
        hit = erow == ix
        onehot = jnp.where(hit, 1.0, onehot)
        logits = jnp.where(hit, -jnp.inf, logits)
        vals.append(m)
        idxs.append(ix)
    es = [jnp.exp(v - vals[0]) for v in vals]
    den = es[0] + es[1] + es[2] + es[3]
    r_i = lax.broadcasted_iota(I32, (tm, tm), 0)
    c_i = lax.broadcasted_iota(I32, (tm, tm), 1)
    upper = (r_i < c_i).astype(BF16)
    rank_dense = _dot(onehot.astype(BF16), upper) + cnt_sc[...]
    prow = lax.broadcasted_iota(I32, (8, tm), 0)
    idx_out = jnp.zeros((8, tm), I32)
    gate_out = jnp.zeros((8, tm), F32)
    rank_out = jnp.zeros((8, tm), F32)
    for k in range(TOP_K):
        rk = jnp.sum(jnp.where(erow == idxs[k], rank_dense, 0.0), axis=0, keepdims=True)
        idx_out = jnp.where(prow == k, idxs[k], idx_out)
        gate_out = jnp.where(prow == k, es[k] / den, gate_out)
        rank_out = jnp.where(prow == k, rk, rank_out)
    idx_o[...] = idx_out
    gate_o[...] = gate_out
    rank_o[...] = rank_out.astype(I32)
    cnt = cnt_sc[...] + jnp.sum(onehot, axis=1, keepdims=True)
    cnt_sc[...] = cnt
    cnt_o[...] = cnt


def _router(x, w_r, b_r, tm):
    n = x.shape[0]
    row = pl.BlockSpec((8, tm), lambda i: (0, i))
    return pl.pallas_call(
        _router_kernel,
        grid=(n // tm,),
        in_specs=[pl.BlockSpec((tm, D_MODEL), lambda i: (i, 0)),
                  pl.BlockSpec((N_EXPERTS, D_MODEL), lambda i: (0, 0)),
                  pl.BlockSpec((N_EXPERTS, 1), lambda i: (0, 0))],
        out_specs=[row, row, row, pl.BlockSpec((N_EXPERTS, 1), lambda i: (0, 0))],
        out_shape=[jax.ShapeDtypeStruct((8, n), I32), jax.ShapeDtypeStruct((8, n), F32),
                   jax.ShapeDtypeStruct((8, n), I32), jax.ShapeDtypeStruct((N_EXPERTS, 1), F32)],
        scratch_shapes=[pltpu.VMEM((N_EXPERTS, 1), F32)],
        compiler_params=_cparams(1),
        name="moe_router",
    )(x, w_r.T, b_r.reshape(N_EXPERTS, 1))


def _moe_kernel(be_ref, nu_ref, *rest, n_parts, blocks_per_part):
    x_refs = rest[:n_parts]
    wgu_ref, bgu_ref, wdn_ref, bdn_ref, o_ref, wgu_sc, wdn_sc = rest[n_parts:]
    i = pl.program_id(0)

    @pl.when(jnp.logical_or(i == 0, be_ref[i] != be_ref[jnp.maximum(i - 1, 0)]))
    def _():
        wgu_sc[...] = wgu_ref[0, 0].astype(BF16)
        wdn_sc[...] = wdn_ref[0, 0].astype(BF16)

    @pl.when(i < nu_ref[0])
    def _():
        part = i // blocks_per_part
        xw = x_refs[0][...]
        for c in range(1, n_parts):
            xw = jnp.where(part == c, x_refs[c][...], xw)
        xa, xb = _unpack_pairs(xw)
        x = jnp.concatenate([xa.astype(BF16), xb.astype(BF16)], axis=1)
        h = _dot(x, wgu_sc[...]) + bgu_ref[0, 0]
        g = jnp.minimum(h[:, :D_FF], SWIGLU_LIMIT)
        up = jnp.clip(h[:, D_FF:], -SWIGLU_LIMIT, SWIGLU_LIMIT)
        a = (up + 1.0) * g * jax.nn.sigmoid(SWIGLU_ALPHA * g)
        o_ref[...] = _pack_pairs(_dot(a.astype(BF16), wdn_sc[...]) + bdn_ref[0, 0])

    @pl.when(i >= nu_ref[0])
    def _():
        o_ref[...] = jnp.zeros(o_ref.shape, U32)


def _moe_experts(xs_parts, blk_e, n_used, layer, w_gu, b_gu, w_dn, b_dn):
    n_parts = len(xs_parts)
    tm = MOE_TM
    bpp = xs_parts[0].shape[0] // tm
    n_rows = n_parts * bpp * tm
    depth = w_gu.shape[0]

    def x_spec(c):
        return pl.BlockSpec((tm, D_MODEL // 2),
                            lambda i, be, nu: (jnp.clip(i - c * bpp, 0, bpp - 1), 0))
    grid_spec = pltpu.PrefetchScalarGridSpec(
        num_scalar_prefetch=2,
        grid=(n_rows // tm,),
        in_specs=[x_spec(c) for c in range(n_parts)] + [
                  pl.BlockSpec((1, 1, D_MODEL, 2 * D_FF), lambda i, be, nu: (layer, be[i], 0, 0)),
                  pl.BlockSpec((1, 1, 1, 2 * D_FF), lambda i, be, nu: (layer, be[i], 0, 0)),
                  pl.BlockSpec((1, 1, D_FF, D_MODEL), lambda i, be, nu: (layer, be[i], 0, 0)),
                  pl.BlockSpec((1, 1, 1, D_MODEL), lambda i, be, nu: (layer, be[i], 0, 0))],
        out_specs=pl.BlockSpec((tm, D_MODEL // 2), lambda i, be, nu: (i, 0)),
        scratch_shapes=[pltpu.VMEM((D_MODEL, 2 * D_FF), BF16), pltpu.VMEM((D_FF, D_MODEL), BF16)])
    return pl.pallas_call(
        functools.partial(_moe_kernel, n_parts=n_parts, blocks_per_part=bpp),
        grid_spec=grid_spec,
        out_shape=jax.ShapeDtypeStruct((n_rows, D_MODEL // 2), U32),
        compiler_params=_cparams(1),
        name="moe_experts",
    )(blk_e, n_used, *xs_parts, w_gu, b_gu.reshape(depth, N_EXPERTS, 1, -1), w_dn,
      b_dn.reshape(depth, N_EXPERTS, 1, -1))


def _combine_ln_kernel(h_ref, y0_ref, y1_ref, y2_ref, y3_ref, gate_ref, g_ref, b_ref, *o_refs,
                       n_first):
    gate = gate_ref[...]
    ya, yb = None, None
    for k, y_ref in enumerate((y0_ref, y1_ref, y2_ref, y3_ref)):
        a, b = _unpack_pairs(y_ref[...])
        gk = gate[:, k:k + 1]
        ya = gk * a if ya is None else ya + gk * a
        yb = gk * b if yb is None else yb + gk * b
    y = jnp.concatenate([ya, yb], axis=1)
    out = _layer_norm(DN_ALPHA * h_ref[...] + y, g_ref[...], b_ref[...])
    if len(o_refs) == 1:
        o_refs[0][...] = out
    else:
        @pl.when(pl.program_id(0) < n_first)
        def _():
            o_refs[0][...] = out

        @pl.when(pl.program_id(0) >= n_first)
        def _():
            o_refs[1][...] = out


def _combine_ln(h, ys, gate, g, b, tm, split_rows=None):
    n = h.shape[0]
    row = pl.BlockSpec((tm, D_MODEL), lambda i: (i, 0))
    half = pl.BlockSpec((tm, D_MODEL // 2), lambda i: (i, 0))
    vec = pl.BlockSpec((1, D_MODEL), lambda i: (0, 0))
    if split_rows is None:
        n_first = n // tm
        out_specs = row
        out_shape = jax.ShapeDtypeStruct((n, D_MODEL), F32)
    else:
        n_first = split_rows // tm
        out_specs = [pl.BlockSpec((tm, D_MODEL), lambda i: (jnp.minimum(i, n_first - 1), 0)),
                     pl.BlockSpec((tm, D_MODEL), lambda i: (jnp.maximum(i - n_first, 0), 0))]
        out_shape = [jax.ShapeDtypeStruct((split_rows, D_MODEL), F32),
                     jax.ShapeDtypeStruct((n - split_rows, D_MODEL), F32)]
    return pl.pallas_call(
        functools.partial(_combine_ln_kernel, n_first=n_first),
        grid=(n // tm,),
        in_specs=[row, half, half, half, half, pl.BlockSpec((tm, TOP_K), lambda i: (i, 0)), vec, vec],
        out_specs=out_specs,
        out_shape=out_shape,
        compiler_params=_cparams(1),
        name="moe_combine_ln",
    )(h, ys[0], ys[1], ys[2], ys[3], gate, g.reshape(1, -1), b.reshape(1, -1))


def _rows(x, idx):
    return x.at[idx].get(mode="promise_in_bounds")


def _moe_layer(h, h_packed, layer, w_r, b_r, w_gu, b_gu, w_dn, b_dn, ln_g, ln_b, tm, split_rows=None):
    n = h.shape[0]
    n_pair = n * TOP_K
    idx_t, gate_t, rank_t, cnt = _router(h, w_r, b_r, tm)
    top_i = idx_t[:TOP_K].T
    counts = cnt[:, 0].astype(I32)
    padded = (counts + MOE_TM - 1) // MOE_TM * MOE_TM
    pad_end = jnp.cumsum(padded)
    start = pad_end - padded
    first = jnp.cumsum(counts) - counts
    dest = _rows(start, top_i) + rank_t[:TOP_K].T
    n_blk = -(-(-(-n_pair // MOE_TM) + N_EXPERTS) // MOE_PARTS) * MOE_PARTS
    n_used = (pad_end[-1] // MOE_TM).astype(I32)
    blk_row = jnp.minimum(jnp.arange(n_blk, dtype=I32), n_used - 1) * MOE_TM
    blk_e = jnp.sum((pad_end[None, :] <= blk_row[:, None]).astype(I32), axis=1)
    blk_e = jnp.minimum(blk_e, N_EXPERTS - 1)
    order = jnp.argsort(top_i.reshape(-1), stable=True).astype(I32)
    row_in_e = (jnp.arange(n_blk, dtype=I32)[:, None] * MOE_TM - _rows(start, blk_e)[:, None]
                + jnp.arange(MOE_TM, dtype=I32)[None, :])
    pair = jnp.clip(_rows(first, blk_e)[:, None] + row_in_e, 0, n_pair - 1).reshape(-1)
    src = (_rows(order, pair) // TOP_K).reshape(MOE_PARTS, -1)
    xs_parts = [_rows(h_packed, src[c]) for c in range(MOE_PARTS)]
    ybuf = _moe_experts(xs_parts, blk_e, n_used.reshape(1), layer, w_gu, b_gu, w_dn, b_dn)
    ys = [_rows(ybuf, dest[:, k]) for k in range(TOP_K)]
    return _combine_ln(h, ys, gate_t[:TOP_K].T, ln_g, ln_b, tm, split_rows)


def _band_kernel(q_ref, *rest, nkb):
    k_refs = rest[:nkb]
    v_refs = rest[nkb:2 * nkb]
    bias_ref = rest[2 * nkb]
    o_ref = rest[-1]
    c = pl.program_id(1)
    tq = q_ref.shape[0]
    tkb = k_refs[0].shape[0]
    q_scale = BAND_HD ** -0.5 * math.log2(math.e)
    kbs = [r[...].astype(BF16) for r in k_refs]
    vbs = [r[...].astype(BF16) for r in v_refs]
    off = [jnp.where((c + j - (nkb - 1)) >= 0, 0.0, NEG_BIG) for j in range(nkb)]
    low = lax.broadcasted_iota(I32, (tq, LANE), 1) < BAND_HD
    for pair in range(BAND_HEADS // 2):
        ls = slice(pair * LANE, (pair + 1) * LANE)
        qp = q_ref[:, ls] * q_scale
        outs = []
        for par in range(2):
            h = 2 * pair + par
            qh = jnp.where(low == (par == 0), qp, 0.0).astype(BF16)
            lgs = [_dot_nt(qh, kbs[j][:, ls]) + (bias_ref[h, :, j * tkb:(j + 1) * tkb] + off[j])
                   for j in range(nkb)]
            mx = lgs[0]
            for j in range(1, nkb):
                mx = jnp.maximum(mx, lgs[j])
            m = mx.max(axis=1, keepdims=True)
            ps = [jnp.exp2(lg - m) for lg in lgs]
            sm = ps[0]
            for j in range(1, nkb):
                sm = sm + ps[j]
            den = sm.sum(axis=1, keepdims=True)
            acc = _dot(ps[0].astype(BF16), vbs[0][:, ls])
            for j in range(1, nkb):
                acc = acc + _dot(ps[j].astype(BF16), vbs[j][:, ls])
            outs.append(acc * (1.0 / den))
        o_ref[:, ls] = jnp.where(low, outs[0], outs[1])


def _band(q_arr, k_arr, v_arr, cols, bias, n_seq, t, tq, tkb, nkb, q_row0, kv_blocks_per_seq):
    nq = t // tq
    qblk0 = q_row0 // tq
    qcol, kcol, vcol = cols

    def kv_spec(j, col):
        def ix(b, c):
            return (b * kv_blocks_per_seq + jnp.maximum(c + j - (nkb - 1), 0), col)
        return pl.BlockSpec((tkb, 512), ix)

    in_specs = ([pl.BlockSpec((tq, 512), lambda b, c: (qblk0 + b * nq + c, qcol))]
                + [kv_spec(j, kcol) for j in range(nkb)]
                + [kv_spec(j, vcol) for j in range(nkb)]
                + [pl.BlockSpec(bias.shape, lambda b, c: (0, 0, 0))])
    return pl.pallas_call(
        functools.partial(_band_kernel, nkb=nkb),
        grid=(n_seq, nq),
        in_specs=in_specs,
        out_specs=pl.BlockSpec((tq, 512), lambda b, c: (b * nq + c, 0)),
        out_shape=jax.ShapeDtypeStruct((n_seq * t, 512), F32),
        compiler_params=_cparams(2),
        name="band_attention",
    )(q_arr, *([k_arr] * nkb), *([v_arr] * nkb), bias)


def _head_rows_kernel(src_ref, k_ref, v_ref, ko_ref, vo_ref):
    for h in range(BAND_HEADS):
        hs = slice(h * BAND_HD, (h + 1) * BAND_HD)
        ko_ref[:, h, :] = k_ref[:, hs]
        vo_ref[:, h, :] = v_ref[:, hs]


def _head_rows(po, src_blocks, tmb):
    n_blocks = src_blocks.shape[0]
    out = jax.ShapeDtypeStruct((n_blocks * tmb, BAND_HEADS, BAND_HD), F32)
    grid_spec = pltpu.PrefetchScalarGridSpec(
        num_scalar_prefetch=1,
        grid=(n_blocks,),
        in_specs=[pl.BlockSpec((tmb, 512), lambda i, src: (src[i], 1)),
                  pl.BlockSpec((tmb, 512), lambda i, src: (src[i], 2))],
        out_specs=[pl.BlockSpec((tmb, BAND_HEADS, BAND_HD), lambda i, src: (i, 0, 0))] * 2)
    return pl.pallas_call(_head_rows_kernel, grid_spec=grid_spec, out_shape=[out, out],
                          compiler_params=_cparams(1), name="band_cache_rows")(src_blocks, po, po)


def _band_bias(rel_bias, tq, n_keys, key0):
    n_off = tq + n_keys - 1
    d_max = tq - 1 - key0
    rel = np.clip(d_max - np.arange(n_off), -REL_CLIP, REL_CLIP) + REL_CLIP
    vals = jnp.concatenate([rel_bias[:, rel], jnp.zeros((rel_bias.shape[0], 1), F32)], axis=1)
    rot = jnp.tile(vals, (1, tq))[:, :tq * n_off].reshape(-1, tq, n_off)
    toep = rot[:, :, tq - 1:tq - 1 + n_keys]
    qp = np.arange(tq)[:, None]
    kp = key0 + np.arange(n_keys)[None, :]
    cs = (qp // CHUNK) * CHUNK
    band = np.logical_and(kp >= cs - BAND_PAST, kp < cs + CHUNK)
    return jnp.where(jnp.asarray(band)[None], toep * math.log2(math.e), NEG_BIG).astype(F32)


def _ssd_kernel(dsk_ref, *refs, n_par):
    seq_refs = [refs[3 * s:3 * s + 3] for s in range(n_par)]
    (cw_ref, cb_ref, dtb_ref, alog_ref, ng_ref, h0_ref, c0_ref,
     o_ref, h_ref, cl_ref, h_sc, xe_sc, y_sc) = refs[3 * n_par:]
    c = pl.program_id(1)
    lc = seq_refs[0][0].shape[0]
    cdim = xe_sc.shape[2]

    @pl.when(c == 0)
    def _():
        h_sc[...] = h0_ref[...]
        for s in range(n_par):
            xe_sc[s, 0:8, :] = jnp.zeros((8, cdim), F32)
            xe_sc[s, 8 - (SSD_CONV - 1):8, :] = c0_ref[s]

    r_i = lax.broadcasted_iota(I32, (lc, lc), 0)
    c_i = lax.broadcasted_iota(I32, (lc, lc), 1)
    causal = c_i <= r_i
    tri = causal.astype(F32)
    low = lax.broadcasted_iota(I32, (lc, LANE), 1) < SSD_HD
    row_low = lax.broadcasted_iota(I32, (LANE, SSD_STATE), 0) < SSD_HD
    gs = SSD_GROUPS * SSD_STATE
    hpg = SSD_HEADS // SSD_GROUPS
    gw = SSD_INNER // SSD_GROUPS
    for s, (z_ref, xbc_ref, dt_ref) in enumerate(seq_refs):
        xe_sc[s, 8:8 + lc, :] = xbc_ref[...]
        conv = cb_ref[...] + cw_ref[SSD_CONV - 1:SSD_CONV, :] * xe_sc[s, 8:8 + lc, :]
        for sh in range(1, SSD_CONV):
            conv = conv + cw_ref[SSD_CONV - 1 - sh:SSD_CONV - sh, :] * xe_sc[s, 8 - sh:8 - sh + lc, :]
        u = _silu(conv)
        xs = u[:, :SSD_INNER]
        bm = u[:, SSD_INNER:SSD_INNER + gs].astype(BF16)
        cm = u[:, SSD_INNER + gs:].astype(BF16)
        dx = dt_ref[...] + dtb_ref[...]
        dtv = jnp.maximum(dx, 0.0) + jnp.log1p(jnp.exp(-jnp.abs(dx)))
        a = dtv * (-jnp.exp(alog_ref[...]))
        acum = _dot_f32(tri, a)
        acum_t = acum.T
        for g in range(SSD_GROUPS):
            ss = slice(g * SSD_STATE, (g + 1) * SSD_STATE)
            cb = _dot_nt(cm[:, ss], bm[:, ss])
            for pp in range(hpg // 2):
                j0 = g * hpg + 2 * pp
                j1 = j0 + 1
                pair = j0 // 2
                ls = slice(pair * LANE, (pair + 1) * LANE)
                col0, col1 = acum[:, j0:j0 + 1], acum[:, j1:j1 + 1]
                last0, last1 = acum[lc - 1:lc, j0:j0 + 1], acum[lc - 1:lc, j1:j1 + 1]
                x_p = xs[:, ls]
                xdt = x_p * jnp.where(low, dtv[:, j0:j0 + 1], dtv[:, j1:j1 + 1])
                xdt_b = xdt.astype(BF16)
                ys = []
                for j, col in ((j0, col0), (j1, col1)):
                    lmat = jnp.exp(jnp.where(causal, col - acum_t[j:j + 1, :], -jnp.inf))
                    ys.append(_dot((cb * lmat).astype(BF16), xdt_b))
                h_old = h_sc[s, pair]
                y = jnp.where(low, ys[0], ys[1])
                y = y + _dot_nt(cm[:, ss], h_old.astype(BF16)) * jnp.exp(jnp.where(low, col0, col1))
                y = y + jnp.where(low, dsk_ref[j0], dsk_ref[j1]) * x_p
                dec = jnp.exp(jnp.where(low, last0 - col0, last1 - col1))
                st = _dot_tn((xdt * dec).astype(BF16), bm[:, ss])
                h_sc[s, pair] = h_old * jnp.where(row_low, jnp.exp(last0), jnp.exp(last1)) + st
                y_sc[s, :, ls] = y
        yd = y_sc[s] * _silu(z_ref[...])
        for g in range(SSD_GROUPS):
            ws = slice(g * gw, (g + 1) * gw)
            yg = yd[:, ws]
            ms = jnp.mean(yg * yg, axis=-1, keepdims=True)
            o_ref[0, s, :, ws] = yg * lax.rsqrt(ms + LN_EPS) * ng_ref[:, ws]
        xe_sc[s, 0:8, :] = xe_sc[s, lc:lc + 8, :]

    @pl.when(c == pl.num_programs(1) - 1)
    def _():
        h_ref[...] = h_sc[...]
        for s in range(n_par):
            cl_ref[s] = xe_sc[s, 8 - (SSD_CONV - 1):8, :]


def _ssd(po, h0, c0, conv_w, conv_b, dt_bias, a_log, d_skip, norm_g, n_seq, t, row0):
    n_par = SEQ_PAR
    lc = min(SSD_LC, t)
    nc = t // lc
    blk0 = row0 // lc
    cdim = conv_w.shape[1]

    def rows(s, wd, j):
        return pl.BlockSpec((lc, wd), lambda b, c: (blk0 + (b * n_par + s) * nc + c, j))

    def const(shape):
        nd = len(shape)
        return pl.BlockSpec(shape, lambda b, c: (0,) * nd)

    pad8 = lambda v: jnp.zeros((1, LANE), F32).at[0, :SSD_HEADS].set(v)
    seq_specs, seq_args = [], []
    for s in range(n_par):
        seq_specs += [rows(s, 512, 3), rows(s, cdim, 2), rows(s, LANE, 24)]
        seq_args += [po, po, po]
    h_spec = pl.BlockSpec((n_par, SSD_HEADS // 2, 2 * SSD_HD, SSD_STATE), lambda b, c: (b, 0, 0, 0))
    c_spec = pl.BlockSpec((n_par, SSD_CONV - 1, cdim), lambda b, c: (b, 0, 0))
    y, h_last, c_last = pl.pallas_call(
        functools.partial(_ssd_kernel, n_par=n_par),
        grid=(n_seq // n_par, nc),
        in_specs=[pl.BlockSpec(memory_space=pltpu.SMEM)] + seq_specs + [
            const((SSD_CONV, cdim)), const((1, cdim)), const((1, LANE)), const((1, LANE)),
            const((1, SSD_INNER)), h_spec, c_spec],
        out_specs=[pl.BlockSpec((1, n_par, lc, 512), lambda b, c: (b, 0, c, 0)), h_spec, c_spec],
        out_shape=[jax.ShapeDtypeStruct((n_seq // n_par, n_par, t, 512), F32),
                   jax.ShapeDtypeStruct((n_seq, SSD_HEADS // 2, 2 * SSD_HD, SSD_STATE), F32),
                   jax.ShapeDtypeStruct((n_seq, SSD_CONV - 1, cdim), F32)],
        scratch_shapes=[pltpu.VMEM((n_par, SSD_HEADS // 2, 2 * SSD_HD, SSD_STATE), F32),
                        pltpu.VMEM((n_par, lc + 8, cdim), F32),
                        pltpu.VMEM((n_par, lc, 512), F32)],
        compiler_params=_cparams(2),
        name="ssd_scan",
    )(d_skip, *seq_args, conv_w, conv_b.reshape(1, -1), pad8(dt_bias), pad8(a_log),
      norm_g.reshape(1, -1), h0.reshape(n_seq, SSD_HEADS // 2, 2 * SSD_HD, SSD_STATE), c0)
    return (y.reshape(n_seq * t, 512), h_last.reshape(n_seq, SSD_HEADS, SSD_HD, SSD_STATE), c_last)


def _pad_cols(w, width):
    return jnp.concatenate([w, jnp.zeros((w.shape[0], width - w.shape[1]), w.dtype)], axis=1)


def kernel(x_prompt, x_sample, state_ret, cache_dsa_k, cache_dsa_v, cache_dsa_kidx, cache_band_k, cache_band_v, state_ssm, state_conv, e_w_in, e_w_out, e_gn_g, e_gn_b, o_w_in, o_w_out, o_rel_bias, o_conv_w, o_conv_b, o_dt_bias, o_a_log, o_d_skip, o_norm_g, ln1_g, ln1_b, ln2_g, ln2_b, router_w, router_b, exp_w_gu, exp_b_gu, exp_w_dn, exp_b_dn):
    bp, tp, _ = x_prompt.shape
    bs, ts, _ = x_sample.shape
    past = cache_dsa_k.shape[2]
    n_p, n_s = bp * tp, bs * ts
    n = n_p + n_s
    tm = math.gcd(512, math.gcd(n_p, n_s))
    assert tp % tm == 0 and tm % ts == 0 and ts == CHUNK

    x_parts = (x_prompt.reshape(n_p, D_MODEL), x_sample.reshape(n_s, D_MODEL))

    pe = _proj(x_parts, _pad_cols(e_w_in[0], EVEN_W).astype(BF16), tm)
    pos_p = jnp.arange(tp, dtype=I32)
    pos_s = past + jnp.arange(ts, dtype=I32)
    pos_tab = jnp.concatenate([pos_p, jnp.tile(pos_s, tm // ts)])
    tabs = (_rope_tables(pos_tab, RET_HEADS, RET_DK, RET_DK, RET_THETA),
            _rope_tables(pos_tab, RET_HEADS, RET_DK, RET_DK, RET_THETA, scale=RET_DK ** -0.5),
            _rope_tables(pos_tab, DSA_HEADS, DSA_HD, DSA_ROT, ROPE_THETA),
            _rope_tables(pos_tab, 1, IDX_DIM, DSA_ROT, ROPE_THETA, pad_to=LANE))
    (qa, ka, qb, kb, iq, ikw, q_st, iq_st, k_hm, ik_bf, v_t, iw_t,
     k5_p, v5_p, kidx_p, k5_s, v5_s, kidx_s) = _even_prep(pe, tabs, tm, n_p // tm, tp // tm)

    ya_p, ret_p = _retention(qa, ka, pe, jnp.zeros((bp, RET_HEADS, RET_DK, RET_DV), F32),
                             e_gn_g[0], e_gn_b[0], bp, tp, 0)
    ya_s, ret_s = _retention(qa, ka, pe, state_ret[0], e_gn_g[0], e_gn_b[0], bs, ts, n_p)

    topk_p = min(DSA_TOPK_MAX, tp // 4)
    qlim_p = (((pos_p // CHUNK) + 1) * CHUNK).reshape(1, tp)
    nq_p = tp // DSA_TQ
    nkb_p = ((jnp.arange(nq_p, dtype=I32) + 1) * DSA_TQ + DSA_TK - 1) // DSA_TK
    yb_p = _dsa(q_st, iq_st, iw_t, qlim_p, nkb_p, k_hm, v_t, ik_bf, bp, nq_p, tp, DSA_TQ, topk_p)

    s_len = past + ts
    s_pad = -(-s_len // (2 * DSA_TK)) * (2 * DSA_TK)
    topk_s = min(DSA_TOPK_MAX, s_len // 4)
    group = DSA_HEADS // DSA_KV_HEADS

    def cat_keys(cache, new, wd):
        zpad = jnp.zeros((bs, s_pad - s_len, wd), F32)
        return jnp.concatenate([cache, new.reshape(bs, ts, wd), zpad], axis=1)

    def pad_q(x):
        return jnp.concatenate([x, jnp.zeros((bs, DSA_TQ - ts) + x.shape[2:], x.dtype)], axis=1)

    ks = cat_keys(cache_dsa_k[0].reshape(bs, past, LANE), kb[n_p:], LANE)
    vs = cat_keys(cache_dsa_v[0].reshape(bs, past, LANE), pe[n_p:, 2176:2304], LANE)
    iks = cat_keys(cache_dsa_kidx[0], ikw[n_p:, :IDX_DIM], IDX_DIM)
    k_hm_s = ks.reshape(bs, s_pad, DSA_KV_HEADS, DSA_HD).transpose(2, 0, 1, 3).reshape(
        DSA_KV_HEADS, bs * s_pad, DSA_HD).astype(BF16)
    v_t_s = vs.reshape(bs * (s_pad // DSA_TK), DSA_TK, LANE).transpose(0, 2, 1)
    ones_s = jnp.ones((v_t_s.shape[0], DSA_VT_ROWS - DSA_HD, DSA_TK), F32)
    v_t_s = jnp.concatenate([v_t_s[:, :DSA_HD], ones_s, v_t_s[:, DSA_HD:], ones_s], axis=1).astype(BF16)
    ik_s = iks.reshape(bs * s_pad, IDX_DIM).astype(BF16)
    q_s = pad_q(qb[n_p:].reshape(bs, ts, DSA_KV_HEADS, group, DSA_HD))
    q_st_s = q_s.transpose(2, 0, 3, 1, 4).reshape(DSA_KV_HEADS, bs * group * DSA_TQ, DSA_HD)
    iq_s = pad_q(iq[n_p:].reshape(bs, ts, IDX_HEADS, IDX_DIM))
    iq_st_s = iq_s.transpose(0, 2, 1, 3).reshape(bs * IDX_HEADS * DSA_TQ, IDX_DIM)
    iw_t_s = pad_q(ikw[n_p:, IDX_DIM:IDX_DIM + 8].reshape(bs, ts, 8)).reshape(bs * DSA_TQ, 8).T
    qlim_s = jnp.full((1, DSA_TQ), s_len, I32)
    nkb_s = jnp.full((1,), -(-s_len // DSA_TK), I32)
    yb_s = _dsa(q_st_s, iq_st_s, iw_t_s, qlim_s, nkb_s, k_hm_s, v_t_s, ik_s, bs, 1, s_pad, ts, topk_s)

    h, h_packed = _outproj_ln((ya_p, ya_s), (yb_p, yb_s), e_w_out[0].astype(BF16), x_parts, ln1_g[0], ln1_b[0],
                              tm, n_p // tm)
    h = _moe_layer(h, h_packed, 0, router_w[0], router_b[0], exp_w_gu, exp_b_gu, exp_w_dn, exp_b_dn,
                   ln2_g[0], ln2_b[0], tm)

    po = _proj((h,), _pad_cols(o_w_in[0], ODD_W).astype(BF16), tm)
    tq_p = min(BAND_TQ, tp)
    nkb_band = BAND_PAST // tq_p + 1
    bias_p = _band_bias(o_rel_bias[0], tq_p, nkb_band * tq_p, -(nkb_band - 1) * tq_p)
    yc_p = _band(po, po, po, (0, 1, 2), bias_p, bp, tp, tq_p, tq_p, nkb_band, 0, tp // tq_p)
    band_len = cache_band_k.shape[2]
    kc_new = po[n_p:, 512:1024].reshape(bs, ts, 512)
    vc_new = po[n_p:, 1024:1536].reshape(bs, ts, 512)
    kcat = jnp.concatenate([cache_band_k[0].reshape(bs, band_len, 512), kc_new], axis=1)
    vcat = jnp.concatenate([cache_band_v[0].reshape(bs, band_len, 512), vc_new], axis=1)
    wlen = band_len + ts
    bias_s = _band_bias(o_rel_bias[0], ts, wlen, -band_len)
    yc_s = _band(po, kcat.reshape(bs * wlen, 512), vcat.reshape(bs * wlen, 512), (0, 0, 0), bias_s,
                 bs, ts, ts, wlen, 1, n_p, 1)

    ssd_w = (o_conv_w[0], o_conv_b[0], o_dt_bias[0], o_a_log[0], o_d_skip[0], o_norm_g[0])
    cdim = o_conv_w.shape[2]
    yd_p, ssm_p, conv_p = _ssd(po, jnp.zeros((bp, SSD_HEADS, SSD_HD, SSD_STATE), F32),
                               jnp.zeros((bp, SSD_CONV - 1, cdim), F32), *ssd_w, bp, tp, 0)
    yd_s, ssm_s, conv_s = _ssd(po, state_ssm[0], state_conv[0], *ssd_w, bs, ts, n_p)

    h, h_packed = _outproj_ln((yc_p, yc_s), (yd_p, yd_s), o_w_out[0].astype(BF16), (h,), ln1_g[1], ln1_b[1],
                              tm, n_p // tm)
    h_p, h_s = _moe_layer(h, h_packed, 1, router_w[1], router_b[1], exp_w_gu, exp_b_gu, exp_w_dn, exp_b_dn,
                          ln2_g[1], ln2_b[1], tm, split_rows=n_p)

    keep = min(BAND_PAST, tp)
    tmb = math.gcd(keep, 256)
    kept = (jnp.arange(bp, dtype=I32)[:, None] * (tp // tmb) + (tp - keep) // tmb
            + jnp.arange(keep // tmb, dtype=I32)[None, :]).reshape(-1)
    kc_p, vc_p = _head_rows(po, kept, tmb)
    kc_s, vc_s = _head_rows(po, n_p // ts + jnp.arange(bs, dtype=I32), ts)
    return (h_p.reshape(bp, tp, D_MODEL), h_s.reshape(bs, ts, D_MODEL),
            ret_p[None],
            k5_p.reshape(1, bp, tp, DSA_KV_HEADS, DSA_HD),
            v5_p.reshape(1, bp, tp, DSA_KV_HEADS, DSA_HD),
            kidx_p.reshape(1, bp, tp, IDX_DIM),
            kc_p.reshape(1, bp, keep, BAND_HEADS, BAND_HD), vc_p.reshape(1, bp, keep, BAND_HEADS, BAND_HD),
            ssm_p[None], conv_p[None],
            ret_s[None],
            k5_s.reshape(1, bs, ts, DSA_KV_HEADS, DSA_HD),
            v5_s.reshape(1, bs, ts, DSA_KV_HEADS, DSA_HD),
            kidx_s.reshape(1, bs, ts, IDX_DIM),
            kc_s.reshape(1, bs, ts, BAND_HEADS, BAND_HD), vc_s.reshape(1, bs, ts, BAND_HEADS, BAND_HD),
            ssm_s[None], conv_s[None])
```

```python
import functools
import math

import jax
import jax.numpy as jnp
import numpy as np
from jax import lax
from jax.experimental import pallas as pl
from jax.experimental.pallas import tpu as pltpu

F32 = jnp.float32
BF16 = jnp.bfloat16
I32 = jnp.int32
U32 = jnp.uint32

D_MODEL = 1024
CHUNK = 64
RET_HEADS, RET_DK, RET_DV, RET_THETA = 8, 32, 64, 10000.0
DSA_HEADS, DSA_KV_HEADS, DSA_HD = 8, 2, 64
DSA_ROT = DSA_HD // 4
IDX_HEADS, IDX_DIM = 4, 64
DSA_TOPK_MAX = 256
ROPE_THETA = 500000.0
BAND_HEADS, BAND_HD, BAND_PREV = 8, 64, 8
BAND_PAST = BAND_PREV * CHUNK
REL_CLIP = 256
SSD_HEADS, SSD_HD, SSD_GROUPS, SSD_STATE, SSD_CONV = 8, 64, 2, 128, 4
SSD_INNER = SSD_HEADS * SSD_HD
N_EXPERTS, TOP_K, D_FF = 32, 4, 1024
SWIGLU_LIMIT, SWIGLU_ALPHA = 7.0, 1.702
DEPTH = 2
DN_ALPHA = (2 * DEPTH) ** 0.25
LN_EPS = 1e-5

LANE = 128
VMEM_LIMIT = 56 * 1024 * 1024
INT_MIN = -(2 ** 31)
NEG_BIG = -1e30

EVEN_W = 2688
ODD_W = 3200

MOE_TM = 512
MOE_PARTS = 4
RET_LC = 256
SSD_LC = 256
DSA_TQ = 128
DSA_TK = 256
DSA_VT_ROWS = 80
BAND_TQ = 256


def _cparams(n_axes):
    return pltpu.CompilerParams(dimension_semantics=("arbitrary",) * n_axes,
                                vmem_limit_bytes=VMEM_LIMIT)


def _dot(a, b):
    return jnp.dot(a, b, preferred_element_type=F32)


def _dot_nt(a, b):
    return lax.dot_general(a, b, (((1,), (1,)), ((), ())), preferred_element_type=F32)


def _dot_tn(a, b):
    return lax.dot_general(a, b, (((0,), (0,)), ((), ())), preferred_element_type=F32)


def _dot_f32(a, b):
    return jnp.dot(a, b, preferred_element_type=F32, precision=lax.Precision.HIGHEST)


def _layer_norm(x, g, b):
    mu = jnp.mean(x, axis=-1, keepdims=True)
    xc = x - mu
    var = jnp.mean(xc * xc, axis=-1, keepdims=True)
    return xc * lax.rsqrt(var + LN_EPS) * g + b


def _silu(x):
    return x * jax.nn.sigmoid(x)


def _pack_pairs(x):
    c = x.shape[1] // 2
    hi = pltpu.bitcast(x[:, :c].astype(jnp.bfloat16).astype(F32), U32)
    lo = pltpu.bitcast(x[:, c:].astype(jnp.bfloat16).astype(F32), U32)
    return hi | (lo >> 16)


def _unpack_pairs(w):
    return (pltpu.bitcast(w & jnp.uint32(0xFFFF0000), F32), pltpu.bitcast(w << 16, F32))


def _row_specs(parts, tm):
    width = parts[0].shape[1]
    if len(parts) == 1:
        return [pl.BlockSpec((tm, width), lambda i: (i, 0))], parts[0].shape[0] // tm
    n_first = parts[0].shape[0] // tm
    return [pl.BlockSpec((tm, width), lambda i: (jnp.minimum(i, n_first - 1), 0)),
            pl.BlockSpec((tm, width), lambda i: (jnp.maximum(i - n_first, 0), 0))], n_first


def _read_rows(refs, n_first):
    if len(refs) == 1:
        return refs[0][...]
    return jnp.where(pl.program_id(0) < n_first, refs[0][...], refs[1][...])


def _proj_kernel(*refs, n_parts, n_first):
    w_ref, o_ref = refs[n_parts:]
    o_ref[...] = _dot(_read_rows(refs[:n_parts], n_first).astype(BF16), w_ref[...])


def _proj(x_parts, w, tm):
    n = sum(p.shape[0] for p in x_parts)
    k, wd = w.shape
    x_specs, n_first = _row_specs(x_parts, tm)
    return pl.pallas_call(
        functools.partial(_proj_kernel, n_parts=len(x_parts), n_first=n_first),
        grid=(n // tm,),
        in_specs=x_specs + [pl.BlockSpec((k, wd), lambda i: (0, 0))],
        out_specs=pl.BlockSpec((tm, wd), lambda i: (i, 0)),
        out_shape=jax.ShapeDtypeStruct((n, wd), F32),
        compiler_params=_cparams(1),
        name="in_proj",
    )(*x_parts, w)


def _rope_tables(pos, n_heads, d, rot, theta, scale=1.0, pad_to=None):
    half = rot // 2
    inv = theta ** (-jnp.arange(half, dtype=F32) / half)
    ang = pos.astype(F32)[:, None] * inv[None, :]
    cos, sin = jnp.cos(ang), jnp.sin(ang)
    p = pos.shape[0]
    one = jnp.ones((p, d - rot), F32)
    zr = jnp.zeros((p, d - rot), F32)
    zh = jnp.zeros((p, half), F32)
    c = jnp.tile(jnp.concatenate([cos, cos, one], 1), (1, n_heads))
    a = jnp.tile(jnp.concatenate([-sin, zh, zr], 1), (1, n_heads))
    b = jnp.tile(jnp.concatenate([zh, sin, zr], 1), (1, n_heads))
    if pad_to is not None and pad_to > n_heads * d:
        extra = pad_to - n_heads * d
        c = jnp.concatenate([c, jnp.ones((p, extra), F32)], 1)
        a = jnp.concatenate([a, jnp.zeros((p, extra), F32)], 1)
        b = jnp.concatenate([b, jnp.zeros((p, extra), F32)], 1)
    return jnp.stack([c, a, b]) * scale


def _rope(x, tab_ref, half):
    w = x.shape[-1]
    return (x * tab_ref[0] + pltpu.roll(x, w - half, 1) * tab_ref[1]
            + pltpu.roll(x, half, 1) * tab_ref[2])


def _even_prep_kernel(qa_ref, ka_ref, qb_ref, kb_ref, iq_ref, ikw_ref, v_ref,
                      tq_ref, tk_ref, td_ref, ti_ref,
                      qa_o, ka_o, qb_o, kb_o, iq_o, ikw_o, qst_o, iqst_o, khm_o, ikb_o, vt_o, iwt_o,
                      k5p_o, v5p_o, xp_o, k5s_o, v5s_o, xs_o, *, n_prompt_blocks):
    tm = qa_ref.shape[0]
    h = DSA_ROT // 2
    qa_o[...] = _rope(qa_ref[...], tq_ref, RET_DK // 2)
    ka_o[...] = _rope(ka_ref[...], tk_ref, RET_DK // 2)
    qb = (_rope(qb_ref[...], td_ref, h) * (DSA_HD ** -0.5 * math.log2(math.e))).astype(BF16)
    qb_o[...] = qb
    kb = kb_ref[...]
    kb = (kb * td_ref[0, :, :LANE] + pltpu.roll(kb, LANE - h, 1) * td_ref[1, :, :LANE]
          + pltpu.roll(kb, h, 1) * td_ref[2, :, :LANE])
    kb_o[...] = kb
    iq = iq_ref[...]
    w = iq.shape[-1]
    iq = (iq * td_ref[0, :, :w] + pltpu.roll(iq, w - h, 1) * td_ref[1, :, :w]
          + pltpu.roll(iq, h, 1) * td_ref[2, :, :w]).astype(BF16)
    iq_o[...] = iq
    ikw = _rope(ikw_ref[...], ti_ref, h)
    ikw_o[...] = ikw
    group = DSA_HEADS // DSA_KV_HEADS
    for jb in range(tm // DSA_TQ):
        rs = slice(jb * DSA_TQ, (jb + 1) * DSA_TQ)
        for hd in range(DSA_HEADS):
            n, g = divmod(hd, group)
            ro = (jb * group + g) * DSA_TQ
            qst_o[n, ro:ro + DSA_TQ, :] = qb[rs, hd * DSA_HD:(hd + 1) * DSA_HD]
        for hd in range(IDX_HEADS):
            ro = (jb * IDX_HEADS + hd) * DSA_TQ
            iqst_o[ro:ro + DSA_TQ, :] = iq[rs, hd * IDX_DIM:(hd + 1) * IDX_DIM]
    kbb = kb.astype(BF16)
    for n in range(DSA_KV_HEADS):
        khm_o[n] = kbb[:, n * DSA_HD:(n + 1) * DSA_HD]
    ikb_o[...] = ikw[:, :IDX_DIM].astype(BF16)
    v = v_ref[...]
    ones = jnp.ones((DSA_VT_ROWS - DSA_HD, DSA_TK), F32)
    for j in range(tm // DSA_TK):
        vt = v[j * DSA_TK:(j + 1) * DSA_TK, :].T
        vt_o[j] = jnp.concatenate([vt[:DSA_HD], ones, vt[DSA_HD:], ones], axis=0).astype(BF16)
    iwt_o[...] = ikw.T[IDX_DIM:IDX_DIM + 8, :]

    def cache_rows(k_o, v_o, x_o):
        for n in range(DSA_KV_HEADS):
            k_o[:, n, :] = kb[:, n * DSA_HD:(n + 1) * DSA_HD]
            v_o[:, n, :] = v[:, n * DSA_HD:(n + 1) * DSA_HD]
        x_o[...] = ikw[:, :IDX_DIM]

    @pl.when(pl.program_id(0) < n_prompt_blocks)
    def _():
        cache_rows(k5p_o, v5p_o, xp_o)

    @pl.when(pl.program_id(0) >= n_prompt_blocks)
    def _():
        cache_rows(k5s_o, v5s_o, xs_o)


def _even_prep(pe, tabs, tm, n_prompt_blocks, tab_blocks):
    n = pe.shape[0]
    tq, tk, td, ti = tabs

    def tix(i):
        return (0, jnp.where(i < n_prompt_blocks, i % tab_blocks, tab_blocks), 0)

    def col(wd, j):
        return pl.BlockSpec((tm, wd), lambda i: (i, j))

    def tab(wd):
        return pl.BlockSpec((3, tm, wd), tix)

    def out(wd):
        return pl.BlockSpec((tm, wd), lambda i: (i, 0))

    group = DSA_HEADS // DSA_KV_HEADS
    n_prompt = n_prompt_blocks * tm
    cache_p = pl.BlockSpec((tm, DSA_KV_HEADS, DSA_HD),
                           lambda i: (jnp.minimum(i, n_prompt_blocks - 1), 0, 0))
    cache_s = pl.BlockSpec((tm, DSA_KV_HEADS, DSA_HD),
                           lambda i: (jnp.maximum(i - n_prompt_blocks, 0), 0, 0))
    kidx_p = pl.BlockSpec((tm, IDX_DIM), lambda i: (jnp.minimum(i, n_prompt_blocks - 1), 0))
    kidx_s = pl.BlockSpec((tm, IDX_DIM), lambda i: (jnp.maximum(i - n_prompt_blocks, 0), 0))
    return pl.pallas_call(
        functools.partial(_even_prep_kernel, n_prompt_blocks=n_prompt_blocks),
        grid=(n // tm,),
        in_specs=[col(256, 0), col(256, 1), col(512, 3), col(128, 16), col(256, 9), col(128, 20),
                  col(128, 17), tab(256), tab(256), tab(512), tab(128)],
        out_specs=[out(256), out(256), out(512), out(128), out(256), out(128),
                   pl.BlockSpec((DSA_KV_HEADS, group * tm, DSA_HD), lambda i: (0, i, 0)),
                   pl.BlockSpec((IDX_HEADS * tm, IDX_DIM), lambda i: (i, 0)),
                   pl.BlockSpec((DSA_KV_HEADS, tm, DSA_HD), lambda i: (0, i, 0)),
                   pl.BlockSpec((tm, IDX_DIM), lambda i: (i, 0)),
                   pl.BlockSpec((tm // DSA_TK, 2 * DSA_VT_ROWS, DSA_TK), lambda i: (i, 0, 0)),
                   pl.BlockSpec((8, tm), lambda i: (0, i)),
                   cache_p, cache_p, kidx_p, cache_s, cache_s, kidx_s],
        out_shape=[jax.ShapeDtypeStruct((n, 256), F32), jax.ShapeDtypeStruct((n, 256), F32),
                   jax.ShapeDtypeStruct((n, 512), BF16), jax.ShapeDtypeStruct((n, 128), F32),
                   jax.ShapeDtypeStruct((n, 256), BF16), jax.ShapeDtypeStruct((n, 128), F32),
                   jax.ShapeDtypeStruct((DSA_KV_HEADS, group * n, DSA_HD), BF16),
                   jax.ShapeDtypeStruct((IDX_HEADS * n, IDX_DIM), BF16),
                   jax.ShapeDtypeStruct((DSA_KV_HEADS, n, DSA_HD), BF16),
                   jax.ShapeDtypeStruct((n, IDX_DIM), BF16),
                   jax.ShapeDtypeStruct((n // DSA_TK, 2 * DSA_VT_ROWS, DSA_TK), BF16),
                   jax.ShapeDtypeStruct((8, n), F32),
                   jax.ShapeDtypeStruct((n_prompt, DSA_KV_HEADS, DSA_HD), F32),
                   jax.ShapeDtypeStruct((n_prompt, DSA_KV_HEADS, DSA_HD), F32),
                   jax.ShapeDtypeStruct((n_prompt, IDX_DIM), F32),
                   jax.ShapeDtypeStruct((n - n_prompt, DSA_KV_HEADS, DSA_HD), F32),
                   jax.ShapeDtypeStruct((n - n_prompt, DSA_KV_HEADS, DSA_HD), F32),
                   jax.ShapeDtypeStruct((n - n_prompt, IDX_DIM), F32)],
        compiler_params=_cparams(1),
        name="even_rope",
    )(pe, pe, pe, pe, pe, pe, pe, tq, tk, td, ti)


def _ret_kernel(gch_ref, q_ref, k_ref, v_ref, g_ref, dm_ref, qd_ref, kd_ref, gng_ref, gnb_ref,
                avg_ref, s0_ref, o_ref, s_ref, s_sc, y_sc):
    c = pl.program_id(1)

    @pl.when(c == 0)
    def _():
        s_sc[...] = s0_ref[0]

    q = q_ref[...]
    k = k_ref[...]
    qx = (q * qd_ref[...]).astype(BF16)
    kw = (k * kd_ref[...]).astype(BF16)
    qb = q.astype(BF16)
    kb = k.astype(BF16)
    vb = v_ref[...].astype(BF16)
    for h in range(RET_HEADS):
        ks = slice(h * RET_DK, (h + 1) * RET_DK)
        vs = slice(h * RET_DV, (h + 1) * RET_DV)
        att = _dot_nt(qb[:, ks], kb[:, ks]) * dm_ref[h]
        s_old = s_sc[h]
        y_sc[:, vs] = _dot(att.astype(BF16), vb[:, vs]) + _dot(qx[:, ks], s_old.astype(BF16))
        s_sc[h] = s_old * gch_ref[h] + _dot_tn(kw[:, ks], vb[:, vs])
    y = y_sc[...]
    mu = _dot(y.astype(BF16), avg_ref[...])
    d = y - mu
    var = _dot((d * d).astype(BF16), avg_ref[...])
    yn = d * lax.rsqrt(var + LN_EPS) * gng_ref[...] + gnb_ref[...]
    o_ref[...] = _silu(g_ref[...]) * yn

    @pl.when(c == pl.num_programs(1) - 1)
    def _():
        s_ref[0] = s_sc[...]


def _retention(qa, ka, pe, s0, gn_g, gn_b, n_seq, t, row0):
    lc = min(RET_LC, t)
    nc = t // lc
    blk0 = row0 // lc
    log_g = jnp.log(1.0 - 2.0 ** (-5.0 - jnp.arange(RET_HEADS, dtype=F32)))
    pos = jnp.arange(lc, dtype=F32)
    diff = pos[:, None] - pos[None, :]
    dmask = jnp.where(diff >= 0, jnp.exp(jnp.maximum(diff, 0.0)[None] * log_g[:, None, None]), 0.0)
    w_end = jnp.exp((lc - 1 - pos)[:, None] * log_g[None, :])
    xi = jnp.exp((pos + 1.0)[:, None] * log_g[None, :])
    kdec = jnp.repeat(w_end, RET_DK, axis=1)
    qdec = jnp.repeat(xi, RET_DK, axis=1)
    gch = jnp.exp(lc * log_g)
    head = np.arange(512) // RET_DV
    avg = jnp.asarray((head[:, None] == head[None, :]) / RET_DV, BF16)

    def rows(wd, j):
        return pl.BlockSpec((lc, wd), lambda b, c: (blk0 + b * nc + c, j))

    def const(shape):
        nd = len(shape)
        return pl.BlockSpec(shape, lambda b, c: (0,) * nd)

    state = pl.BlockSpec((1, RET_HEADS, RET_DK, RET_DV), lambda b, c: (b, 0, 0, 0))
    return pl.pallas_call(
        _ret_kernel,
        grid=(n_seq, nc),
        in_specs=[pl.BlockSpec(memory_space=pltpu.SMEM),
                  rows(256, 0), rows(256, 0), rows(512, 1), rows(512, 2),
                  const((RET_HEADS, lc, lc)), const((lc, 256)), const((lc, 256)),
                  const((1, 512)), const((1, 512)), const((512, 512)), state],
        out_specs=[pl.BlockSpec((lc, 512), lambda b, c: (b * nc + c, 0)), state],
        out_shape=[jax.ShapeDtypeStruct((n_seq * t, 512), F32),
                   jax.ShapeDtypeStruct((n_seq, RET_HEADS, RET_DK, RET_DV), F32)],
        scratch_shapes=[pltpu.VMEM((RET_HEADS, RET_DK, RET_DV), F32),
                        pltpu.VMEM((lc, 512), F32)],
        compiler_params=_cparams(2),
        name="retention",
    )(gch, qa, ka, pe, pe, dmask, qdec, kdec, gn_g.reshape(1, 512), gn_b.reshape(1, 512), avg, s0)


def _col_reduce(x, op):
    r, c = x.shape
    return op(op(x.reshape(r // 8, 8, c), axis=0), axis=0, keepdims=True)


def _dsa_kernel(nkb_ref, q_ref, iq_ref, iwt_ref, qlim_ref, k_ref, vt_ref, ik_ref, *rest,
                topk, tq_out):
    o_ref, key_sc, m_sc, l_sc, acc_sc, lga_sc, lgb_sc = rest
    nkb = nkb_ref[pl.program_id(1)]
    tq = qlim_ref.shape[1]
    tk = key_sc.shape[1]
    group = DSA_HEADS // DSA_KV_HEADS
    qlim = qlim_ref[...]
    iwt = iwt_ref[...] * ((IDX_HEADS * IDX_DIM) ** -0.5)
    iqs = iq_ref[...]
    krow = lax.broadcasted_iota(I32, (tk, tq), 0)

    def score_body(kb, carry):
        off = pl.multiple_of(kb * tk, tk)
        s_all = _dot_nt(ik_ref[pl.ds(off, tk), :], iqs)
        s = jnp.zeros((tk, tq), F32)
        for h in range(IDX_HEADS):
            s = s + iwt[h:h + 1, :] * jnp.maximum(s_all[:, h * tq:(h + 1) * tq], 0.0)
        s = jnp.where(s == 0.0, 0.0, s)
        bits = pltpu.bitcast(s, I32)
        key = jnp.where(bits >= 0, bits, bits ^ jnp.int32(0x7FFFFFFF))
        adm = (off + krow) < qlim
        key_sc[kb] = jnp.where(adm, key, jnp.int32(INT_MIN))
        return carry

    n_pair = (nkb + 1) // 2
    last = 2 * n_pair - 1

    def score_pair(j, carry):
        return score_body(2 * j + 1, score_body(2 * j, carry))

    lax.fori_loop(0, n_pair, score_pair, 0)

    def count(pred):
        def body(kb, acc):
            hit = jnp.where(pred(key_sc[kb]), 1.0, 0.0)
            return acc + jnp.sum(hit.reshape(tk // 64, 64, tq), axis=0)
        acc = lax.fori_loop(0, nkb, body, jnp.zeros((64, tq), F32))
        return jnp.sum(acc, axis=0, keepdims=True)

    def bit_body(it, ans):
        cand = ans + (jnp.int32(1) << (31 - it))
        return jnp.where(count(lambda k: k >= cand) >= topk, cand, ans)

    t = lax.fori_loop(0, 32, bit_body, jnp.full((1, tq), INT_MIN, I32))
    need = topk - count(lambda k: k > t)

    m_sc[...] = jnp.full(m_sc.shape, 0.1 * NEG_BIG, F32)
    l_sc[...] = jnp.zeros(l_sc.shape, F32)
    acc_sc[...] = jnp.zeros(acc_sc.shape, F32)
    r_i = lax.broadcasted_iota(I32, (tk, tk), 0)
    c_i = lax.broadcasted_iota(I32, (tk, tk), 1)
    lower = (c_i < r_i).astype(BF16)

    def logits_stage(kb, n_eq, dst):
        off = pl.multiple_of(kb * tk, tk)
        key = key_sc[kb]
        adm = (off + krow) < qlim
        eq = jnp.logical_and(key == t, adm)
        eqf = jnp.where(eq, 1.0, 0.0)
        pref = _dot(lower, eqf.astype(BF16))
        sel = jnp.logical_and(adm, jnp.logical_or(
            key > t, jnp.logical_and(eq, (n_eq + pref) < need)))
        bias = jnp.where(sel, 0.0, NEG_BIG)
        for n in range(DSA_KV_HEADS):
            lg_all = _dot_nt(k_ref[n, pl.ds(off, tk), :], q_ref[n])
            for g in range(group):
                ls = slice(g * tq, (g + 1) * tq)
                dst[n, :, ls] = lg_all[:, ls] + bias
        return n_eq + _col_reduce(eqf, jnp.sum)

    def softmax_stage(kb, src):
        vt = vt_ref[kb]
        for n in range(DSA_KV_HEADS):
            ps, alphas = [], []
            for g in range(group):
                ls = slice(g * tq, (g + 1) * tq)
                lg = src[n, :, ls]
                m_old = m_sc[n, :, ls]
                m_new = jnp.maximum(m_old, _col_reduce(lg, jnp.max))
                m_sc[n, :, ls] = m_new
                ps.append(jnp.exp2(lg - m_new).astype(BF16))
                alphas.append(jnp.exp2(m_old - m_new))
            p_all = jnp.concatenate(ps, axis=1)
            alpha_all = jnp.concatenate(alphas, axis=1)
            pv = _dot(vt[n * DSA_VT_ROWS:(n + 1) * DSA_VT_ROWS, :], p_all)
            acc_sc[n] = alpha_all * acc_sc[n] + pv[:DSA_HD]
            l_sc[n] = alpha_all * l_sc[n] + pv[DSA_HD:DSA_HD + 1]

    def pair_body(j, n_eq):
        kb0 = 2 * j
        softmax_stage(kb0, lga_sc)
        n_eq = logits_stage(kb0 + 1, n_eq, lgb_sc)
        softmax_stage(kb0 + 1, lgb_sc)
        return logits_stage(jnp.minimum(kb0 + 2, last), n_eq, lga_sc)

    lax.fori_loop(0, n_pair, pair_body, logits_stage(0, jnp.zeros((1, tq), F32), lga_sc))
    pieces = []
    for n in range(DSA_KV_HEADS):
        o_n = acc_sc[n] / l_sc[n]
        for g in range(group):
            pieces.append(o_n[:, g * tq:(g + 1) * tq])
    o_ref[...] = jnp.concatenate(pieces, axis=0).T[:tq_out, :]


def _dsa(q_st, iq_st, iw_t, qlim, nkb, k_hm, v_t, ik_bf, n_seq, nq, s_len, tq_out, topk):
    tq = DSA_TQ
    group = DSA_HEADS // DSA_KV_HEADS
    in_specs = [pl.BlockSpec((DSA_KV_HEADS, group * tq, DSA_HD), lambda b, i, s: (0, b * nq + i, 0)),
                pl.BlockSpec((IDX_HEADS * tq, IDX_DIM), lambda b, i, s: (b * nq + i, 0)),
                pl.BlockSpec((8, tq), lambda b, i, s: (0, b * nq + i)),
                pl.BlockSpec((1, tq), lambda b, i, s: (0, i)),
                pl.BlockSpec((DSA_KV_HEADS, s_len, DSA_HD), lambda b, i, s: (0, b, 0)),
                pl.BlockSpec((s_len // DSA_TK, 2 * DSA_VT_ROWS, DSA_TK), lambda b, i, s: (b, 0, 0)),
                pl.BlockSpec((s_len, IDX_DIM), lambda b, i, s: (b, 0))]
    grid_spec = pltpu.PrefetchScalarGridSpec(
        num_scalar_prefetch=1,
        grid=(n_seq, nq),
        in_specs=in_specs,
        out_specs=pl.BlockSpec((tq_out, 512), lambda b, i, s: (b * nq + i, 0)),
        scratch_shapes=[pltpu.VMEM((s_len // DSA_TK, DSA_TK, tq), I32),
                        pltpu.VMEM((DSA_KV_HEADS, 1, group * tq), F32),
                        pltpu.VMEM((DSA_KV_HEADS, 1, group * tq), F32),
                        pltpu.VMEM((DSA_KV_HEADS, DSA_HD, group * tq), F32),
                        pltpu.VMEM((DSA_KV_HEADS, DSA_TK, group * tq), F32),
                        pltpu.VMEM((DSA_KV_HEADS, DSA_TK, group * tq), F32)])
    return pl.pallas_call(
        functools.partial(_dsa_kernel, topk=topk, tq_out=tq_out),
        grid_spec=grid_spec,
        out_shape=jax.ShapeDtypeStruct((n_seq * nq * tq_out, 512), F32),
        compiler_params=_cparams(2),
        name="dsa_attention",
    )(nkb, q_st, iq_st, iw_t, qlim, k_hm, v_t, ik_bf)


def _outproj_ln_kernel(*refs, counts, n_first):
    na, nb, nh = counts
    ya = _read_rows(refs[:na], n_first)
    yb = _read_rows(refs[na:na + nb], n_first)
    w_ref = refs[na + nb]
    h = _read_rows(refs[na + nb + 1:na + nb + 1 + nh], n_first)
    g_ref, b_ref, o_ref, op_ref = refs[na + nb + 1 + nh:]
    half = ya.shape[1]
    y = _dot(ya.astype(BF16), w_ref[:half, :]) + _dot(yb.astype(BF16), w_ref[half:, :])
    out = _layer_norm(DN_ALPHA * h + y, g_ref[...], b_ref[...])
    o_ref[...] = out
    op_ref[...] = _pack_pairs(out)


def _outproj_ln(ya_parts, yb_parts, w, h_parts, g, b, tm, n_first):
    n = sum(p.shape[0] for p in h_parts)
    specs = []
    for parts in (ya_parts, yb_parts):
        sp, nf = _row_specs(parts, tm)
        assert len(parts) == 1 or nf == n_first
        specs += sp
    h_specs, nf = _row_specs(h_parts, tm)
    assert len(h_parts) == 1 or nf == n_first
    return pl.pallas_call(
        functools.partial(_outproj_ln_kernel, counts=(len(ya_parts), len(yb_parts), len(h_parts)),
                          n_first=n_first),
        grid=(n // tm,),
        in_specs=specs + [pl.BlockSpec((D_MODEL, D_MODEL), lambda i: (0, 0))] + h_specs + [
                  pl.BlockSpec((1, D_MODEL), lambda i: (0, 0)),
                  pl.BlockSpec((1, D_MODEL), lambda i: (0, 0))],
        out_specs=[pl.BlockSpec((tm, D_MODEL), lambda i: (i, 0)),
                   pl.BlockSpec((tm, D_MODEL // 2), lambda i: (i, 0))],
        out_shape=[jax.ShapeDtypeStruct((n, D_MODEL), F32),
                   jax.ShapeDtypeStruct((n, D_MODEL // 2), U32)],
        compiler_params=_cparams(1),
        name="out_proj_ln",
    )(*ya_parts, *yb_parts, w, *h_parts, g.reshape(1, -1), b.reshape(1, -1))


def _router_kernel(x_ref, w_ref, b_ref, idx_o, gate_o, rank_o, cnt_o, cnt_sc):
    i = pl.program_id(0)

    @pl.when(i == 0)
    def _():
        cnt_sc[...] = jnp.zeros(cnt_sc.shape, F32)

    tm = x_ref.shape[0]
    ne = w_ref.shape[0]
    x = x_ref[...]
    w = w_ref[...]
    x_hi = x.astype(BF16)
    x_lo = (x - x_hi.astype(F32)).astype(BF16)
    w_hi = w.astype(BF16)
    w_lo = (w - w_hi.astype(F32)).astype(BF16)
    logits = (_dot_nt(w_hi, x_hi) + _dot_nt(w_lo, x_hi) + _dot_nt(w_hi, x_lo)) + b_ref[...]
    erow = lax.broadcasted_iota(I32, (ne, tm), 0)
    vals, idxs = [], []
    onehot = jnp.zeros((ne, tm), F32)
    for _ in range(TOP_K):
        m = jnp.max(logits, axis=0, keepdims=True)
        ix = jnp.min(jnp.where(logits == m, erow, ne), axis=0, keepdims=True)
        hit = erow == ix
        onehot = jnp.where(hit, 1.0, onehot)
        logits = jnp.where(hit, -jnp.inf, logits)
        vals.append(m)
        idxs.append(ix)
    es = [jnp.exp(v - vals[0]) for v in vals]
    den = es[0] + es[1] + es[2] + es[3]
    r_i = lax.broadcasted_iota(I32, (tm, tm), 0)
    c_i = lax.broadcasted_iota(I32, (tm, tm), 1)
    upper = (r_i < c_i).astype(BF16)
    rank_dense = _dot(onehot.astype(BF16), upper) + cnt_sc[...]
    prow = lax.broadcasted_iota(I32, (8, tm), 0)
    idx_out = jnp.zeros((8, tm), I32)
    gate_out = jnp.zeros((8, tm), F32)
    rank_out = jnp.zeros((8, tm), F32)
    for k in range(TOP_K):
        rk = jnp.sum(jnp.where(erow == idxs[k], rank_dense, 0.0), axis=0, keepdims=True)
        idx_out = jnp.where(prow == k, idxs[k], idx_out)
        gate_out = jnp.where(prow == k, es[k] / den, gate_out)
        rank_out = jnp.where(prow == k, rk, rank_out)
    idx_o[...] = idx_out
    gate_o[...] = gate_out
    rank_o[...] = rank_out.astype(I32)
    cnt = cnt_sc[...] + jnp.sum(onehot, axis=1, keepdims=True)
    cnt_sc[...] = cnt
    cnt_o[...] = cnt


def _router(x, w_r, b_r, tm):
    n = x.shape[0]
    row = pl.BlockSpec((8, tm), lambda i: (0, i))
    return pl.pallas_call(
        _router_kernel,
        grid=(n // tm,),
        in_specs=[pl.BlockSpec((tm, D_MODEL), lambda i: (i, 0)),
                  pl.BlockSpec((N_EXPERTS, D_MODEL), lambda i: (0, 0)),
                  pl.BlockSpec((N_EXPERTS, 1), lambda i: (0, 0))],
        out_specs=[row, row, row, pl.BlockSpec((N_EXPERTS, 1), lambda i: (0, 0))],
        out_shape=[jax.ShapeDtypeStruct((8, n), I32), jax.ShapeDtypeStruct((8, n), F32),
                   jax.ShapeDtypeStruct((8, n), I32), jax.ShapeDtypeStruct((N_EXPERTS, 1), F32)],
        scratch_shapes=[pltpu.VMEM((N_EXPERTS, 1), F32)],
        compiler_params=_cparams(1),
        name="moe_router",
    )(x, w_r.T, b_r.reshape(N_EXPERTS, 1))


def _moe_kernel(be_ref, nu_ref, *rest, n_parts, blocks_per_part):
    x_refs = rest[:n_parts]
    wgu_ref, bgu_ref, wdn_ref, bdn_ref, o_ref, wgu_sc, wdn_sc = rest[n_parts:]
    i = pl.program_id(0)

    @pl.when(jnp.logical_or(i == 0, be_ref[i] != be_ref[jnp.maximum(i - 1, 0)]))
    def _():
        wgu_sc[...] = wgu_ref[0, 0].astype(BF16)
        wdn_sc[...] = wdn_ref[0, 0].astype(BF16)

    @pl.when(i < nu_ref[0])
    def _():
        part = i // blocks_per_part
        xw = x_refs[0][...]
        for c in range(1, n_parts):
            xw = jnp.where(part == c, x_refs[c][...], xw)
        xa, xb = _unpack_pairs(xw)
        x = jnp.concatenate([xa.astype(BF16), xb.astype(BF16)], axis=1)
        h = _dot(x, wgu_sc[...]) + bgu_ref[0, 0]
        g = jnp.minimum(h[:, :D_FF], SWIGLU_LIMIT)
        up = jnp.clip(h[:, D_FF:], -SWIGLU_LIMIT, SWIGLU_LIMIT)
        a = (up + 1.0) * g * jax.nn.sigmoid(SWIGLU_ALPHA * g)
        o_ref[...] = _pack_pairs(_dot(a.astype(BF16), wdn_sc[...]) + bdn_ref[0, 0])

    @pl.when(i >= nu_ref[0])
    def _():
        o_ref[...] = jnp.zeros(o_ref.shape, U32)


def _moe_experts(xs_parts, blk_e, n_used, layer, w_gu, b_gu, w_dn, b_dn):
    n_parts = len(xs_parts)
    tm = MOE_TM
    bpp = xs_parts[0].shape[0] // tm
    n_rows = n_parts * bpp * tm
    depth = w_gu.shape[0]

    def x_spec(c):
        return pl.BlockSpec((tm, D_MODEL // 2),
                            lambda i, be, nu: (jnp.clip(i - c * bpp, 0, bpp - 1), 0))
    grid_spec = pltpu.PrefetchScalarGridSpec(
        num_scalar_prefetch=2,
        grid=(n_rows // tm,),
        in_specs=[x_spec(c) for c in range(n_parts)] + [
                  pl.BlockSpec((1, 1, D_MODEL, 2 * D_FF), lambda i, be, nu: (layer, be[i], 0, 0)),
                  pl.BlockSpec((1, 1, 1, 2 * D_FF), lambda i, be, nu: (layer, be[i], 0, 0)),
                  pl.BlockSpec((1, 1, D_FF, D_MODEL), lambda i, be, nu: (layer, be[i], 0, 0)),
                  pl.BlockSpec((1, 1, 1, D_MODEL), lambda i, be, nu: (layer, be[i], 0, 0))],
        out_specs=pl.BlockSpec((tm, D_MODEL // 2), lambda i, be, nu: (i, 0)),
        scratch_shapes=[pltpu.VMEM((D_MODEL, 2 * D_FF), BF16), pltpu.VMEM((D_FF, D_MODEL), BF16)])
    return pl.pallas_call(
        functools.partial(_moe_kernel, n_parts=n_parts, blocks_per_part=bpp),
        grid_spec=grid_spec,
        out_shape=jax.ShapeDtypeStruct((n_rows, D_MODEL // 2), U32),
        compiler_params=_cparams(1),
        name="moe_experts",
    )(blk_e, n_used, *xs_parts, w_gu, b_gu.reshape(depth, N_EXPERTS, 1, -1), w_dn,
      b_dn.reshape(depth, N_EXPERTS, 1, -1))


def _combine_ln_kernel(h_ref, y0_ref, y1_ref, y2_ref, y3_ref, gate_ref, g_ref, b_ref, *o_refs,
                       n_first):
    gate = gate_ref[...]
    ya, yb = None, None
    for k, y_ref in enumerate((y0_ref, y1_ref, y2_ref, y3_ref)):
        a, b = _unpack_pairs(y_ref[...])
        gk = gate[:, k:k + 1]
        ya = gk * a if ya is None else ya + gk * a
        yb = gk * b if yb is None else yb + gk * b
    y = jnp.concatenate([ya, yb], axis=1)
    out = _layer_norm(DN_ALPHA * h_ref[...] + y, g_ref[...], b_ref[...])
    if len(o_refs) == 1:
        o_refs[0][...] = out
    else:
        @pl.when(pl.program_id(0) < n_first)
        def _():
            o_refs[0][...] = out

        @pl.when(pl.program_id(0) >= n_first)
        def _():
            o_refs[1][...] = out


def _combine_ln(h, ys, gate, g, b, tm, split_rows=None):
    n = h.shape[0]
    row = pl.BlockSpec((tm, D_MODEL), lambda i: (i, 0))
    half = pl.BlockSpec((tm, D_MODEL // 2), lambda i: (i, 0))
    vec = pl.BlockSpec((1, D_MODEL), lambda i: (0, 0))
    if split_rows is None:
        n_first = n // tm
        out_specs = row
        out_shape = jax.ShapeDtypeStruct((n, D_MODEL), F32)
    else:
        n_first = split_rows // tm
        out_specs = [pl.BlockSpec((tm, D_MODEL), lambda i: (jnp.minimum(i, n_first - 1), 0)),
                     pl.BlockSpec((tm, D_MODEL), lambda i: (jnp.maximum(i - n_first, 0), 0))]
        out_shape = [jax.ShapeDtypeStruct((split_rows, D_MODEL), F32),
                     jax.ShapeDtypeStruct((n - split_rows, D_MODEL), F32)]
    return pl.pallas_call(
        functools.partial(_combine_ln_kernel, n_first=n_first),
        grid=(n // tm,),
        in_specs=[row, half, half, half, half, pl.BlockSpec((tm, TOP_K), lambda i: (i, 0)), vec, vec],
        out_specs=out_specs,
        out_shape=out_shape,
        compiler_params=_cparams(1),
        name="moe_combine_ln",
    )(h, ys[0], ys[1], ys[2], ys[3], gate, g.reshape(1, -1), b.reshape(1, -1))


def _rows(x, idx):
    return x.at[idx].get(mode="promise_in_bounds")


def _moe_layer(h, h_packed, layer, w_r, b_r, w_gu, b_gu, w_dn, b_dn, ln_g, ln_b, tm, split_rows=None):
    n = h.shape[0]
    n_pair = n * TOP_K
    idx_t, gate_t, rank_t, cnt = _router(h, w_r, b_r, tm)
    top_i = idx_t[:TOP_K].T
    counts = cnt[:, 0].astype(I32)
    padded = (counts + MOE_TM - 1) // MOE_TM * MOE_TM
    pad_end = jnp.cumsum(padded)
    start = pad_end - padded
    first = jnp.cumsum(counts) - counts
    dest = _rows(start, top_i) + rank_t[:TOP_K].T
    n_blk = -(-(-(-n_pair // MOE_TM) + N_EXPERTS) // MOE_PARTS) * MOE_PARTS
    n_used = (pad_end[-1] // MOE_TM).astype(I32)
    blk_row = jnp.minimum(jnp.arange(n_blk, dtype=I32), n_used - 1) * MOE_TM
    blk_e = jnp.sum((pad_end[None, :] <= blk_row[:, None]).astype(I32), axis=1)
    blk_e = jnp.minimum(blk_e, N_EXPERTS - 1)
    order = jnp.argsort(top_i.reshape(-1), stable=True).astype(I32)
    row_in_e = (jnp.arange(n_blk, dtype=I32)[:, None] * MOE_TM - _rows(start, blk_e)[:, None]
                + jnp.arange(MOE_TM, dtype=I32)[None, :])
    pair = jnp.clip(_rows(first, blk_e)[:, None] + row_in_e, 0, n_pair - 1).reshape(-1)
    src = (_rows(order, pair) // TOP_K).reshape(MOE_PARTS, -1)
    xs_parts = [_rows(h_packed, src[c]) for c in range(MOE_PARTS)]
    ybuf = _moe_experts(xs_parts, blk_e, n_used.reshape(1), layer, w_gu, b_gu, w_dn, b_dn)
    ys = [_rows(ybuf, dest[:, k]) for k in range(TOP_K)]
    return _combine_ln(h, ys, gate_t[:TOP_K].T, ln_g, ln_b, tm, split_rows)


def _band_kernel(q_ref, *rest, nkb):
    k_refs = rest[:nkb]
    v_refs = rest[nkb:2 * nkb]
    bias_ref = rest[2 * nkb]
    o_ref = rest[-1]
    c = pl.program_id(1)
    tq = q_ref.shape[0]
    tkb = k_refs[0].shape[0]
    q_scale = BAND_HD ** -0.5 * math.log2(math.e)
    kbs = [r[...].astype(BF16) for r in k_refs]
    vbs = [r[...].astype(BF16) for r in v_refs]
    off = [jnp.where((c + j - (nkb - 1)) >= 0, 0.0, NEG_BIG) for j in range(nkb)]
    low = lax.broadcasted_iota(I32, (tq, LANE), 1) < BAND_HD
    for pair in range(BAND_HEADS // 2):
        ls = slice(pair * LANE, (pair + 1) * LANE)
        qp = q_ref[:, ls] * q_scale
        outs = []
        for par in range(2):
            h = 2 * pair + par
            qh = jnp.where(low == (par == 0), qp, 0.0).astype(BF16)
            lgs = [_dot_nt(qh, kbs[j][:, ls]) + (bias_ref[h, :, j * tkb:(j + 1) * tkb] + off[j])
                   for j in range(nkb)]
            mx = lgs[0]
            for j in range(1, nkb):
                mx = jnp.maximum(mx, lgs[j])
            m = mx.max(axis=1, keepdims=True)
            ps = [jnp.exp2(lg - m) for lg in lgs]
            sm = ps[0]
            for j in range(1, nkb):
                sm = sm + ps[j]
            den = sm.sum(axis=1, keepdims=True)
            acc = _dot(ps[0].astype(BF16), vbs[0][:, ls])
            for j in range(1, nkb):
                acc = acc + _dot(ps[j].astype(BF16), vbs[j][:, ls])
            outs.append(acc * (1.0 / den))
        o_ref[:, ls] = jnp.where(low, outs[0], outs[1])


def _band(q_arr, k_arr, v_arr, cols, bias, n_seq, t, tq, tkb, nkb, q_row0, kv_blocks_per_seq):
    nq = t // tq
    qblk0 = q_row0 // tq
    qcol, kcol, vcol = cols

    def kv_spec(j, col):
        def ix(b, c):
            return (b * kv_blocks_per_seq + jnp.maximum(c + j - (nkb - 1), 0), col)
        return pl.BlockSpec((tkb, 512), ix)

    in_specs = ([pl.BlockSpec((tq, 512), lambda b, c: (qblk0 + b * nq + c, qcol))]
                + [kv_spec(j, kcol) for j in range(nkb)]
                + [kv_spec(j, vcol) for j in range(nkb)]
                + [pl.BlockSpec(bias.shape, lambda b, c: (0, 0, 0))])
    return pl.pallas_call(
        functools.partial(_band_kernel, nkb=nkb),
        grid=(n_seq, nq),
        in_specs=in_specs,
        out_specs=pl.BlockSpec((tq, 512), lambda b, c: (b * nq + c, 0)),
        out_shape=jax.ShapeDtypeStruct((n_seq * t, 512), F32),
        compiler_params=_cparams(2),
        name="band_attention",
    )(q_arr, *([k_arr] * nkb), *([v_arr] * nkb), bias)


def _head_rows_kernel(src_ref, k_ref, v_ref, ko_ref, vo_ref):
    for h in range(BAND_HEADS):
        hs = slice(h * BAND_HD, (h + 1) * BAND_HD)
        ko_ref[:, h, :] = k_ref[:, hs]
        vo_ref[:, h, :] = v_ref[:, hs]


def _head_rows(po, src_blocks, tmb):
    n_blocks = src_blocks.shape[0]
    out = jax.ShapeDtypeStruct((n_blocks * tmb, BAND_HEADS, BAND_HD), F32)
    grid_spec = pltpu.PrefetchScalarGridSpec(
        num_scalar_prefetch=1,
        grid=(n_blocks,),
        in_specs=[pl.BlockSpec((tmb, 512), lambda i, src: (src[i], 1)),
                  pl.BlockSpec((tmb, 512), lambda i, src: (src[i], 2))],
        out_specs=[pl.BlockSpec((tmb, BAND_HEADS, BAND_HD), lambda i, src: (i, 0, 0))] * 2)
    return pl.pallas_call(_head_rows_kernel, grid_spec=grid_spec, out_shape=[out, out],
                          compiler_params=_cparams(1), name="band_cache_rows")(src_blocks, po, po)


def _band_bias(rel_bias, tq, n_keys, key0):
    n_off = tq + n_keys - 1
    d_max = tq - 1 - key0
    rel = np.clip(d_max - np.arange(n_off), -REL_CLIP, REL_CLIP) + REL_CLIP
    vals = jnp.concatenate([rel_bias[:, rel], jnp.zeros((rel_bias.shape[0], 1), F32)], axis=1)
    rot = jnp.tile(vals, (1, tq))[:, :tq * n_off].reshape(-1, tq, n_off)
    toep = rot[:, :, tq - 1:tq - 1 + n_keys]
    qp = np.arange(tq)[:, None]
    kp = key0 + np.arange(n_keys)[None, :]
    cs = (qp // CHUNK) * CHUNK
    band = np.logical_and(kp >= cs - BAND_PAST, kp < cs + CHUNK)
    return jnp.where(jnp.asarray(band)[None], toep * math.log2(math.e), NEG_BIG).astype(F32)


def _ssd_kernel(dsk_ref, z_ref, xbc_ref, dt_ref, cw_ref, cb_ref, dtb_ref, alog_ref, ng_ref,
                h0_ref, c0_ref, o_ref, h_ref, cl_ref, h_sc, xe_sc, y_sc):
    c = pl.program_id(1)
    lc = z_ref.shape[0]
    cdim = xe_sc.shape[1]

    @pl.when(c == 0)
    def _():
        h_sc[...] = h0_ref[0]
        xe_sc[0:8, :] = jnp.zeros((8, cdim), F32)
        xe_sc[8 - (SSD_CONV - 1):8, :] = c0_ref[0]

    r_i = lax.broadcasted_iota(I32, (lc, lc), 0)
    c_i = lax.broadcasted_iota(I32, (lc, lc), 1)
    causal = c_i <= r_i
    low = lax.broadcasted_iota(I32, (lc, LANE), 1) < SSD_HD
    row_low = lax.broadcasted_iota(I32, (LANE, SSD_STATE), 0) < SSD_HD
    gs = SSD_GROUPS * SSD_STATE
    hpg = SSD_HEADS // SSD_GROUPS
    gw = SSD_INNER // SSD_GROUPS

    xe_sc[8:8 + lc, :] = xbc_ref[...]
    conv = cb_ref[...] + cw_ref[SSD_CONV - 1:SSD_CONV, :] * xe_sc[8:8 + lc, :]
    for sh in range(1, SSD_CONV):
        conv = conv + cw_ref[SSD_CONV - 1 - sh:SSD_CONV - sh, :] * xe_sc[8 - sh:8 - sh + lc, :]
    u = _silu(conv)
    xs = u[:, :SSD_INNER]
    bm = u[:, SSD_INNER:SSD_INNER + gs].astype(BF16)
    cm = u[:, SSD_INNER + gs:].astype(BF16)
    dx = dt_ref[...] + dtb_ref[...]
    dtv = jnp.maximum(dx, 0.0) + jnp.log1p(jnp.exp(-jnp.abs(dx)))
    a = dtv * (-jnp.exp(alog_ref[...]))
    acum = _dot_f32(causal.astype(F32), a)
    acum_t = acum.T
    for g in range(SSD_GROUPS):
        ss = slice(g * SSD_STATE, (g + 1) * SSD_STATE)
        cb = _dot_nt(cm[:, ss], bm[:, ss])
        for pp in range(hpg // 2):
            j0 = g * hpg + 2 * pp
            j1 = j0 + 1
            pair = j0 // 2
            ls = slice(pair * LANE, (pair + 1) * LANE)
            col0, col1 = acum[:, j0:j0 + 1], acum[:, j1:j1 + 1]
            last0, last1 = acum[lc - 1:lc, j0:j0 + 1], acum[lc - 1:lc, j1:j1 + 1]
            x_p = xs[:, ls]
            xdt = x_p * jnp.where(low, dtv[:, j0:j0 + 1], dtv[:, j1:j1 + 1])
            xdt_b = xdt.astype(BF16)
            ys = []
            for j, col in ((j0, col0), (j1, col1)):
                lmat = jnp.exp(jnp.where(causal, col - acum_t[j:j + 1, :], -jnp.inf))
                ys.append(_dot((cb * lmat).astype(BF16), xdt_b))
            h_old = h_sc[pair]
            y = jnp.where(low, ys[0], ys[1])
            y = y + _dot_nt(cm[:, ss], h_old.astype(BF16)) * jnp.exp(jnp.where(low, col0, col1))
            y = y + jnp.where(low, dsk_ref[j0], dsk_ref[j1]) * x_p
            dec = jnp.exp(jnp.where(low, last0 - col0, last1 - col1))
            st = _dot_tn((xdt * dec).astype(BF16), bm[:, ss])
            h_sc[pair] = h_old * jnp.where(row_low, jnp.exp(last0), jnp.exp(last1)) + st
            y_sc[:, ls] = y
    yd = y_sc[...] * _silu(z_ref[...])
    for g in range(SSD_GROUPS):
        ws = slice(g * gw, (g + 1) * gw)
        yg = yd[:, ws]
        ms = jnp.mean(yg * yg, axis=-1, keepdims=True)
        o_ref[:, ws] = yg * lax.rsqrt(ms + LN_EPS) * ng_ref[:, ws]
    xe_sc[0:8, :] = xe_sc[lc:lc + 8, :]

    @pl.when(c == pl.num_programs(1) - 1)
    def _():
        h_ref[0] = h_sc[...]
        cl_ref[0] = xe_sc[8 - (SSD_CONV - 1):8, :]


def _ssd(po, h0, c0, conv_w, conv_b, dt_bias, a_log, d_skip, norm_g, n_seq, t, row0):
    lc = min(SSD_LC, t)
    nc = t // lc
    blk0 = row0 // lc
    cdim = conv_w.shape[1]
    slab = (SSD_HEADS // 2, 2 * SSD_HD, SSD_STATE)

    def rows(wd, j):
        return pl.BlockSpec((lc, wd), lambda b, c: (blk0 + b * nc + c, j))

    def const(shape):
        nd = len(shape)
        return pl.BlockSpec(shape, lambda b, c: (0,) * nd)

    pad8 = lambda v: jnp.zeros((1, LANE), F32).at[0, :SSD_HEADS].set(v)
    h_spec = pl.BlockSpec((1,) + slab, lambda b, c: (b, 0, 0, 0))
    c_spec = pl.BlockSpec((1, SSD_CONV - 1, cdim), lambda b, c: (b, 0, 0))
    y, h_last, c_last = pl.pallas_call(
        _ssd_kernel,
        grid=(n_seq, nc),
        in_specs=[pl.BlockSpec(memory_space=pltpu.SMEM), rows(512, 3), rows(cdim, 2), rows(LANE, 24),
                  const((SSD_CONV, cdim)), const((1, cdim)), const((1, LANE)), const((1, LANE)),
                  const((1, SSD_INNER)), h_spec, c_spec],
        out_specs=[pl.BlockSpec((lc, 512), lambda b, c: (b * nc + c, 0)), h_spec, c_spec],
        out_shape=[jax.ShapeDtypeStruct((n_seq * t, 512), F32),
                   jax.ShapeDtypeStruct((n_seq,) + slab, F32),
                   jax.ShapeDtypeStruct((n_seq, SSD_CONV - 1, cdim), F32)],
        scratch_shapes=[pltpu.VMEM(slab, F32),
                        pltpu.VMEM((lc + 8, cdim), F32),
                        pltpu.VMEM((lc, 512), F32)],
        compiler_params=_cparams(2),
        name="ssd_scan",
    )(d_skip, po, po, po, conv_w, conv_b.reshape(1, -1), pad8(dt_bias), pad8(a_log),
      norm_g.reshape(1, -1), h0.reshape((n_seq,) + slab), c0)
    return y, h_last.reshape(n_seq, SSD_HEADS, SSD_HD, SSD_STATE), c_last


def _pad_cols(w, width):
    return jnp.concatenate([w, jnp.zeros((w.shape[0], width - w.shape[1]), w.dtype)], axis=1)


def kernel(x_prompt, x_sample, state_ret, cache_dsa_k, cache_dsa_v, cache_dsa_kidx, cache_band_k, cache_band_v, state_ssm, state_conv, e_w_in, e_w_out, e_gn_g, e_gn_b, o_w_in, o_w_out, o_rel_bias, o_conv_w, o_conv_b, o_dt_bias, o_a_log, o_d_skip, o_norm_g, ln1_g, ln1_b, ln2_g, ln2_b, router_w, router_b, exp_w_gu, exp_b_gu, exp_w_dn, exp_b_dn):
    bp, tp, _ = x_prompt.shape
    bs, ts, _ = x_sample.shape
    past = cache_dsa_k.shape[2]
    n_p, n_s = bp * tp, bs * ts
    tm = math.gcd(512, math.gcd(n_p, n_s))
    assert tp % tm == 0 and tm % ts == 0 and ts == CHUNK

    x_parts = (x_prompt.reshape(n_p, D_MODEL), x_sample.reshape(n_s, D_MODEL))

    pe = _proj(x_parts, _pad_cols(e_w_in[0], EVEN_W).astype(BF16), tm)
    pos_p = jnp.arange(tp, dtype=I32)
    pos_s = past + jnp.arange(ts, dtype=I32)
    pos_tab = jnp.concatenate([pos_p, jnp.tile(pos_s, tm // ts)])
    tabs = (_rope_tables(pos_tab, RET_HEADS, RET_DK, RET_DK, RET_THETA),
            _rope_tables(pos_tab, RET_HEADS, RET_DK, RET_DK, RET_THETA, scale=RET_DK ** -0.5),
            _rope_tables(pos_tab, DSA_HEADS, DSA_HD, DSA_ROT, ROPE_THETA),
            _rope_tables(pos_tab, 1, IDX_DIM, DSA_ROT, ROPE_THETA, pad_to=LANE))
    (qa, ka, qb, kb, iq, ikw, q_st, iq_st, k_hm, ik_bf, v_t, iw_t,
     k5_p, v5_p, kidx_p, k5_s, v5_s, kidx_s) = _even_prep(pe, tabs, tm, n_p // tm, tp // tm)

    ya_p, ret_p = _retention(qa, ka, pe, jnp.zeros((bp, RET_HEADS, RET_DK, RET_DV), F32),
                             e_gn_g[0], e_gn_b[0], bp, tp, 0)
    ya_s, ret_s = _retention(qa, ka, pe, state_ret[0], e_gn_g[0], e_gn_b[0], bs, ts, n_p)

    topk_p = min(DSA_TOPK_MAX, tp // 4)
    qlim_p = (((pos_p // CHUNK) + 1) * CHUNK).reshape(1, tp)
    nq_p = tp // DSA_TQ
    nkb_p = ((jnp.arange(nq_p, dtype=I32) + 1) * DSA_TQ + DSA_TK - 1) // DSA_TK
    yb_p = _dsa(q_st, iq_st, iw_t, qlim_p, nkb_p, k_hm, v_t, ik_bf, bp, nq_p, tp, DSA_TQ, topk_p)

    s_len = past + ts
    s_pad = -(-s_len // (2 * DSA_TK)) * (2 * DSA_TK)
    topk_s = min(DSA_TOPK_MAX, s_len // 4)
    group = DSA_HEADS // DSA_KV_HEADS

    def cat_keys(cache, new, wd):
        zpad = jnp.zeros((bs, s_pad - s_len, wd), F32)
        return jnp.concatenate([cache, new.reshape(bs, ts, wd), zpad], axis=1)

    def pad_q(x):
        return jnp.concatenate([x, jnp.zeros((bs, DSA_TQ - ts) + x.shape[2:], x.dtype)], axis=1)

    ks = cat_keys(cache_dsa_k[0].reshape(bs, past, LANE), kb[n_p:], LANE)
    vs = cat_keys(cache_dsa_v[0].reshape(bs, past, LANE), pe[n_p:, 2176:2304], LANE)
    iks = cat_keys(cache_dsa_kidx[0], ikw[n_p:, :IDX_DIM], IDX_DIM)
    k_hm_s = ks.reshape(bs, s_pad, DSA_KV_HEADS, DSA_HD).transpose(2, 0, 1, 3).reshape(
        DSA_KV_HEADS, bs * s_pad, DSA_HD).astype(BF16)
    v_t_s = vs.reshape(bs * (s_pad // DSA_TK), DSA_TK, LANE).transpose(0, 2, 1)
    ones_s = jnp.ones((v_t_s.shape[0], DSA_VT_ROWS - DSA_HD, DSA_TK), F32)
    v_t_s = jnp.concatenate([v_t_s[:, :DSA_HD], ones_s, v_t_s[:, DSA_HD:], ones_s], axis=1).astype(BF16)
    ik_s = iks.reshape(bs * s_pad, IDX_DIM).astype(BF16)
    q_s = pad_q(qb[n_p:].reshape(bs, ts, DSA_KV_HEADS, group, DSA_HD))
    q_st_s = q_s.transpose(2, 0, 3, 1, 4).reshape(DSA_KV_HEADS, bs * group * DSA_TQ, DSA_HD)
    iq_s = pad_q(iq[n_p:].reshape(bs, ts, IDX_HEADS, IDX_DIM))
    iq_st_s = iq_s.transpose(0, 2, 1, 3).reshape(bs * IDX_HEADS * DSA_TQ, IDX_DIM)
    iw_t_s = pad_q(ikw[n_p:, IDX_DIM:IDX_DIM + 8].reshape(bs, ts, 8)).reshape(bs * DSA_TQ, 8).T
    qlim_s = jnp.full((1, DSA_TQ), s_len, I32)
    nkb_s = jnp.full((1,), -(-s_len // DSA_TK), I32)
    yb_s = _dsa(q_st_s, iq_st_s, iw_t_s, qlim_s, nkb_s, k_hm_s, v_t_s, ik_s, bs, 1, s_pad, ts, topk_s)

    h, h_packed = _outproj_ln((ya_p, ya_s), (yb_p, yb_s), e_w_out[0].astype(BF16), x_parts, ln1_g[0], ln1_b[0],
                              tm, n_p // tm)
    h = _moe_layer(h, h_packed, 0, router_w[0], router_b[0], exp_w_gu, exp_b_gu, exp_w_dn, exp_b_dn,
                   ln2_g[0], ln2_b[0], tm)

    po = _proj((h,), _pad_cols(o_w_in[0], ODD_W).astype(BF16), tm)
    tq_p = min(BAND_TQ, tp)
    nkb_band = BAND_PAST // tq_p + 1
    bias_p = _band_bias(o_rel_bias[0], tq_p, nkb_band * tq_p, -(nkb_band - 1) * tq_p)
    yc_p = _band(po, po, po, (0, 1, 2), bias_p, bp, tp, tq_p, tq_p, nkb_band, 0, tp // tq_p)
    band_len = cache_band_k.shape[2]
    kc_new = po[n_p:, 512:1024].reshape(bs, ts, 512)
    vc_new = po[n_p:, 1024:1536].reshape(bs, ts, 512)
    kcat = jnp.concatenate([cache_band_k[0].reshape(bs, band_len, 512), kc_new], axis=1)
    vcat = jnp.concatenate([cache_band_v[0].reshape(bs, band_len, 512), vc_new], axis=1)
    wlen = band_len + ts
    bias_s = _band_bias(o_rel_bias[0], ts, wlen, -band_len)
    yc_s = _band(po, kcat.reshape(bs * wlen, 512), vcat.reshape(bs * wlen, 512), (0, 0, 0), bias_s,
                 bs, ts, ts, wlen, 1, n_p, 1)

    ssd_w = (o_conv_w[0], o_conv_b[0], o_dt_bias[0], o_a_log[0], o_d_skip[0], o_norm_g[0])
    cdim = o_conv_w.shape[2]
    yd_p, ssm_p, conv_p = _ssd(po, jnp.zeros((bp, SSD_HEADS, SSD_HD, SSD_STATE), F32),
                               jnp.zeros((bp, SSD_CONV - 1, cdim), F32), *ssd_w, bp, tp, 0)
    yd_s, ssm_s, conv_s = _ssd(po, state_ssm[0], state_conv[0], *ssd_w, bs, ts, n_p)

    h, h_packed = _outproj_ln((yc_p, yc_s), (yd_p, yd_s), o_w_out[0].astype(BF16), (h,), ln1_g[1], ln1_b[1],
                              tm, n_p // tm)
    h_p, h_s = _moe_layer(h, h_packed, 1, router_w[1], router_b[1], exp_w_gu, exp_b_gu, exp_w_dn, exp_b_dn,
                          ln2_g[1], ln2_b[1], tm, split_rows=n_p)

    keep = min(BAND_PAST, tp)
    tmb = math.gcd(keep, 256)
    kept = (jnp.arange(bp, dtype=I32)[:, None] * (tp // tmb) + (tp - keep) // tmb
            + jnp.arange(keep // tmb, dtype=I32)[None, :]).reshape(-1)
    kc_p, vc_p = _head_rows(po, kept, tmb)
    kc_s, vc_s = _head_rows(po, n_p // ts + jnp.arange(bs, dtype=I32), ts)
    return (h_p.reshape(bp, tp, D_MODEL), h_s.reshape(bs, ts, D_MODEL),
            ret_p[None],
            k5_p.reshape(1, bp, tp, DSA_KV_HEADS, DSA_HD),
            v5_p.reshape(1, bp, tp, DSA_KV_HEADS, DSA_HD),
            kidx_p.reshape(1, bp, tp, IDX_DIM),
            kc_p.reshape(1, bp, keep, BAND_HEADS, BAND_HD), vc_p.reshape(1, bp, keep, BAND_HEADS, BAND_HD),
            ssm_p[None], conv_p[None],
            ret_s[None],
            k5_s.reshape(1, bs, ts, DSA_KV_HEADS, DSA_HD),
            v5_s.reshape(1, bs, ts, DSA_KV_HEADS, DSA_HD),
            kidx_s.reshape(1, bs, ts, IDX_DIM),
            kc_s.reshape(1, bs, ts, BAND_HEADS, BAND_HD), vc_s.reshape(1, bs, ts, BAND_HEADS, BAND_HD),
            ssm_s[None], conv_s[None])
```
